```python
import math
import jax, jax.numpy as jnp
from jax import lax
import numpy as np

D_MODEL = 2048
BATCH = 4
SEQ = 4096
DEPTH = 2

N_MEM = 256
HEAD_DIM = 128
N_HEADS_TOTAL = D_MODEL // HEAD_DIM
N_MEM_HEADS = 4
N_SELF_HEADS = N_HEADS_TOTAL - N_MEM_HEADS
SELF_WIDTH = N_SELF_HEADS * HEAD_DIM
MEM_WIDTH = N_MEM_HEADS * HEAD_DIM
DIFF_QK_DIM = HEAD_DIM // 2
DSA_KV_HEADS = 4
DSA_GROUP = N_SELF_HEADS // DSA_KV_HEADS
IDX_HEADS = 16
IDX_DIM = 64
TOPK_MAX = 256
NUM_BUCKETS = 32
MAX_DISTANCE = 128
D_FF = 5632
Q_BLOCK = 128
N_MIXERS = 2
N_A = (DEPTH + 1) // 2
N_B = DEPTH // 2
EPS = 1e-6
NEG_INF = -1e30
A_WIDTHS = (SELF_WIDTH, SELF_WIDTH, SELF_WIDTH, MEM_WIDTH)
B_WIDTHS = (SELF_WIDTH, DSA_KV_HEADS * HEAD_DIM, DSA_KV_HEADS * HEAD_DIM,
            IDX_HEADS * IDX_DIM, IDX_DIM, IDX_HEADS, MEM_WIDTH)
A_IN_WIDTH = sum(A_WIDTHS)
B_IN_WIDTH = sum(B_WIDTHS)

kernel_name = "hybrid_diffattn_dsa_macaron"


def rms_norm(x, g):
    xf = x.astype(jnp.float32)
    y = xf * lax.rsqrt(jnp.mean(xf * xf, axis=-1, keepdims=True) + EPS)
    return (y * g.astype(jnp.float32)).astype(x.dtype)


def swiglu(h, w_gate, w_up, w_down):
    return (jax.nn.silu(h @ w_gate) * (h @ w_up)) @ w_down


def _split(x, widths):
    cuts = [int(c) for c in np.cumsum(widths)[:-1]]
    return jnp.split(x, cuts, axis=-1)


def t5_bucket(dist):
    n = jnp.maximum(dist, 0)
    max_exact = NUM_BUCKETS // 2
    nf = jnp.maximum(n, 1).astype(jnp.float32)
    large = max_exact + (jnp.log(nf / max_exact) / math.log(MAX_DISTANCE / max_exact)
                         * (NUM_BUCKETS - max_exact)).astype(jnp.int32)
    large = jnp.minimum(large, NUM_BUCKETS - 1)
    return jnp.where(n < max_exact, n, large)


def diff_attention(q, k, v, lam, rel_bias):
    B, T, H, _, dq = q.shape
    nb = T // Q_BLOCK
    scale = dq ** -0.5
    kpos = jnp.arange(T)
    qb = jnp.moveaxis(q.reshape(B, nb, Q_BLOCK, H, 2, dq), 1, 0)

    def block(args):
        q_blk, blk = args
        qpos = blk * Q_BLOCK + jnp.arange(Q_BLOCK)
        dist = qpos[:, None] - kpos[None, :]
        bias = jnp.transpose(rel_bias[t5_bucket(dist)], (2, 0, 1)).astype(jnp.float32)
        s = jnp.einsum("bqhmd,bkhmd->bhmqk", q_blk, k).astype(jnp.float32) * scale
        s = jnp.where(dist >= 0, s + bias[None, :, None], NEG_INF)
        p = jax.nn.softmax(s, axis=-1)
        a = p[:, :, 0] - lam * p[:, :, 1]
        return jnp.einsum("bhqk,bkhd->bqhd", a.astype(v.dtype), v)

    out = lax.map(block, (qb, jnp.arange(nb)))
    return jnp.moveaxis(out, 0, 1).reshape(B, T, H, v.shape[-1])


def dsa_attention(q, k, v, iq, ik, iw, rel_bias):
    B, T, H, dh = q.shape
    G = k.shape[2]
    R = H // G
    topk = min(TOPK_MAX, T // 4)
    nb = T // Q_BLOCK
    scale = dh ** -0.5
    idx_scale = IDX_DIM ** -0.5
    w_scale = IDX_HEADS ** -0.5
    kpos = jnp.arange(T)
    qb = jnp.moveaxis(q.reshape(B, nb, Q_BLOCK, G, R, dh), 1, 0)
    iqb = jnp.moveaxis(iq.reshape(B, nb, Q_BLOCK, IDX_HEADS, IDX_DIM), 1, 0)
    iwb = jnp.moveaxis(iw.reshape(B, nb, Q_BLOCK, IDX_HEADS), 1, 0)
    gather = jax.vmap(lambda tb, ib: tb[ib])

    def block(args):
        q_blk, iq_blk, iw_blk, blk = args
        qpos = blk * Q_BLOCK + jnp.arange(Q_BLOCK)
        causal = qpos[:, None] >= kpos[None, :]
        logits = jnp.einsum("bqjd,bkd->bqjk", iq_blk, ik).astype(jnp.float32) * idx_scale
        score = jnp.einsum("bqj,bqjk->bqk", iw_blk.astype(jnp.float32) * w_scale,
                           jax.nn.relu(logits))
        score = jnp.where(causal[None], score, NEG_INF)
        _, idx = lax.top_k(score, topk)
        valid = idx <= qpos[None, :, None]
        k_sel = gather(k, idx)
        v_sel = gather(v, idx)
        s = jnp.einsum("bqgrd,bqkgd->bgrqk", q_blk, k_sel).astype(jnp.float32) * scale
        bias = rel_bias[t5_bucket(qpos[None, :, None] - idx)].astype(jnp.float32)
        bias = jnp.transpose(bias.reshape(B, Q_BLOCK, topk, G, R), (0, 3, 4, 1, 2))
        s = jnp.where(valid[:, None, None], s + bias, NEG_INF)
        p = jax.nn.softmax(s, axis=-1)
        o = jnp.einsum("bgrqk,bqkgd->bqgrd", p.astype(v.dtype), v_sel)
        return o.reshape(B, Q_BLOCK, H, dh)

    out = lax.map(block, (qb, iqb, iwb, jnp.arange(nb)))
    return jnp.moveaxis(out, 0, 1).reshape(B, T, H, dh)


def memory_attention(qm, km, vm):
    scale = qm.shape[-1] ** -0.5
    s = jnp.einsum("bthd,bnhd->bhtn", qm, km).astype(jnp.float32) * scale
    p = jax.nn.softmax(s, axis=-1)
    return jnp.einsum("bhtn,bnhd->bthd", p.astype(vm.dtype), vm)


def _nrm(key, shape, scale):
    return jax.random.normal(key, shape, jnp.float32) * scale


def setup_inputs(seed: int = 0) -> dict:
    key = jax.random.key(seed)
    k = jax.random.split(key, 32)
    D, F = D_MODEL, D_FF
    return {
        "x": _nrm(k[0], (BATCH, SEQ, D), 1.0),
        "mem": _nrm(k[1], (BATCH, N_MEM, D), 1.0),
        "rel_bias": _nrm(k[2], (NUM_BUCKETS, N_SELF_HEADS), 0.2),
        "ffn1_g": 1.0 + _nrm(k[3], (DEPTH, D), 0.02),
        "ffn1_w_gate": _nrm(k[4], (DEPTH, D, F), D ** -0.5),
        "ffn1_w_up": _nrm(k[5], (DEPTH, D, F), D ** -0.5),
        "ffn1_w_down": _nrm(k[6], (DEPTH, F, D), F ** -0.5),
        "ffn2_g": 1.0 + _nrm(k[7], (DEPTH, D), 0.02),
        "ffn2_w_gate": _nrm(k[8], (DEPTH, D, F), D ** -0.5),
        "ffn2_w_up": _nrm(k[9], (DEPTH, D, F), D ** -0.5),
        "ffn2_w_down": _nrm(k[10], (DEPTH, F, D), F ** -0.5),
        "mix_g": 1.0 + _nrm(k[11], (DEPTH, D), 0.02),
        "mem_g": 1.0 + _nrm(k[12], (DEPTH, D), 0.02),
        "mem_w_kv": _nrm(k[13], (DEPTH, D, 2 * MEM_WIDTH), D ** -0.5),
        "mem_gq": 1.0 + _nrm(k[14], (DEPTH, HEAD_DIM), 0.02),
        "mem_gk": 1.0 + _nrm(k[15], (DEPTH, HEAD_DIM), 0.02),
        "w_out": _nrm(k[16], (DEPTH, D, D), D ** -0.5),
        "a_w_in": _nrm(k[17], (N_A, D, A_IN_WIDTH), D ** -0.5),
        "a_gq": 1.0 + _nrm(k[18], (N_A, DIFF_QK_DIM), 0.02),
        "a_gk": 1.0 + _nrm(k[19], (N_A, DIFF_QK_DIM), 0.02),
        "a_lam_q1": _nrm(k[20], (N_A, DIFF_QK_DIM), 0.1),
        "a_lam_k1": _nrm(k[21], (N_A, DIFF_QK_DIM), 0.1),
        "a_lam_q2": _nrm(k[22], (N_A, DIFF_QK_DIM), 0.1),
        "a_lam_k2": _nrm(k[23], (N_A, DIFF_QK_DIM), 0.1),
        "a_g_sub": 1.0 + _nrm(k[24], (N_A, HEAD_DIM), 0.02),
        "b_w_in": _nrm(k[25], (N_B, D, B_IN_WIDTH), D ** -0.5),
        "b_gq": 1.0 + _nrm(k[26], (N_B, HEAD_DIM), 0.02),
        "b_gk": 1.0 + _nrm(k[27], (N_B, HEAD_DIM), 0.02),
    }


def reference(x, mem, rel_bias, ffn1_g, ffn1_w_gate, ffn1_w_up, ffn1_w_down,
              ffn2_g, ffn2_w_gate, ffn2_w_up, ffn2_w_down, mix_g, mem_g, mem_w_kv,
              mem_gq, mem_gk, w_out, a_w_in, a_gq, a_gk, a_lam_q1, a_lam_k1,
              a_lam_q2, a_lam_k2, a_g_sub, b_w_in, b_gq, b_gk):
    B, T, _ = x.shape
    N = mem.shape[1]
    for i in range(DEPTH):
        x = x + 0.5 * swiglu(rms_norm(x, ffn1_g[i]), ffn1_w_gate[i], ffn1_w_up[i], ffn1_w_down[i])

        km, vm = _split(rms_norm(mem, mem_g[i]) @ mem_w_kv[i], (MEM_WIDTH, MEM_WIDTH))
        km = rms_norm(km.reshape(B, N, N_MEM_HEADS, HEAD_DIM), mem_gk[i])
        vm = vm.reshape(B, N, N_MEM_HEADS, HEAD_DIM)

        h = rms_norm(x, mix_g[i])
        if i % N_MIXERS == 0:
            j = i // N_MIXERS
            q, k, v, qm = _split(h @ a_w_in[j], A_WIDTHS)
            q = rms_norm(q.reshape(B, T, N_SELF_HEADS, 2, DIFF_QK_DIM), a_gq[j])
            k = rms_norm(k.reshape(B, T, N_SELF_HEADS, 2, DIFF_QK_DIM), a_gk[j])
            v = v.reshape(B, T, N_SELF_HEADS, HEAD_DIM)
            lambda_init = 0.8 - 0.6 * math.exp(-0.3 * i)
            lam = (jnp.exp(jnp.sum(a_lam_q1[j].astype(jnp.float32) * a_lam_k1[j].astype(jnp.float32)))
                   - jnp.exp(jnp.sum(a_lam_q2[j].astype(jnp.float32) * a_lam_k2[j].astype(jnp.float32)))
                   + lambda_init)
            y_self = diff_attention(q, k, v, lam, rel_bias)
            y_self = rms_norm(y_self, a_g_sub[j]) * (1.0 - lambda_init)
        else:
            j = i // N_MIXERS
            q, k, v, iq, ik, iw, qm = _split(h @ b_w_in[j], B_WIDTHS)
            q = rms_norm(q.reshape(B, T, N_SELF_HEADS, HEAD_DIM), b_gq[j])
            k = rms_norm(k.reshape(B, T, DSA_KV_HEADS, HEAD_DIM), b_gk[j])
            v = v.reshape(B, T, DSA_KV_HEADS, HEAD_DIM)
            iq = iq.reshape(B, T, IDX_HEADS, IDX_DIM)
            y_self = dsa_attention(q, k, v, iq, ik, iw, rel_bias)

        qm = rms_norm(qm.reshape(B, T, N_MEM_HEADS, HEAD_DIM), mem_gq[i])
        y_mem = memory_attention(qm, km, vm)
        y = jnp.concatenate([y_self.reshape(B, T, SELF_WIDTH),
                             y_mem.reshape(B, T, MEM_WIDTH)], axis=-1)
        x = x + y @ w_out[i]

        x = x + 0.5 * swiglu(rms_norm(x, ffn2_g[i]), ffn2_w_gate[i], ffn2_w_up[i], ffn2_w_down[i])
    return x
```

```python
import functools
import math

import jax
import jax.numpy as jnp
from jax import lax
from jax.experimental import pallas as pl
from jax.experimental.pallas import tpu as pltpu

HEAD_DIM = 128
N_SELF_HEADS = 12
N_MEM_HEADS = 4
SELF_WIDTH = N_SELF_HEADS * HEAD_DIM
MEM_WIDTH = N_MEM_HEADS * HEAD_DIM
DIFF_QK_DIM = HEAD_DIM // 2
DSA_KV_HEADS = 4
DSA_GROUP = N_SELF_HEADS // DSA_KV_HEADS
IDX_HEADS = 16
IDX_DIM = 64
TOPK_MAX = 256
NUM_BUCKETS = 32
MAX_DISTANCE = 128
EPS = 1e-6
NEG_INF = -1e30
INT32_MIN = -(2 ** 31)

LANES = 128
VMEM_LIMIT = 56 * 1024 * 1024

F32 = jnp.float32
BF16 = jnp.bfloat16


def _dot(a, b):
    return jnp.dot(a, b, preferred_element_type=F32)


def _dot_nt(a, b):
    return lax.dot_general(a, b, (((1,), (1,)), ((), ())), preferred_element_type=F32)


def _params(semantics):
    return pltpu.CompilerParams(dimension_semantics=semantics, vmem_limit_bytes=VMEM_LIMIT)


def _rms_rows(x, g):
    ms = jnp.mean(x * x, axis=-1, keepdims=True)
    return x * lax.rsqrt(ms + EPS) * g


def _ffn_body(x_ref, g_ref, wg_ref, wu_ref, wd_ref, o_ref, h_ref):
    @pl.when(pl.program_id(1) == 0)
    def _():
        x = x_ref[...]
        h_ref[...] = _rms_rows(x, g_ref[...]).astype(BF16)
        o_ref[...] = x

    h = h_ref[...]
    gate = _dot(h, wg_ref[...])
    up = _dot(h, wu_ref[...])
    act = (gate * jax.nn.sigmoid(gate) * up).astype(BF16)
    o_ref[...] += _dot(act, wd_ref[...])


def _ffn(x2, g, wg, wu, wd_half):
    M, D = x2.shape
    F = wg.shape[1]
    tm = min(512, M)
    tf = min(512, F)
    return pl.pallas_call(
        _ffn_body,
        grid=(M // tm, F // tf),
        in_specs=[
            pl.BlockSpec((tm, D), lambda i, f: (i, 0)),
            pl.BlockSpec((1, D), lambda i, f: (0, 0)),
            pl.BlockSpec((D, tf), lambda i, f: (0, f)),
            pl.BlockSpec((D, tf), lambda i, f: (0, f)),
            pl.BlockSpec((tf, D), lambda i, f: (f, 0)),
        ],
        out_specs=pl.BlockSpec((tm, D), lambda i, f: (i, 0)),
        out_shape=jax.ShapeDtypeStruct((M, D), F32),
        scratch_shapes=[pltpu.VMEM((tm, D), BF16)],
        compiler_params=_params(("parallel", "arbitrary")),
        name="ffn",
    )(x2, g.reshape(1, D), wg, wu, wd_half)


def _proj_body(types_ref, x_ref, g_ref, w_ref, gmat_ref, gain_ref, o_ref, h_ref, *, tn):
    j = pl.program_id(1)

    @pl.when(j == 0)
    def _():
        h_ref[...] = _rms_rows(x_ref[...], g_ref[...]).astype(BF16)

    y = _dot(h_ref[...], w_ref[...])
    kind = types_ref[j]

    @pl.when(kind == 0)
    def _():
        o_ref[...] = (y * gain_ref[...]).astype(o_ref.dtype)

    @pl.when(kind != 0)
    def _():
        gmat = gmat_ref[...]
        for c in range(tn // LANES):
            sl = slice(c * LANES, (c + 1) * LANES)
            yc = y[:, sl]
            ms = _dot((yc * yc).astype(BF16), gmat)
            o_ref[:, sl] = (yc * lax.rsqrt(ms + EPS) * gain_ref[:, sl]).astype(o_ref.dtype)


def _group_mats():
    r = jnp.arange(LANES)
    same64 = (r[:, None] // DIFF_QK_DIM) == (r[None, :] // DIFF_QK_DIM)
    return jnp.stack([
        jnp.zeros((LANES, LANES), F32),
        jnp.where(same64, 1.0 / DIFF_QK_DIM, 0.0),
        jnp.full((LANES, LANES), 1.0 / HEAD_DIM, F32),
    ]).astype(BF16)


def _proj(x2, g, w, kinds, gain, out_dtype, tn):
    M, D = x2.shape
    N = w.shape[1]
    tm = min(1024, M)
    kinds = jnp.asarray(kinds, jnp.int32)
    grid_spec = pltpu.PrefetchScalarGridSpec(
        num_scalar_prefetch=1,
        grid=(M // tm, N // tn),
        in_specs=[
            pl.BlockSpec((tm, D), lambda i, j, t: (i, 0)),
            pl.BlockSpec((1, D), lambda i, j, t: (0, 0)),
            pl.BlockSpec((D, tn), lambda i, j, t: (0, j)),
            pl.BlockSpec((None, LANES, LANES), lambda i, j, t: (t[j], 0, 0)),
            pl.BlockSpec((1, tn), lambda i, j, t: (0, j)),
        ],
        out_specs=pl.BlockSpec((tm, tn), lambda i, j, t: (i, j)),
        scratch_shapes=[pltpu.VMEM((tm, D), BF16)],
    )
    return pl.pallas_call(
        functools.partial(_proj_body, tn=tn),
        grid_spec=grid_spec,
        out_shape=jax.ShapeDtypeStruct((M, N), out_dtype),
        compiler_params=_params(("parallel", "arbitrary")),
        name="proj",
    )(kinds, x2, g.reshape(1, D), w, _group_mats(), gain.reshape(1, N))


def _bias_body(rb_ref, o_ref, *, tq, tk):
    h = pl.program_id(0)
    r = lax.broadcasted_iota(jnp.int32, (tq, tk), 0)
    c = lax.broadcasted_iota(jnp.int32, (tq, tk), 1)
    max_exact = NUM_BUCKETS // 2
    far = rb_ref[NUM_BUCKETS - 1, h]
    for off in range(2):
        dist = r - c + off * tk
        n = jnp.maximum(dist, 0)
        nf = jnp.maximum(n, 1).astype(F32)
        large = max_exact + (jnp.log(nf / max_exact) / math.log(MAX_DISTANCE / max_exact)
                             * (NUM_BUCKETS - max_exact)).astype(jnp.int32)
        large = jnp.minimum(large, NUM_BUCKETS - 1)
        bucket = jnp.where(n < max_exact, n, large)
        val = jnp.zeros((tq, tk), F32)
        for b in range(NUM_BUCKETS):
            val = jnp.where(bucket == b, rb_ref[b, h], val)
        o_ref[off] = jnp.where(dist >= 0, val - far, NEG_INF)


def _bias_tiles(rel_bias, tq, tk):
    assert tq == tk and tk >= MAX_DISTANCE
    return pl.pallas_call(
        functools.partial(_bias_body, tq=tq, tk=tk),
        grid=(N_SELF_HEADS,),
        in_specs=[pl.BlockSpec(memory_space=pltpu.SMEM)],
        out_specs=pl.BlockSpec((None, 2, tq, tk), lambda h: (h, 0, 0, 0)),
        out_shape=jax.ShapeDtypeStruct((N_SELF_HEADS, 2, tq, tk), F32),
        compiler_params=_params(("arbitrary",)),
        name="bias_tiles",
    )(rel_bias)


def _softmax_step(carry, s, v):
    m, l, acc = carry
    m_new = jnp.maximum(m, jnp.max(s, axis=-1, keepdims=True))
    alpha = jnp.exp(m - m_new)
    p = jnp.exp(s - m_new)
    l = alpha * l + jnp.sum(p, axis=-1, keepdims=True)
    acc = alpha * acc + _dot(p.astype(BF16), v)
    return m_new, l, acc


def _softmax_init(rows):
    return (jnp.full((rows, 1), NEG_INF, F32), jnp.zeros((rows, 1), F32),
            jnp.zeros((rows, HEAD_DIM), F32))


def _diff_body(lq1_ref, lk1_ref, lq2_ref, lk2_ref, q_ref, k_ref, v_ref, tab_ref, gsub_ref, o_ref,
               *, tq, tk, lambda_init):
    i = pl.program_id(2)
    q = q_ref[...]
    lane = lax.broadcasted_iota(jnp.int32, q.shape, 1)
    zero = jnp.zeros_like(q)
    qq = jnp.concatenate([jnp.where(lane < DIFF_QK_DIM, q, zero),
                          jnp.where(lane >= DIFF_QK_DIM, q, zero)], axis=0)

    def scores(j):
        rows = pl.ds(pl.multiple_of(j * tk, tk), tk)
        return _dot_nt(qq, k_ref[rows, :]), v_ref[rows, :]

    def far_step(j, carry):
        s, v = scores(j)
        return _softmax_step(carry, s, v)

    def near_step(j, carry, tab):
        s, v = scores(j)
        return _softmax_step(carry, s + jnp.concatenate([tab, tab], axis=0), v)

    n_far = jnp.maximum(i - 1, 0)
    carry = lax.fori_loop(0, n_far, far_step, _softmax_init(2 * tq))
    carry = lax.fori_loop(n_far, i, lambda j, c: near_step(j, c, tab_ref[1]), carry)
    _, l, acc = near_step(i, carry, tab_ref[0])

    lam = (jnp.exp(jnp.sum(lq1_ref[...] * lk1_ref[...], axis=-1, keepdims=True))
           - jnp.exp(jnp.sum(lq2_ref[...] * lk2_ref[...], axis=-1, keepdims=True)) + lambda_init)
    o = acc[:tq] / l[:tq] - lam * (acc[tq:] / l[tq:])
    o_ref[...] = (_rms_rows(o, gsub_ref[...]) * (1.0 - lambda_init)).astype(o_ref.dtype)


def _diff_attn(p3, tab, lq1, lk1, lq2, lk2, g_sub, lambda_init, tq):
    B, T, _ = p3.shape
    H = N_SELF_HEADS
    vec = lambda a: a.reshape(1, -1).astype(F32)
    small = lambda n: pl.BlockSpec((1, n), lambda b, h, i: (0, 0))
    return pl.pallas_call(
        functools.partial(_diff_body, tq=tq, tk=tq, lambda_init=lambda_init),
        grid=(B, H, T // tq),
        in_specs=[
            small(DIFF_QK_DIM), small(DIFF_QK_DIM), small(DIFF_QK_DIM), small(DIFF_QK_DIM),
            pl.BlockSpec((None, tq, HEAD_DIM), lambda b, h, i: (b, i, h)),
            pl.BlockSpec((None, T, HEAD_DIM), lambda b, h, i: (b, 0, H + h)),
            pl.BlockSpec((None, T, HEAD_DIM), lambda b, h, i: (b, 0, 2 * H + h)),
            pl.BlockSpec((None, 2, tq, tq), lambda b, h, i: (h, 0, 0, 0)),
            small(HEAD_DIM),
        ],
        out_specs=pl.BlockSpec((None, tq, HEAD_DIM), lambda b, h, i: (b, i, h)),
        out_shape=jax.ShapeDtypeStruct((B, T, SELF_WIDTH), BF16),
        compiler_params=_params(("parallel", "parallel", "arbitrary")),
        name="diff_attn",
    )(vec(lq1), vec(lk1), vec(lq2), vec(lk2), p3, p3, p3, tab, vec(g_sub))


def _sortable(x):
    bits = pltpu.bitcast(x, jnp.int32)
    return jnp.where(bits < 0, bits ^ jnp.int32(0x7FFFFFFF), bits)


def _dsa_select(iq_ref, ik_ref, iw_ref, key_ref, madd_ref, iqm_ref, i, *, tq, tk, topk):
    n_tiles = i + 1
    lane = lax.broadcasted_iota(jnp.int32, (tq, LANES), 1)
    for hh in range(IDX_HEADS):
        pair = iq_ref[:, (hh // 2) * LANES:(hh // 2 + 1) * LANES]
        keep = (lane < IDX_DIM) if hh % 2 == 0 else (lane >= IDX_DIM)
        iqm_ref[hh] = jnp.where(keep, pair, jnp.zeros_like(pair))

    row = lax.broadcasted_iota(jnp.int32, (tq, tk), 0)
    col = lax.broadcasted_iota(jnp.int32, (tq, tk), 1)

    def causal(j):
        return (i * tq + row) >= (j * tk + col)

    def score_tile(j, _):
        ik = ik_ref[pl.ds(pl.multiple_of(j * tk, tk), tk), :]
        sc = jnp.zeros((tq, tk), F32)
        for hh in range(IDX_HEADS):
            sc = sc + jnp.maximum(_dot_nt(iqm_ref[hh], ik), 0.0) * iw_ref[:, hh:hh + 1]
        key_ref[j] = _sortable(jnp.where(causal(j), sc, NEG_INF))
        return 0

    lax.fori_loop(0, n_tiles, score_tile, 0)

    def count(pred):
        def body(j, part):
            hit = jnp.where(pred(key_ref[j]), 1.0, 0.0)
            return part + jnp.sum(hit, axis=-1, keepdims=True)
        return lax.fori_loop(0, n_tiles, body, jnp.zeros((tq, 1), F32))

    kf = float(topk)
    base = jnp.where(count(lambda k: k >= 0) >= kf, jnp.int32(0), jnp.int32(INT32_MIN))

    def bit_step(b, base):
        cand = base | lax.shift_left(jnp.int32(1), jnp.int32(30) - b)
        return jnp.where(count(lambda k: k >= cand) >= kf, cand, base)

    thr = lax.fori_loop(0, 31, bit_step, base)
    need = kf - count(lambda k: k > thr)

    upper = jnp.where(row < col, 1.0, 0.0).astype(BF16)

    def mask_tile(j, seen):
        key = key_ref[j]
        eq = jnp.where(key == thr, 1.0, 0.0)
        rank = seen + _dot(eq.astype(BF16), upper)
        take = (key > thr) | ((key == thr) & (rank < need))
        madd_ref[j] = jnp.where(take & causal(j), 0.0, NEG_INF)
        return seen + jnp.sum(eq, axis=-1, keepdims=True)

    lax.fori_loop(0, n_tiles, mask_tile, jnp.zeros((tq, 1), F32))


def _dsa_body(q_ref, k_ref, v_ref, iq_ref, ik_ref, iw_ref, tab_ref, o_ref,
              key_ref, madd_ref, iqm_ref, *, tq, tk, topk):
    i = pl.program_id(1)

    @pl.when(pl.program_id(2) == 0)
    def _():
        _dsa_select(iq_ref, ik_ref, iw_ref, key_ref, madd_ref, iqm_ref, i, tq=tq, tk=tk, topk=topk)

    q = q_ref[...]

    def step(j, carry, tab):
        rows = pl.ds(pl.multiple_of(j * tk, tk), tk)
        s = _dot_nt(q, k_ref[rows, :]) + madd_ref[j]
        if tab is not None:
            s = s + tab
        return _softmax_step(carry, s, v_ref[rows, :])

    n_far = jnp.maximum(i - 1, 0)
    carry = lax.fori_loop(0, n_far, lambda j, c: step(j, c, None), _softmax_init(tq))
    carry = lax.fori_loop(n_far, i, lambda j, c: step(j, c, tab_ref[1]), carry)
    _, l, acc = step(i, carry, tab_ref[0])
    o_ref[...] = (acc / l).astype(o_ref.dtype)


def _dsa_attn(p3, ik2, iw, tab, tq):
    B, T, _ = p3.shape
    H, R = N_SELF_HEADS, DSA_GROUP
    topk = min(TOPK_MAX, T // 4)
    nq = T // tq
    iq_w = IDX_HEADS * IDX_DIM
    q0 = iq_w // HEAD_DIM
    k0 = q0 + H
    v0 = k0 + DSA_KV_HEADS
    return pl.pallas_call(
        functools.partial(_dsa_body, tq=tq, tk=tq, topk=topk),
        grid=(B, nq, H),
        in_specs=[
            pl.BlockSpec((None, tq, HEAD_DIM), lambda b, i, h: (b, i, q0 + h)),
            pl.BlockSpec((None, T, HEAD_DIM), lambda b, i, h: (b, 0, k0 + h // R)),
            pl.BlockSpec((None, T, HEAD_DIM), lambda b, i, h: (b, 0, v0 + h // R)),
            pl.BlockSpec((None, tq, iq_w), lambda b, i, h: (b, i, 0)),
            pl.BlockSpec((None, T, LANES), lambda b, i, h: (b, 0, 0)),
            pl.BlockSpec((None, tq, IDX_HEADS), lambda b, i, h: (b, i, 0)),
            pl.BlockSpec((None, 2, tq, tq), lambda b, i, h: (h, 0, 0, 0)),
        ],
        out_specs=pl.BlockSpec((None, tq, HEAD_DIM), lambda b, i, h: (b, i, h)),
        out_shape=jax.ShapeDtypeStruct((B, T, SELF_WIDTH), BF16),
        scratch_shapes=[
            pltpu.VMEM((nq, tq, tq), jnp.int32),
            pltpu.VMEM((nq, tq, tq), F32),
            pltpu.VMEM((IDX_HEADS, tq, LANES), BF16),
        ],
        compiler_params=_params(("parallel", "arbitrary", "arbitrary")),
        name="dsa_attn",
    )(p3, p3, p3, p3, ik2, iw, tab)


def _mem_body(q_ref, k_ref, v_ref, o_ref):
    for h in range(N_MEM_HEADS):
        sl = slice(h * HEAD_DIM, (h + 1) * HEAD_DIM)
        s = _dot_nt(q_ref[:, sl], k_ref[:, sl])
        p = jnp.exp(s - jnp.max(s, axis=-1, keepdims=True))
        o = _dot(p.astype(BF16), v_ref[:, sl]) / jnp.sum(p, axis=-1, keepdims=True)
        o_ref[:, sl] = o.astype(o_ref.dtype)


def _mem_attn(p3, kv3, qm_block):
    B, T, _ = p3.shape
    N = kv3.shape[1]
    tq = min(512, T)
    return pl.pallas_call(
        _mem_body,
        grid=(B, T // tq),
        in_specs=[
            pl.BlockSpec((None, tq, MEM_WIDTH), lambda b, i: (b, i, qm_block)),
            pl.BlockSpec((None, N, MEM_WIDTH), lambda b, i: (b, 0, 0)),
            pl.BlockSpec((None, N, MEM_WIDTH), lambda b, i: (b, 0, 1)),
        ],
        out_specs=pl.BlockSpec((None, tq, MEM_WIDTH), lambda b, i: (b, i, 0)),
        out_shape=jax.ShapeDtypeStruct((B, T, MEM_WIDTH), BF16),
        compiler_params=_params(("parallel", "arbitrary")),
        name="mem_attn",
    )(p3, kv3, kv3)


def _out_body(ys_ref, ym_ref, ws_ref, wm_ref, x_ref, o_ref):
    o_ref[...] = x_ref[...] + _dot(ys_ref[...], ws_ref[...]) + _dot(ym_ref[...], wm_ref[...])


def _out_proj(ys, ym, w_self, w_mem, x2):
    M, D = x2.shape
    tm = min(512, M)
    return pl.pallas_call(
        _out_body,
        grid=(M // tm,),
        in_specs=[
            pl.BlockSpec((tm, SELF_WIDTH), lambda i: (i, 0)),
            pl.BlockSpec((tm, MEM_WIDTH), lambda i: (i, 0)),
            pl.BlockSpec((SELF_WIDTH, D), lambda i: (0, 0)),
            pl.BlockSpec((MEM_WIDTH, D), lambda i: (0, 0)),
            pl.BlockSpec((tm, D), lambda i: (i, 0)),
        ],
        out_specs=pl.BlockSpec((tm, D), lambda i: (i, 0)),
        out_shape=jax.ShapeDtypeStruct((M, D), F32),
        compiler_params=_params(("parallel",)),
        name="out_proj",
    )(ys, ym, w_self, w_mem, x2)


def kernel(x, mem, rel_bias, ffn1_g, ffn1_w_gate, ffn1_w_up, ffn1_w_down, ffn2_g, ffn2_w_gate,
           ffn2_w_up, ffn2_w_down, mix_g, mem_g, mem_w_kv, mem_gq, mem_gk, w_out, a_w_in, a_gq, a_gk,
           a_lam_q1, a_lam_k1, a_lam_q2, a_lam_k2, a_g_sub, b_w_in, b_gq, b_gk):
    B, T, D = x.shape
    N = mem.shape[1]
    depth = ffn1_g.shape[0]
    M = B * T
    tq = min(256, T)
    head_scale = HEAD_DIM ** -0.5
    ones = lambda n: jnp.ones((n,), F32)
    bf = lambda w: w.astype(BF16)

    x2 = x.reshape(M, D)
    mem2 = mem.reshape(B * N, D)
    tab = _bias_tiles(rel_bias, tq, tq)

    for i in range(depth):
        x2 = _ffn(x2, ffn1_g[i], bf(ffn1_w_gate[i]), bf(ffn1_w_up[i]), bf(0.5 * ffn1_w_down[i]))

        kv_gain = jnp.concatenate([jnp.tile(mem_gk[i], N_MEM_HEADS), ones(MEM_WIDTH)])
        kv3 = _proj(mem2, mem_g[i], bf(mem_w_kv[i]), [2, 0], kv_gain, BF16, MEM_WIDTH)
        kv3 = kv3.reshape(B, N, 2 * MEM_WIDTH)
        qm_gain = jnp.tile(mem_gq[i], N_MEM_HEADS) * head_scale
        j = i // 2

        if i % 2 == 0:
            n_grp = SELF_WIDTH // DIFF_QK_DIM
            gain = jnp.concatenate([jnp.tile(a_gq[j], n_grp) * DIFF_QK_DIM ** -0.5,
                                    jnp.tile(a_gk[j], n_grp), ones(SELF_WIDTH), qm_gain])
            kinds = [1] * 6 + [0] * 3 + [2]
            p3 = _proj(x2, mix_g[i], bf(a_w_in[j]), kinds, gain, BF16, 512).reshape(B, T, -1)
            lambda_init = 0.8 - 0.6 * math.exp(-0.3 * i)
            y_self = _diff_attn(p3, tab, a_lam_q1[j], a_lam_k1[j], a_lam_q2[j], a_lam_k2[j],
                                a_g_sub[j], lambda_init, tq)
            qm_block = 3 * SELF_WIDTH // MEM_WIDTH
        else:
            w = b_w_in[j]
            kvw = DSA_KV_HEADS * HEAD_DIM
            iqw = IDX_HEADS * IDX_DIM
            c_q, c_k, c_v, c_iq = 0, SELF_WIDTH, SELF_WIDTH + kvw, SELF_WIDTH + 2 * kvw
            c_ik = c_iq + iqw
            c_iw = c_ik + IDX_DIM
            c_qm = c_iw + IDX_HEADS
            w_main = jnp.concatenate([w[:, c_iq:c_ik], w[:, c_q:c_k], w[:, c_k:c_v], w[:, c_v:c_iq],
                                      w[:, c_qm:]], axis=1)
            gain = jnp.concatenate([ones(iqw), jnp.tile(b_gq[j], N_SELF_HEADS) * head_scale,
                                    jnp.tile(b_gk[j], DSA_KV_HEADS), ones(kvw), qm_gain])
            kinds = [0, 0, 2, 2, 2, 2, 0, 2]
            p3 = _proj(x2, mix_g[i], bf(w_main), kinds, gain, BF16, 512).reshape(B, T, -1)
            pad = LANES - IDX_DIM - IDX_HEADS
            w_idx = jnp.concatenate([w[:, c_ik:c_qm], jnp.zeros((D, pad), F32)], axis=1)
            idx_gain = jnp.concatenate([ones(IDX_DIM),
                                        jnp.full((IDX_HEADS,), (IDX_DIM * IDX_HEADS) ** -0.5, F32),
                                        jnp.zeros((pad,), F32)])
            ikw = _proj(x2, mix_g[i], bf(w_idx), [0], idx_gain, F32, LANES).reshape(B, T, LANES)
            ik = ikw[:, :, :IDX_DIM].astype(BF16)
            ik2 = jnp.concatenate([ik, ik], axis=-1)
            iw = ikw[:, :, IDX_DIM:IDX_DIM + IDX_HEADS]
            y_self = _dsa_attn(p3, ik2, iw, tab, tq)
            qm_block = (p3.shape[-1] - MEM_WIDTH) // MEM_WIDTH

        y_mem = _mem_attn(p3, kv3, qm_block)
        wo = bf(w_out[i])
        x2 = _out_proj(y_self.reshape(M, SELF_WIDTH), y_mem.reshape(M, MEM_WIDTH),
                       wo[:SELF_WIDTH], wo[SELF_WIDTH:], x2)

        x2 = _ffn(x2, ffn2_g[i], bf(ffn2_w_gate[i]), bf(ffn2_w_up[i]), bf(0.5 * ffn2_w_down[i]))

    return x2.reshape(B, T, D)
```

```python
import functools
import math
from typing import Any, Callable, NamedTuple

import jax
import jax.numpy as jnp
from jax import lax
from jax.experimental import pallas as pl
from jax.experimental.pallas import tpu as pltpu

HEAD_DIM = 128
N_SELF_HEADS = 12
N_MEM_HEADS = 4
SELF_WIDTH = N_SELF_HEADS * HEAD_DIM
MEM_WIDTH = N_MEM_HEADS * HEAD_DIM
DIFF_QK_DIM = HEAD_DIM // 2
DSA_KV_HEADS = 4
DSA_GROUP = N_SELF_HEADS // DSA_KV_HEADS
IDX_HEADS = 16
IDX_DIM = 64
TOPK_MAX = 256
NUM_BUCKETS = 32
MAX_DISTANCE = 128
EPS = 1e-6
NEG_INF = -1e30
INT32_MIN = -(2 ** 31)

LANES = 128
VMEM_LIMIT = 56 * 1024 * 1024

F32 = jnp.float32
BF16 = jnp.bfloat16


def _dot(a, b):
    return jnp.dot(a, b, preferred_element_type=F32)


def _dot_nt(a, b):
    return lax.dot_general(a, b, (((1,), (1,)), ((), ())), preferred_element_type=F32)


def _params(semantics):
    return pltpu.CompilerParams(dimension_semantics=semantics, vmem_limit_bytes=VMEM_LIMIT)


def _rms_rows(x, g):
    ms = jnp.mean(x * x, axis=-1, keepdims=True)
    return x * lax.rsqrt(ms + EPS) * g


def _ffn_body(x_ref, g_ref, wg_ref, wu_ref, wd_ref, o_ref, h_ref):
    @pl.when(pl.program_id(1) == 0)
    def _():
        x = x_ref[...]
        h_ref[...] = _rms_rows(x, g_ref[...]).astype(BF16)
        o_ref[...] = x

    h = h_ref[...]
    gate = _dot(h, wg_ref[...])
    up = _dot(h, wu_ref[...])
    act = (gate * jax.nn.sigmoid(gate) * up).astype(BF16)
    o_ref[...] += _dot(act, wd_ref[...])


def _ffn(x2, g, wg, wu, wd_half):
    M, D = x2.shape
    F = wg.shape[1]
    tm = min(512, M)
    tf = min(512, F)
    return pl.pallas_call(
        _ffn_body,
        grid=(M // tm, F // tf),
        in_specs=[
            pl.BlockSpec((tm, D), lambda i, f: (i, 0)),
            pl.BlockSpec((1, D), lambda i, f: (0, 0)),
            pl.BlockSpec((D, tf), lambda i, f: (0, f)),
            pl.BlockSpec((D, tf), lambda i, f: (0, f)),
            pl.BlockSpec((tf, D), lambda i, f: (f, 0)),
        ],
        out_specs=pl.BlockSpec((tm, D), lambda i, f: (i, 0)),
        out_shape=jax.ShapeDtypeStruct((M, D), F32),
        scratch_shapes=[pltpu.VMEM((tm, D), BF16)],
        compiler_params=_params(("parallel", "arbitrary")),
        name="ffn",
    )(x2, g.reshape(1, D), wg, wu, wd_half)


def _proj_body(types_ref, x_ref, g_ref, w_ref, gmat_ref, gain_ref, o_ref, h_ref, *, tn):
    j = pl.program_id(1)

    @pl.when(j == 0)
    def _():
        h_ref[...] = _rms_rows(x_ref[...], g_ref[...]).astype(BF16)

    y = _dot(h_ref[...], w_ref[...])
    kind = types_ref[j]

    @pl.when(kind == 0)
    def _():
        o_ref[...] = (y * gain_ref[...]).astype(o_ref.dtype)

    @pl.when(kind != 0)
    def _():
        gmat = gmat_ref[...]
        for c in range(tn // LANES):
            sl = slice(c * LANES, (c + 1) * LANES)
            yc = y[:, sl]
            ms = _dot((yc * yc).astype(BF16), gmat)
            o_ref[:, sl] = (yc * lax.rsqrt(ms + EPS) * gain_ref[:, sl]).astype(o_ref.dtype)


def _group_mats():
    r = jnp.arange(LANES)
    same64 = (r[:, None] // DIFF_QK_DIM) == (r[None, :] // DIFF_QK_DIM)
    return jnp.stack([
        jnp.zeros((LANES, LANES), F32),
        jnp.where(same64, 1.0 / DIFF_QK_DIM, 0.0),
        jnp.full((LANES, LANES), 1.0 / HEAD_DIM, F32),
    ]).astype(BF16)


def _proj(x2, g, w, kinds, gain, out_dtype, tn):
    M, D = x2.shape
    N = w.shape[1]
    tm = min(1024, M)
    kinds = jnp.asarray(kinds, jnp.int32)
    grid_spec = pltpu.PrefetchScalarGridSpec(
        num_scalar_prefetch=1,
        grid=(M // tm, N // tn),
        in_specs=[
            pl.BlockSpec((tm, D), lambda i, j, t: (i, 0)),
            pl.BlockSpec((1, D), lambda i, j, t: (0, 0)),
            pl.BlockSpec((D, tn), lambda i, j, t: (0, j)),
            pl.BlockSpec((None, LANES, LANES), lambda i, j, t: (t[j], 0, 0)),
            pl.BlockSpec((1, tn), lambda i, j, t: (0, j)),
        ],
        out_specs=pl.BlockSpec((tm, tn), lambda i, j, t: (i, j)),
        scratch_shapes=[pltpu.VMEM((tm, D), BF16)],
    )
    return pl.pallas_call(
        functools.partial(_proj_body, tn=tn),
        grid_spec=grid_spec,
        out_shape=jax.ShapeDtypeStruct((M, N), out_dtype),
        compiler_params=_params(("parallel", "arbitrary")),
        name="proj",
    )(kinds, x2, g.reshape(1, D), w, _group_mats(), gain.reshape(1, N))


def _bias_body(rb_ref, o_ref, *, tq, tk):
    h = pl.program_id(0)
    r = lax.broadcasted_iota(jnp.int32, (tq, tk), 0)
    c = lax.broadcasted_iota(jnp.int32, (tq, tk), 1)
    max_exact = NUM_BUCKETS // 2
    far = rb_ref[NUM_BUCKETS - 1, h]
    for off in range(2):
        dist = r - c + off * tk
        n = jnp.maximum(dist, 0)
        nf = jnp.maximum(n, 1).astype(F32)
        large = max_exact + (jnp.log(nf / max_exact) / math.log(MAX_DISTANCE / max_exact)
                             * (NUM_BUCKETS - max_exact)).astype(jnp.int32)
        large = jnp.minimum(large, NUM_BUCKETS - 1)
        bucket = jnp.where(n < max_exact, n, large)
        val = jnp.zeros((tq, tk), F32)
        for b in range(NUM_BUCKETS):
            val = jnp.where(bucket == b, rb_ref[b, h], val)
        o_ref[off] = jnp.where(dist >= 0, val - far, NEG_INF)


def _bias_tiles(rel_bias, tq, tk):
    assert tq == tk and tk >= MAX_DISTANCE
    return pl.pallas_call(
        functools.partial(_bias_body, tq=tq, tk=tk),
        grid=(N_SELF_HEADS,),
        in_specs=[pl.BlockSpec(memory_space=pltpu.SMEM)],
        out_specs=pl.BlockSpec((None, 2, tq, tk), lambda h: (h, 0, 0, 0)),
        out_shape=jax.ShapeDtypeStruct((N_SELF_HEADS, 2, tq, tk), F32),
        compiler_params=_params(("arbitrary",)),
        name="bias_tiles",
    )(rel_bias)


class _Stream(NamedTuple):
    qs_ref: Any
    s_ref: Any
    m_ref: Any
    l_ref: Any
    acc_ref: Any
    k_tile: Callable
    v_tile: Callable
    bias: Callable


def _lane_fold(x, op):
    out = x[:, :LANES]
    for c in range(1, x.shape[1] // LANES):
        out = op(out, x[:, c * LANES:(c + 1) * LANES])
    return out


def _two_pass_attention(streams, i, tk):
    for st in streams:
        st.m_ref[...] = jnp.full(st.m_ref.shape, NEG_INF, F32)
        st.l_ref[...] = jnp.zeros(st.l_ref.shape, F32)
        st.acc_ref[...] = jnp.zeros(st.acc_ref.shape, F32)

    def score(j, near):
        for st in streams:
            s = st.bias(_dot_nt(st.qs_ref[...], st.k_tile(j)), j, near)
            st.s_ref[j] = s
            st.m_ref[...] = jnp.maximum(st.m_ref[...], _lane_fold(s, jnp.maximum))

    def score_loop(lo, hi, near):
        def body(j, c):
            score(j, near)
            return c
        lax.fori_loop(lo, hi, body, 0)

    n_far = jnp.maximum(i - 1, 0)
    score_loop(0, n_far, None)
    score_loop(n_far, i, 1)
    score(i, 0)

    for st in streams:
        m = jnp.max(st.m_ref[...], axis=-1, keepdims=True)
        st.m_ref[...] = jnp.broadcast_to(m, st.m_ref.shape)

    def pv(j, c):
        for st in streams:
            m = st.m_ref[...]
            p = jnp.exp(st.s_ref[j] - jnp.concatenate([m] * (tk // LANES), axis=1))
            st.l_ref[...] += _lane_fold(p, jnp.add)
            st.acc_ref[...] += _dot(p.astype(BF16), st.v_tile(j))
        return c

    lax.fori_loop(0, i + 1, pv, 0)


def _key_rows(j, tk):
    return pl.ds(pl.multiple_of(j * tk, tk), tk)


def _diff_body(lq1_ref, lk1_ref, lq2_ref, lk2_ref, q_ref, k_ref, v_ref, tab_ref, gsub_ref, o_ref,
               qs_ref, s_ref, m_ref, l_ref, acc_ref, *, tq, tk, hp, lambda_init):
    i = pl.program_id(2)
    lane = lax.broadcasted_iota(jnp.int32, (tq, HEAD_DIM), 1)
    streams = []
    for hh in range(hp):
        cols = slice(hh * HEAD_DIM, (hh + 1) * HEAD_DIM)
        q = q_ref[:, cols]
        zero = jnp.zeros_like(q)
        qs_ref[hh, :tq] = jnp.where(lane < DIFF_QK_DIM, q, zero)
        qs_ref[hh, tq:] = jnp.where(lane >= DIFF_QK_DIM, q, zero)

        def bias(s, j, near, hh=hh):
            if near is None:
                return s
            tab = tab_ref[hh, near]
            return s + jnp.concatenate([tab, tab], axis=0)

        streams.append(_Stream(
            qs_ref.at[hh], s_ref.at[hh], m_ref.at[hh], l_ref.at[hh], acc_ref.at[hh],
            lambda j, cols=cols: k_ref[_key_rows(j, tk), cols],
            lambda j, cols=cols: v_ref[_key_rows(j, tk), cols],
            bias))

    _two_pass_attention(streams, i, tk)

    lam = (jnp.exp(jnp.sum(lq1_ref[...] * lk1_ref[...], axis=-1, keepdims=True))
           - jnp.exp(jnp.sum(lq2_ref[...] * lk2_ref[...], axis=-1, keepdims=True)) + lambda_init)
    for hh in range(hp):
        out = acc_ref[hh] / jnp.sum(l_ref[hh], axis=-1, keepdims=True)
        o = out[:tq] - lam * out[tq:]
        o_ref[:, hh * HEAD_DIM:(hh + 1) * HEAD_DIM] = (
            _rms_rows(o, gsub_ref[...]) * (1.0 - lambda_init)).astype(o_ref.dtype)


def _diff_attn(p3, tab, lq1, lk1, lq2, lk2, g_sub, lambda_init, tq):
    B, T, _ = p3.shape
    H = N_SELF_HEADS
    hp = 2
    nq = T // tq
    w = hp * HEAD_DIM
    vec = lambda a: a.reshape(1, -1).astype(F32)
    small = lambda n: pl.BlockSpec((1, n), lambda b, h, i: (0, 0))
    return pl.pallas_call(
        functools.partial(_diff_body, tq=tq, tk=tq, hp=hp, lambda_init=lambda_init),
        grid=(B, H // hp, nq),
        in_specs=[
            small(DIFF_QK_DIM), small(DIFF_QK_DIM), small(DIFF_QK_DIM), small(DIFF_QK_DIM),
            pl.BlockSpec((None, tq, w), lambda b, h, i: (b, i, h)),
            pl.BlockSpec((None, T, w), lambda b, h, i: (b, 0, H // hp + h)),
            pl.BlockSpec((None, T, w), lambda b, h, i: (b, 0, 2 * (H // hp) + h)),
            pl.BlockSpec((hp, 2, tq, tq), lambda b, h, i: (h, 0, 0, 0)),
            small(HEAD_DIM),
        ],
        out_specs=pl.BlockSpec((None, tq, w), lambda b, h, i: (b, i, h)),
        out_shape=jax.ShapeDtypeStruct((B, T, SELF_WIDTH), BF16),
        scratch_shapes=[
            pltpu.VMEM((hp, 2 * tq, HEAD_DIM), BF16),
            pltpu.VMEM((hp, nq, 2 * tq, tq), F32),
            pltpu.VMEM((hp, 2 * tq, LANES), F32),
            pltpu.VMEM((hp, 2 * tq, LANES), F32),
            pltpu.VMEM((hp, 2 * tq, HEAD_DIM), F32),
        ],
        compiler_params=_params(("parallel", "parallel", "arbitrary")),
        name="diff_attn",
    )(vec(lq1), vec(lk1), vec(lq2), vec(lk2), p3, p3, p3, tab, vec(g_sub))


SEARCH_ROWS = 128


def _sortable(x):
    bits = pltpu.bitcast(x, jnp.int32)
    return jnp.where(bits < 0, bits ^ jnp.int32(0x7FFFFFFF), bits)


def _dsa_select(iq_ref, ik_ref, iw_ref, key_ref, madd_ref, iqm_ref, thr_ref, need_ref, i,
                *, tq, tk, topk):
    n_tiles = i + 1
    lane = lax.broadcasted_iota(jnp.int32, (tq, LANES), 1)
    for hh in range(IDX_HEADS):
        pair = iq_ref[:, (hh // 2) * LANES:(hh // 2 + 1) * LANES]
        keep = (lane < IDX_DIM) if hh % 2 == 0 else (lane >= IDX_DIM)
        iqm_ref[hh] = jnp.where(keep, pair, jnp.zeros_like(pair))

    row = lax.broadcasted_iota(jnp.int32, (tq, tk), 0)
    col = lax.broadcasted_iota(jnp.int32, (tq, tk), 1)

    def causal(j):
        return (i * tq + row) >= (j * tk + col)

    def score_tile(j, c):
        ik = ik_ref[_key_rows(j, tk), :]
        sc = jnp.zeros((tq, tk), F32)
        for hh in range(IDX_HEADS):
            sc = sc + jnp.maximum(_dot_nt(iqm_ref[hh], ik), 0.0) * iw_ref[:, hh:hh + 1]
        key_ref[j] = _sortable(jnp.where(causal(j), sc, NEG_INF))
        return c

    lax.fori_loop(0, n_tiles, score_tile, 0)

    kf = float(topk)
    blocks = [slice(r0, r0 + SEARCH_ROWS) for r0 in range(0, tq, SEARCH_ROWS)]

    def count(cmp, cands):
        cands_b = [jnp.broadcast_to(c, (SEARCH_ROWS, LANES)) for c in cands]

        def body(j, parts):
            out = []
            for rows, cand_b, part in zip(blocks, cands_b, parts):
                k = key_ref[j, rows, :]
                for c in range(tk // LANES):
                    hit = cmp(k[:, c * LANES:(c + 1) * LANES], cand_b)
                    part = part + jnp.where(hit, 1.0, 0.0)
                out.append(part)
            return tuple(out)

        zeros = tuple(jnp.zeros((SEARCH_ROWS, LANES), F32) for _ in blocks)
        parts = lax.fori_loop(0, n_tiles, body, zeros)
        return [jnp.sum(p, axis=-1, keepdims=True) for p in parts]

    ge = lambda k, c: k >= c
    zero = jnp.zeros((SEARCH_ROWS, 1), jnp.int32)
    bases = tuple(jnp.where(n >= kf, zero, jnp.int32(INT32_MIN))
                  for n in count(ge, [zero] * len(blocks)))

    def bit_step(b, bases):
        bit = lax.shift_left(jnp.int32(1), jnp.int32(30) - b)
        cands = [base | bit for base in bases]
        return tuple(jnp.where(n >= kf, cand, base)
                     for n, cand, base in zip(count(ge, cands), cands, bases))

    thrs = lax.fori_loop(0, 31, bit_step, bases)
    needs = [kf - n for n in count(lambda k, c: k > c, thrs)]
    for rows, thr, need in zip(blocks, thrs, needs):
        thr_ref[rows, :] = jnp.broadcast_to(thr, (SEARCH_ROWS, LANES))
        need_ref[rows, :] = jnp.broadcast_to(need, (SEARCH_ROWS, LANES))

    upper = jnp.where(row < col, 1.0, 0.0).astype(BF16)
    reps = tk // LANES

    def mask_tile(j, seen):
        key = key_ref[j]
        thr = jnp.concatenate([thr_ref[...]] * reps, axis=1)
        need = jnp.concatenate([need_ref[...]] * reps, axis=1)
        eq = jnp.where(key == thr, 1.0, 0.0)
        rank = seen + _dot(eq.astype(BF16), upper)
        take = (key > thr) | ((key == thr) & (rank < need))
        madd_ref[j] = jnp.where(take & causal(j), 0.0, NEG_INF)
        return seen + jnp.sum(eq, axis=-1, keepdims=True)

    lax.fori_loop(0, n_tiles, mask_tile, jnp.zeros((tq, 1), F32))


def _dsa_body(q_ref, k_ref, v_ref, iq_ref, ik_ref, iw_ref, tab_ref, o_ref,
              key_ref, madd_ref, iqm_ref, thr_ref, need_ref, qs_ref, s_ref, m_ref, l_ref, acc_ref,
              *, tq, tk, topk):
    i = pl.program_id(1)
    R = DSA_GROUP

    @pl.when(pl.program_id(2) == 0)
    def _():
        _dsa_select(iq_ref, ik_ref, iw_ref, key_ref, madd_ref, iqm_ref, thr_ref, need_ref, i,
                    tq=tq, tk=tk, topk=topk)

    for r in range(R):
        qs_ref[r * tq:(r + 1) * tq] = q_ref[:, r * HEAD_DIM:(r + 1) * HEAD_DIM]

    def bias(s, j, near):
        madd = madd_ref[j]
        s = s + jnp.concatenate([madd] * R, axis=0)
        if near is not None:
            s = s + jnp.concatenate([tab_ref[r, near] for r in range(R)], axis=0)
        return s

    stream = _Stream(qs_ref, s_ref, m_ref, l_ref, acc_ref,
                     lambda j: k_ref[_key_rows(j, tk), :], lambda j: v_ref[_key_rows(j, tk), :], bias)
    _two_pass_attention([stream], i, tk)

    out = acc_ref[...] / jnp.sum(l_ref[...], axis=-1, keepdims=True)
    for r in range(R):
        o_ref[:, r * HEAD_DIM:(r + 1) * HEAD_DIM] = out[r * tq:(r + 1) * tq].astype(o_ref.dtype)


def _dsa_attn(p3, ik2, iw, tab, tq):
    B, T, _ = p3.shape
    H, R, G = N_SELF_HEADS, DSA_GROUP, DSA_KV_HEADS
    topk = min(TOPK_MAX, T // 4)
    nq = T // tq
    iq_w = IDX_HEADS * IDX_DIM
    k0 = H
    iq0 = (H + G) * HEAD_DIM // iq_w
    v0 = H + G + iq_w // HEAD_DIM
    assert (H + G) * HEAD_DIM % iq_w == 0
    return pl.pallas_call(
        functools.partial(_dsa_body, tq=tq, tk=tq, topk=topk),
        grid=(B, nq, G),
        in_specs=[
            pl.BlockSpec((None, tq, R * HEAD_DIM), lambda b, i, g: (b, i, g)),
            pl.BlockSpec((None, T, HEAD_DIM), lambda b, i, g: (b, 0, k0 + g)),
            pl.BlockSpec((None, T, HEAD_DIM), lambda b, i, g: (b, 0, v0 + g)),
            pl.BlockSpec((None, tq, iq_w), lambda b, i, g: (b, i, iq0)),
            pl.BlockSpec((None, T, LANES), lambda b, i, g: (b, 0, 0)),
            pl.BlockSpec((None, tq, IDX_HEADS), lambda b, i, g: (b, i, 0)),
            pl.BlockSpec((R, 2, tq, tq), lambda b, i, g: (g, 0, 0, 0)),
        ],
        out_specs=pl.BlockSpec((None, tq, R * HEAD_DIM), lambda b, i, g: (b, i, g)),
        out_shape=jax.ShapeDtypeStruct((B, T, SELF_WIDTH), BF16),
        scratch_shapes=[
            pltpu.VMEM((nq, tq, tq), jnp.int32),
            pltpu.VMEM((nq, tq, tq), F32),
            pltpu.VMEM((IDX_HEADS, tq, LANES), BF16),
            pltpu.VMEM((tq, LANES), jnp.int32),
            pltpu.VMEM((tq, LANES), F32),
            pltpu.VMEM((R * tq, HEAD_DIM), BF16),
            pltpu.VMEM((nq, R * tq, tq), F32),
            pltpu.VMEM((R * tq, LANES), F32),
            pltpu.VMEM((R * tq, LANES), F32),
            pltpu.VMEM((R * tq, HEAD_DIM), F32),
        ],
        compiler_params=_params(("parallel", "arbitrary", "arbitrary")),
        name="dsa_attn",
    )(p3, p3, p3, p3, ik2, iw, tab)


def _mem_body(q_ref, k_ref, v_ref, o_ref):
    for h in range(N_MEM_HEADS):
        sl = slice(h * HEAD_DIM, (h + 1) * HEAD_DIM)
        s = _dot_nt(q_ref[:, sl], k_ref[:, sl])
        p = jnp.exp(s - jnp.max(s, axis=-1, keepdims=True))
        o = _dot(p.astype(BF16), v_ref[:, sl]) / jnp.sum(p, axis=-1, keepdims=True)
        o_ref[:, sl] = o.astype(o_ref.dtype)


def _mem_attn(p3, kv3, qm_block):
    B, T, _ = p3.shape
    N = kv3.shape[1]
    tq = min(512, T)
    return pl.pallas_call(
        _mem_body,
        grid=(B, T // tq),
        in_specs=[
            pl.BlockSpec((None, tq, MEM_WIDTH), lambda b, i: (b, i, qm_block)),
            pl.BlockSpec((None, N, MEM_WIDTH), lambda b, i: (b, 0, 0)),
            pl.BlockSpec((None, N, MEM_WIDTH), lambda b, i: (b, 0, 1)),
        ],
        out_specs=pl.BlockSpec((None, tq, MEM_WIDTH), lambda b, i: (b, i, 0)),
        out_shape=jax.ShapeDtypeStruct((B, T, MEM_WIDTH), BF16),
        compiler_params=_params(("parallel", "arbitrary")),
        name="mem_attn",
    )(p3, kv3, kv3)


def _out_body(ys_ref, ym_ref, ws_ref, wm_ref, x_ref, o_ref):
    o_ref[...] = x_ref[...] + _dot(ys_ref[...], ws_ref[...]) + _dot(ym_ref[...], wm_ref[...])


def _out_proj(ys, ym, w_self, w_mem, x2):
    M, D = x2.shape
    tm = min(512, M)
    return pl.pallas_call(
        _out_body,
        grid=(M // tm,),
        in_specs=[
            pl.BlockSpec((tm, SELF_WIDTH), lambda i: (i, 0)),
            pl.BlockSpec((tm, MEM_WIDTH), lambda i: (i, 0)),
            pl.BlockSpec((SELF_WIDTH, D), lambda i: (0, 0)),
            pl.BlockSpec((MEM_WIDTH, D), lambda i: (0, 0)),
            pl.BlockSpec((tm, D), lambda i: (i, 0)),
        ],
        out_specs=pl.BlockSpec((tm, D), lambda i: (i, 0)),
        out_shape=jax.ShapeDtypeStruct((M, D), F32),
        compiler_params=_params(("parallel",)),
        name="out_proj",
    )(ys, ym, w_self, w_mem, x2)


def kernel(x, mem, rel_bias, ffn1_g, ffn1_w_gate, ffn1_w_up, ffn1_w_down, ffn2_g, ffn2_w_gate,
           ffn2_w_up, ffn2_w_down, mix_g, mem_g, mem_w_kv, mem_gq, mem_gk, w_out, a_w_in, a_gq, a_gk,
           a_lam_q1, a_lam_k1, a_lam_q2, a_lam_k2, a_g_sub, b_w_in, b_gq, b_gk):
    B, T, D = x.shape
    N = mem.shape[1]
    depth = ffn1_g.shape[0]
    M = B * T
    tq = min(256, T)
    head_scale = HEAD_DIM ** -0.5
    ones = lambda n: jnp.ones((n,), F32)
    bf = lambda w: w.astype(BF16)

    x2 = x.reshape(M, D)
    mem2 = mem.reshape(B * N, D)
    tab = _bias_tiles(rel_bias, tq, tq)

    for i in range(depth):
        x2 = _ffn(x2, ffn1_g[i], bf(ffn1_w_gate[i]), bf(ffn1_w_up[i]), bf(0.5 * ffn1_w_down[i]))

        kv_gain = jnp.concatenate([jnp.tile(mem_gk[i], N_MEM_HEADS), ones(MEM_WIDTH)])
        kv3 = _proj(mem2, mem_g[i], bf(mem_w_kv[i]), [2, 0], kv_gain, BF16, MEM_WIDTH)
        kv3 = kv3.reshape(B, N, 2 * MEM_WIDTH)
        qm_gain = jnp.tile(mem_gq[i], N_MEM_HEADS) * head_scale
        j = i // 2

        if i % 2 == 0:
            n_grp = SELF_WIDTH // DIFF_QK_DIM
            gain = jnp.concatenate([jnp.tile(a_gq[j], n_grp) * DIFF_QK_DIM ** -0.5,
                                    jnp.tile(a_gk[j], n_grp), ones(SELF_WIDTH), qm_gain])
            kinds = [1] * 6 + [0] * 3 + [2]
            p3 = _proj(x2, mix_g[i], bf(a_w_in[j]), kinds, gain, BF16, 512).reshape(B, T, -1)
            lambda_init = 0.8 - 0.6 * math.exp(-0.3 * i)
            y_self = _diff_attn(p3, tab, a_lam_q1[j], a_lam_k1[j], a_lam_q2[j], a_lam_k2[j],
                                a_g_sub[j], lambda_init, tq)
            qm_block = 3 * SELF_WIDTH // MEM_WIDTH
        else:
            w = b_w_in[j]
            kvw = DSA_KV_HEADS * HEAD_DIM
            iqw = IDX_HEADS * IDX_DIM
            c_q, c_k, c_v, c_iq = 0, SELF_WIDTH, SELF_WIDTH + kvw, SELF_WIDTH + 2 * kvw
            c_ik = c_iq + iqw
            c_iw = c_ik + IDX_DIM
            c_qm = c_iw + IDX_HEADS
            w_main = jnp.concatenate([w[:, c_q:c_k], w[:, c_k:c_v], w[:, c_iq:c_ik], w[:, c_v:c_iq],
                                      w[:, c_qm:]], axis=1)
            gain = jnp.concatenate([jnp.tile(b_gq[j], N_SELF_HEADS) * head_scale,
                                    jnp.tile(b_gk[j], DSA_KV_HEADS), ones(iqw), ones(kvw), qm_gain])
            kinds = [2, 2, 2, 2, 0, 0, 0, 2]
            p3 = _proj(x2, mix_g[i], bf(w_main), kinds, gain, BF16, 512).reshape(B, T, -1)
            pad = LANES - IDX_DIM - IDX_HEADS
            w_idx = jnp.concatenate([w[:, c_ik:c_qm], jnp.zeros((D, pad), F32)], axis=1)
            idx_gain = jnp.concatenate([ones(IDX_DIM),
                                        jnp.full((IDX_HEADS,), (IDX_DIM * IDX_HEADS) ** -0.5, F32),
                                        jnp.zeros((pad,), F32)])
            ikw = _proj(x2, mix_g[i], bf(w_idx), [0], idx_gain, F32, LANES).reshape(B, T, LANES)
            ik = ikw[:, :, :IDX_DIM].astype(BF16)
            ik2 = jnp.concatenate([ik, ik], axis=-1)
            iw = ikw[:, :, IDX_DIM:IDX_DIM + IDX_HEADS]
            y_self = _dsa_attn(p3, ik2, iw, tab, tq)
            qm_block = (p3.shape[-1] - MEM_WIDTH) // MEM_WIDTH

        y_mem = _mem_attn(p3, kv3, qm_block)
        wo = bf(w_out[i])
        x2 = _out_proj(y_self.reshape(M, SELF_WIDTH), y_mem.reshape(M, MEM_WIDTH),
                       wo[:SELF_WIDTH], wo[SELF_WIDTH:], x2)

        x2 = _ffn(x2, ffn2_g[i], bf(ffn2_w_gate[i]), bf(ffn2_w_up[i]), bf(0.5 * ffn2_w_down[i]))

    return x2.reshape(B, T, D)
```

```python
import functools
import math
from typing import Any, Callable, NamedTuple

import jax
import jax.numpy as jnp
from jax import lax
from jax.experimental import pallas as pl
from jax.experimental.pallas import tpu as pltpu

HEAD_DIM = 128
N_SELF_HEADS = 12
N_MEM_HEADS = 4
SELF_WIDTH = N_SELF_HEADS * HEAD_DIM
MEM_WIDTH = N_MEM_HEADS * HEAD_DIM
DIFF_QK_DIM = HEAD_DIM // 2
DSA_KV_HEADS = 4
DSA_GROUP = N_SELF_HEADS // DSA_KV_HEADS
IDX_HEADS = 16
IDX_DIM = 64
TOPK_MAX = 256
NUM_BUCKETS = 32
MAX_DISTANCE = 128
EPS = 1e-6
NEG_INF = -1e30
INT32_MIN = -(2 ** 31)
LOG2E = math.log2(math.e)

LANES = 128
VMEM_LIMIT = 56 * 1024 * 1024

F32 = jnp.float32
BF16 = jnp.bfloat16


def _dot(a, b):
    return jnp.dot(a, b, preferred_element_type=F32)


def _dot_nt(a, b):
    return lax.dot_general(a, b, (((1,), (1,)), ((), ())), preferred_element_type=F32)


def _params(semantics):
    return pltpu.CompilerParams(dimension_semantics=semantics, vmem_limit_bytes=VMEM_LIMIT)


def _rms_rows(x, g):
    ms = jnp.mean(x * x, axis=-1, keepdims=True)
    return x * lax.rsqrt(ms + EPS) * g


def _ffn_body(x_ref, g_ref, wg_ref, wu_ref, wd_ref, o_ref, h_ref):
    @pl.when(pl.program_id(1) == 0)
    def _():
        x = x_ref[...]
        h_ref[...] = _rms_rows(x, g_ref[...]).astype(BF16)
        o_ref[...] = x

    h = h_ref[...]
    gate = _dot(h, wg_ref[...])
    up = _dot(h, wu_ref[...])
    act = (gate * jax.nn.sigmoid(gate) * up).astype(BF16)
    o_ref[...] += _dot(act, wd_ref[...])


def _ffn(x2, g, wg, wu, wd_half):
    M, D = x2.shape
    F = wg.shape[1]
    tm = min(512, M)
    tf = min(512, F)
    return pl.pallas_call(
        _ffn_body,
        grid=(M // tm, F // tf),
        in_specs=[
            pl.BlockSpec((tm, D), lambda i, f: (i, 0)),
            pl.BlockSpec((1, D), lambda i, f: (0, 0)),
            pl.BlockSpec((D, tf), lambda i, f: (0, f)),
            pl.BlockSpec((D, tf), lambda i, f: (0, f)),
            pl.BlockSpec((tf, D), lambda i, f: (f, 0)),
        ],
        out_specs=pl.BlockSpec((tm, D), lambda i, f: (i, 0)),
        out_shape=jax.ShapeDtypeStruct((M, D), F32),
        scratch_shapes=[pltpu.VMEM((tm, D), BF16)],
        compiler_params=_params(("parallel", "arbitrary")),
        name="ffn",
    )(x2, g.reshape(1, D), wg, wu, wd_half)


def _proj_body(types_ref, x_ref, g_ref, w_ref, gmat_ref, gain_ref, o_ref, h_ref, *, tn):
    j = pl.program_id(1)

    @pl.when(j == 0)
    def _():
        h_ref[...] = _rms_rows(x_ref[...], g_ref[...]).astype(BF16)

    y = _dot(h_ref[...], w_ref[...])
    kind = types_ref[j]

    @pl.when(kind == 0)
    def _():
        o_ref[...] = (y * gain_ref[...]).astype(o_ref.dtype)

    @pl.when(kind != 0)
    def _():
        gmat = gmat_ref[...]
        for c in range(tn // LANES):
            sl = slice(c * LANES, (c + 1) * LANES)
            yc = y[:, sl]
            ms = _dot((yc * yc).astype(BF16), gmat)
            o_ref[:, sl] = (yc * lax.rsqrt(ms + EPS) * gain_ref[:, sl]).astype(o_ref.dtype)


def _group_mats():
    r = jnp.arange(LANES)
    same64 = (r[:, None] // DIFF_QK_DIM) == (r[None, :] // DIFF_QK_DIM)
    return jnp.stack([
        jnp.zeros((LANES, LANES), F32),
        jnp.where(same64, 1.0 / DIFF_QK_DIM, 0.0),
        jnp.full((LANES, LANES), 1.0 / HEAD_DIM, F32),
    ]).astype(BF16)


def _proj(x2, g, w, kinds, gain, out_dtype, tn):
    M, D = x2.shape
    N = w.shape[1]
    tm = min(1024, M)
    kinds = jnp.asarray(kinds, jnp.int32)
    grid_spec = pltpu.PrefetchScalarGridSpec(
        num_scalar_prefetch=1,
        grid=(M // tm, N // tn),
        in_specs=[
            pl.BlockSpec((tm, D), lambda i, j, t: (i, 0)),
            pl.BlockSpec((1, D), lambda i, j, t: (0, 0)),
            pl.BlockSpec((D, tn), lambda i, j, t: (0, j)),
            pl.BlockSpec((None, LANES, LANES), lambda i, j, t: (t[j], 0, 0)),
            pl.BlockSpec((1, tn), lambda i, j, t: (0, j)),
        ],
        out_specs=pl.BlockSpec((tm, tn), lambda i, j, t: (i, j)),
        scratch_shapes=[pltpu.VMEM((tm, D), BF16)],
    )
    return pl.pallas_call(
        functools.partial(_proj_body, tn=tn),
        grid_spec=grid_spec,
        out_shape=jax.ShapeDtypeStruct((M, N), out_dtype),
        compiler_params=_params(("parallel", "arbitrary")),
        name="proj",
    )(kinds, x2, g.reshape(1, D), w, _group_mats(), gain.reshape(1, N))


def _bias_body(rb_ref, o_ref, *, tq, tk):
    h = pl.program_id(0)
    r = lax.broadcasted_iota(jnp.int32, (tq, tk), 0)
    c = lax.broadcasted_iota(jnp.int32, (tq, tk), 1)
    max_exact = NUM_BUCKETS // 2
    far = rb_ref[NUM_BUCKETS - 1, h]
    for off in range(2):
        dist = r - c + off * tk
        n = jnp.maximum(dist, 0)
        nf = jnp.maximum(n, 1).astype(F32)
        large = max_exact + (jnp.log(nf / max_exact) / math.log(MAX_DISTANCE / max_exact)
                             * (NUM_BUCKETS - max_exact)).astype(jnp.int32)
        large = jnp.minimum(large, NUM_BUCKETS - 1)
        bucket = jnp.where(n < max_exact, n, large)
        val = jnp.zeros((tq, tk), F32)
        for b in range(NUM_BUCKETS):
            val = jnp.where(bucket == b, rb_ref[b, h], val)
        o_ref[off] = jnp.where(dist >= 0, (val - far) * LOG2E, NEG_INF)


def _bias_tiles(rel_bias, tq, tk):
    assert tq == tk and tk >= MAX_DISTANCE
    return pl.pallas_call(
        functools.partial(_bias_body, tq=tq, tk=tk),
        grid=(N_SELF_HEADS,),
        in_specs=[pl.BlockSpec(memory_space=pltpu.SMEM)],
        out_specs=pl.BlockSpec((None, 2, tq, tk), lambda h: (h, 0, 0, 0)),
        out_shape=jax.ShapeDtypeStruct((N_SELF_HEADS, 2, tq, tk), F32),
        compiler_params=_params(("arbitrary",)),
        name="bias_tiles",
    )(rel_bias)


class _Stream(NamedTuple):
    qs_ref: Any
    s_ref: Any
    m_ref: Any
    l_ref: Any
    acc_ref: Any
    k_rows: Callable
    v_rows: Callable
    bias: Callable


TILE_GROUPS = (4, 2, 1)


def _lane_fold(x, op):
    out = x[:, :LANES]
    for c in range(1, x.shape[1] // LANES):
        out = op(out, x[:, c * LANES:(c + 1) * LANES])
    return out


def _key_rows(j, tk, n=1):
    return pl.ds(pl.multiple_of(j * tk, tk), n * tk)


def _for_tile_groups(lo, hi, body):
    for u in TILE_GROUPS:
        n = (hi - lo) // u

        def trip(t, c, u=u, lo=lo):
            body(lo + t * u, u)
            return c

        lax.fori_loop(0, n, trip, 0)
        lo = lo + n * u


def _two_pass_attention(streams, i, tk):
    for st in streams:
        st.m_ref[...] = jnp.full(st.m_ref.shape, NEG_INF, F32)
        st.l_ref[...] = jnp.zeros(st.l_ref.shape, F32)
        st.acc_ref[...] = jnp.zeros(st.acc_ref.shape, F32)

    def score(j0, u, near=None):
        for st in streams:
            s = _dot_nt(st.qs_ref[...], st.k_rows(_key_rows(j0, tk, u)))
            m = st.m_ref[...]
            for t in range(u):
                s_t = st.bias(s[:, t * tk:(t + 1) * tk], j0 + t, near)
                st.s_ref[j0 + t] = s_t
                m = jnp.maximum(m, _lane_fold(s_t, jnp.maximum))
            st.m_ref[...] = m

    n_far = jnp.maximum(i - 1, 0)
    _for_tile_groups(0, n_far, score)
    lax.fori_loop(n_far, i, lambda j, c: (score(j, 1, 1), c)[1], 0)
    score(i, 1, 0)

    for st in streams:
        m = jnp.max(st.m_ref[...], axis=-1, keepdims=True)
        st.m_ref[...] = jnp.broadcast_to(m, st.m_ref.shape)

    def pv(j0, u):
        for st in streams:
            m = jnp.concatenate([st.m_ref[...]] * (tk // LANES), axis=1)
            p = jnp.concatenate([jnp.exp2(st.s_ref[j0 + t] - m) for t in range(u)], axis=1)
            st.l_ref[...] += _lane_fold(p, jnp.add)
            st.acc_ref[...] += _dot(p.astype(BF16), st.v_rows(_key_rows(j0, tk, u)))

    _for_tile_groups(0, i + 1, pv)


def _diff_body(lq1_ref, lk1_ref, lq2_ref, lk2_ref, q_ref, k_ref, v_ref, tab_ref, gsub_ref, o_ref,
               qs_ref, s_ref, m_ref, l_ref, acc_ref, *, tq, tk, hp, lambda_init):
    i = pl.program_id(2)
    lane = lax.broadcasted_iota(jnp.int32, (tq, HEAD_DIM), 1)
    streams = []
    for hh in range(hp):
        cols = slice(hh * HEAD_DIM, (hh + 1) * HEAD_DIM)
        q = q_ref[:, cols]
        zero = jnp.zeros_like(q)
        qs_ref[hh, :tq] = jnp.where(lane < DIFF_QK_DIM, q, zero)
        qs_ref[hh, tq:] = jnp.where(lane >= DIFF_QK_DIM, q, zero)

        def bias(s, j, near, hh=hh):
            if near is None:
                return s
            tab = tab_ref[hh, near]
            return s + jnp.concatenate([tab, tab], axis=0)

        streams.append(_Stream(
            qs_ref.at[hh], s_ref.at[hh], m_ref.at[hh], l_ref.at[hh], acc_ref.at[hh],
            lambda rows, cols=cols: k_ref[rows, cols],
            lambda rows, cols=cols: v_ref[rows, cols],
            bias))

    _two_pass_attention(streams, i, tk)

    lam = (jnp.exp(jnp.sum(lq1_ref[...] * lk1_ref[...], axis=-1, keepdims=True))
           - jnp.exp(jnp.sum(lq2_ref[...] * lk2_ref[...], axis=-1, keepdims=True)) + lambda_init)
    for hh in range(hp):
        out = acc_ref[hh] / jnp.sum(l_ref[hh], axis=-1, keepdims=True)
        o = out[:tq] - lam * out[tq:]
        o_ref[:, hh * HEAD_DIM:(hh + 1) * HEAD_DIM] = (
            _rms_rows(o, gsub_ref[...]) * (1.0 - lambda_init)).astype(o_ref.dtype)


def _diff_attn(p3, tab, lq1, lk1, lq2, lk2, g_sub, lambda_init, tq):
    B, T, _ = p3.shape
    H = N_SELF_HEADS
    hp = 2
    nq = T // tq
    w = hp * HEAD_DIM
    vec = lambda a: a.reshape(1, -1).astype(F32)
    small = lambda n: pl.BlockSpec((1, n), lambda b, h, i: (0, 0))
    return pl.pallas_call(
        functools.partial(_diff_body, tq=tq, tk=tq, hp=hp, lambda_init=lambda_init),
        grid=(B, H // hp, nq),
        in_specs=[
            small(DIFF_QK_DIM), small(DIFF_QK_DIM), small(DIFF_QK_DIM), small(DIFF_QK_DIM),
            pl.BlockSpec((None, tq, w), lambda b, h, i: (b, i, h)),
            pl.BlockSpec((None, T, w), lambda b, h, i: (b, 0, H // hp + h)),
            pl.BlockSpec((None, T, w), lambda b, h, i: (b, 0, 2 * (H // hp) + h)),
            pl.BlockSpec((hp, 2, tq, tq), lambda b, h, i: (h, 0, 0, 0)),
            small(HEAD_DIM),
        ],
        out_specs=pl.BlockSpec((None, tq, w), lambda b, h, i: (b, i, h)),
        out_shape=jax.ShapeDtypeStruct((B, T, SELF_WIDTH), BF16),
        scratch_shapes=[
            pltpu.VMEM((hp, 2 * tq, HEAD_DIM), BF16),
            pltpu.VMEM((hp, nq, 2 * tq, tq), F32),
            pltpu.VMEM((hp, 2 * tq, LANES), F32),
            pltpu.VMEM((hp, 2 * tq, LANES), F32),
            pltpu.VMEM((hp, 2 * tq, HEAD_DIM), F32),
        ],
        compiler_params=_params(("parallel", "parallel", "arbitrary")),
        name="diff_attn",
    )(vec(lq1), vec(lk1), vec(lq2), vec(lk2), p3, p3, p3, tab, vec(g_sub))


SEARCH_ROWS = 128


def _sortable(x):
    bits = pltpu.bitcast(x, jnp.int32)
    return jnp.where(bits < 0, bits ^ jnp.int32(0x7FFFFFFF), bits)


def _dsa_select(iq_ref, ik_ref, iw_ref, key_ref, madd_ref, iqm_ref, thr_ref, need_ref, i,
                *, tq, tk, topk):
    n_tiles = i + 1
    lane = lax.broadcasted_iota(jnp.int32, (tq, LANES), 1)
    for hh in range(IDX_HEADS):
        pair = iq_ref[:, (hh // 2) * LANES:(hh // 2 + 1) * LANES]
        keep = (lane < IDX_DIM) if hh % 2 == 0 else (lane >= IDX_DIM)
        iqm_ref[hh] = jnp.where(keep, pair, jnp.zeros_like(pair))

    row = lax.broadcasted_iota(jnp.int32, (tq, tk), 0)
    col = lax.broadcasted_iota(jnp.int32, (tq, tk), 1)

    def causal(j):
        return (i * tq + row) >= (j * tk + col)

    def score_tile(j, c):
        ik = ik_ref[_key_rows(j, tk), :]
        sc = jnp.zeros((tq, tk), F32)
        for hh in range(IDX_HEADS):
            sc = sc + jnp.maximum(_dot_nt(iqm_ref[hh], ik), 0.0) * iw_ref[:, hh:hh + 1]
        key_ref[j] = _sortable(jnp.where(causal(j), sc, NEG_INF))
        return c

    lax.fori_loop(0, n_tiles, score_tile, 0)

    kf = float(topk)
    blocks = [slice(r0, r0 + SEARCH_ROWS) for r0 in range(0, tq, SEARCH_ROWS)]

    def count(cmp, cands):
        cands_b = [jnp.broadcast_to(c, (SEARCH_ROWS, LANES)) for c in cands]

        def body(j, parts):
            out = []
            for rows, cand_b, part in zip(blocks, cands_b, parts):
                k = key_ref[j, rows, :]
                for c in range(tk // LANES):
                    hit = cmp(k[:, c * LANES:(c + 1) * LANES], cand_b)
                    part = part + jnp.where(hit, 1.0, 0.0)
                out.append(part)
            return tuple(out)

        zeros = tuple(jnp.zeros((SEARCH_ROWS, LANES), F32) for _ in blocks)
        parts = lax.fori_loop(0, n_tiles, body, zeros)
        return [jnp.sum(p, axis=-1, keepdims=True) for p in parts]

    ge = lambda k, c: k >= c
    zero = jnp.zeros((SEARCH_ROWS, 1), jnp.int32)
    bases = tuple(jnp.where(n >= kf, zero, jnp.int32(INT32_MIN))
                  for n in count(ge, [zero] * len(blocks)))

    def bit_step(b, bases):
        bit = lax.shift_left(jnp.int32(1), jnp.int32(30) - b)
        cands = [base | bit for base in bases]
        return tuple(jnp.where(n >= kf, cand, base)
                     for n, cand, base in zip(count(ge, cands), cands, bases))

    thrs = lax.fori_loop(0, 31, bit_step, bases)
    needs = [kf - n for n in count(lambda k, c: k > c, thrs)]
    for rows, thr, need in zip(blocks, thrs, needs):
        thr_ref[rows, :] = jnp.broadcast_to(thr, (SEARCH_ROWS, LANES))
        need_ref[rows, :] = jnp.broadcast_to(need, (SEARCH_ROWS, LANES))

    upper = jnp.where(row < col, 1.0, 0.0).astype(BF16)
    reps = tk // LANES

    def mask_tile(j, seen):
        key = key_ref[j]
        thr = jnp.concatenate([thr_ref[...]] * reps, axis=1)
        need = jnp.concatenate([need_ref[...]] * reps, axis=1)
        eq = jnp.where(key == thr, 1.0, 0.0)
        rank = seen + _dot(eq.astype(BF16), upper)
        take = (key > thr) | ((key == thr) & (rank < need))
        madd_ref[j] = jnp.where(take & causal(j), 0.0, NEG_INF)
        return seen + jnp.sum(eq, axis=-1, keepdims=True)

    lax.fori_loop(0, n_tiles, mask_tile, jnp.zeros((tq, 1), F32))


def _dsa_body(q_ref, k_ref, v_ref, iq_ref, ik_ref, iw_ref, tab_ref, o_ref,
              key_ref, madd_ref, iqm_ref, thr_ref, need_ref, qs_ref, s_ref, m_ref, l_ref, acc_ref,
              *, tq, tk, topk):
    i = pl.program_id(1)
    R = DSA_GROUP

    @pl.when(pl.program_id(2) == 0)
    def _():
        _dsa_select(iq_ref, ik_ref, iw_ref, key_ref, madd_ref, iqm_ref, thr_ref, need_ref, i,
                    tq=tq, tk=tk, topk=topk)

    for r in range(R):
        qs_ref[r * tq:(r + 1) * tq] = q_ref[:, r * HEAD_DIM:(r + 1) * HEAD_DIM]

    def bias(s, j, near):
        madd = madd_ref[j]
        s = s + jnp.concatenate([madd] * R, axis=0)
        if near is not None:
            s = s + jnp.concatenate([tab_ref[r, near] for r in range(R)], axis=0)
        return s

    stream = _Stream(qs_ref, s_ref, m_ref, l_ref, acc_ref,
                     lambda rows: k_ref[rows, :], lambda rows: v_ref[rows, :], bias)
    _two_pass_attention([stream], i, tk)

    out = acc_ref[...] / jnp.sum(l_ref[...], axis=-1, keepdims=True)
    for r in range(R):
        o_ref[:, r * HEAD_DIM:(r + 1) * HEAD_DIM] = out[r * tq:(r + 1) * tq].astype(o_ref.dtype)


def _dsa_attn(p3, ik2, iw, tab, tq):
    B, T, _ = p3.shape
    H, R, G = N_SELF_HEADS, DSA_GROUP, DSA_KV_HEADS
    topk = min(TOPK_MAX, T // 4)
    nq = T // tq
    iq_w = IDX_HEADS * IDX_DIM
    k0 = H
    iq0 = (H + G) * HEAD_DIM // iq_w
    v0 = H + G + iq_w // HEAD_DIM
    assert (H + G) * HEAD_DIM % iq_w == 0
    return pl.pallas_call(
        functools.partial(_dsa_body, tq=tq, tk=tq, topk=topk),
        grid=(B, nq, G),
        in_specs=[
            pl.BlockSpec((None, tq, R * HEAD_DIM), lambda b, i, g: (b, i, g)),
            pl.BlockSpec((None, T, HEAD_DIM), lambda b, i, g: (b, 0, k0 + g)),
            pl.BlockSpec((None, T, HEAD_DIM), lambda b, i, g: (b, 0, v0 + g)),
            pl.BlockSpec((None, tq, iq_w), lambda b, i, g: (b, i, iq0)),
            pl.BlockSpec((None, T, LANES), lambda b, i, g: (b, 0, 0)),
            pl.BlockSpec((None, tq, IDX_HEADS), lambda b, i, g: (b, i, 0)),
            pl.BlockSpec((R, 2, tq, tq), lambda b, i, g: (g, 0, 0, 0)),
        ],
        out_specs=pl.BlockSpec((None, tq, R * HEAD_DIM), lambda b, i, g: (b, i, g)),
        out_shape=jax.ShapeDtypeStruct((B, T, SELF_WIDTH), BF16),
        scratch_shapes=[
            pltpu.VMEM((nq, tq, tq), jnp.int32),
            pltpu.VMEM((nq, tq, tq), F32),
            pltpu.VMEM((IDX_HEADS, tq, LANES), BF16),
            pltpu.VMEM((tq, LANES), jnp.int32),
            pltpu.VMEM((tq, LANES), F32),
            pltpu.VMEM((R * tq, HEAD_DIM), BF16),
            pltpu.VMEM((nq, R * tq, tq), F32),
            pltpu.VMEM((R * tq, LANES), F32),
            pltpu.VMEM((R * tq, LANES), F32),
            pltpu.VMEM((R * tq, HEAD_DIM), F32),
        ],
        compiler_params=_params(("parallel", "arbitrary", "arbitrary")),
        name="dsa_attn",
    )(p3, p3, p3, p3, ik2, iw, tab)


def _mem_body(q_ref, k_ref, v_ref, o_ref):
    for h in range(N_MEM_HEADS):
        sl = slice(h * HEAD_DIM, (h + 1) * HEAD_DIM)
        s = _dot_nt(q_ref[:, sl], k_ref[:, sl])
        p = jnp.exp2(s - jnp.max(s, axis=-1, keepdims=True))
        o = _dot(p.astype(BF16), v_ref[:, sl]) / jnp.sum(p, axis=-1, keepdims=True)
        o_ref[:, sl] = o.astype(o_ref.dtype)


def _mem_attn(p3, kv3, qm_block):
    B, T, _ = p3.shape
    N = kv3.shape[1]
    tq = min(512, T)
    return pl.pallas_call(
        _mem_body,
        grid=(B, T // tq),
        in_specs=[
            pl.BlockSpec((None, tq, MEM_WIDTH), lambda b, i: (b, i, qm_block)),
            pl.BlockSpec((None, N, MEM_WIDTH), lambda b, i: (b, 0, 0)),
            pl.BlockSpec((None, N, MEM_WIDTH), lambda b, i: (b, 0, 1)),
        ],
        out_specs=pl.BlockSpec((None, tq, MEM_WIDTH), lambda b, i: (b, i, 0)),
        out_shape=jax.ShapeDtypeStruct((B, T, MEM_WIDTH), BF16),
        compiler_params=_params(("parallel", "arbitrary")),
        name="mem_attn",
    )(p3, kv3, kv3)


def _out_body(ys_ref, ym_ref, ws_ref, wm_ref, x_ref, o_ref):
    o_ref[...] = x_ref[...] + _dot(ys_ref[...], ws_ref[...]) + _dot(ym_ref[...], wm_ref[...])


def _out_proj(ys, ym, w_self, w_mem, x2):
    M, D = x2.shape
    tm = min(512, M)
    return pl.pallas_call(
        _out_body,
        grid=(M // tm,),
        in_specs=[
            pl.BlockSpec((tm, SELF_WIDTH), lambda i: (i, 0)),
            pl.BlockSpec((tm, MEM_WIDTH), lambda i: (i, 0)),
            pl.BlockSpec((SELF_WIDTH, D), lambda i: (0, 0)),
            pl.BlockSpec((MEM_WIDTH, D), lambda i: (0, 0)),
            pl.BlockSpec((tm, D), lambda i: (i, 0)),
        ],
        out_specs=pl.BlockSpec((tm, D), lambda i: (i, 0)),
        out_shape=jax.ShapeDtypeStruct((M, D), F32),
        compiler_params=_params(("parallel",)),
        name="out_proj",
    )(ys, ym, w_self, w_mem, x2)


def kernel(x, mem, rel_bias, ffn1_g, ffn1_w_gate, ffn1_w_up, ffn1_w_down, ffn2_g, ffn2_w_gate,
           ffn2_w_up, ffn2_w_down, mix_g, mem_g, mem_w_kv, mem_gq, mem_gk, w_out, a_w_in, a_gq, a_gk,
           a_lam_q1, a_lam_k1, a_lam_q2, a_lam_k2, a_g_sub, b_w_in, b_gq, b_gk):
    B, T, D = x.shape
    N = mem.shape[1]
    depth = ffn1_g.shape[0]
    M = B * T
    tq = min(256, T)
    head_scale = HEAD_DIM ** -0.5 * LOG2E
    ones = lambda n: jnp.ones((n,), F32)
    bf = lambda w: w.astype(BF16)

    x2 = x.reshape(M, D)
    mem2 = mem.reshape(B * N, D)
    tab = _bias_tiles(rel_bias, tq, tq)

    for i in range(depth):
        x2 = _ffn(x2, ffn1_g[i], bf(ffn1_w_gate[i]), bf(ffn1_w_up[i]), bf(0.5 * ffn1_w_down[i]))

        kv_gain = jnp.concatenate([jnp.tile(mem_gk[i], N_MEM_HEADS), ones(MEM_WIDTH)])
        kv3 = _proj(mem2, mem_g[i], bf(mem_w_kv[i]), [2, 0], kv_gain, BF16, MEM_WIDTH)
        kv3 = kv3.reshape(B, N, 2 * MEM_WIDTH)
        qm_gain = jnp.tile(mem_gq[i], N_MEM_HEADS) * head_scale
        j = i // 2

        if i % 2 == 0:
            n_grp = SELF_WIDTH // DIFF_QK_DIM
            gain = jnp.concatenate([jnp.tile(a_gq[j], n_grp) * (DIFF_QK_DIM ** -0.5 * LOG2E),
                                    jnp.tile(a_gk[j], n_grp), ones(SELF_WIDTH), qm_gain])
            kinds = [1] * 6 + [0] * 3 + [2]
            p3 = _proj(x2, mix_g[i], bf(a_w_in[j]), kinds, gain, BF16, 512).reshape(B, T, -1)
            lambda_init = 0.8 - 0.6 * math.exp(-0.3 * i)
            y_self = _diff_attn(p3, tab, a_lam_q1[j], a_lam_k1[j], a_lam_q2[j], a_lam_k2[j],
                                a_g_sub[j], lambda_init, tq)
            qm_block = 3 * SELF_WIDTH // MEM_WIDTH
        else:
            w = b_w_in[j]
            kvw = DSA_KV_HEADS * HEAD_DIM
            iqw = IDX_HEADS * IDX_DIM
            c_q, c_k, c_v, c_iq = 0, SELF_WIDTH, SELF_WIDTH + kvw, SELF_WIDTH + 2 * kvw
            c_ik = c_iq + iqw
            c_iw = c_ik + IDX_DIM
            c_qm = c_iw + IDX_HEADS
            w_main = jnp.concatenate([w[:, c_q:c_k], w[:, c_k:c_v], w[:, c_iq:c_ik], w[:, c_v:c_iq],
                                      w[:, c_qm:]], axis=1)
            gain = jnp.concatenate([jnp.tile(b_gq[j], N_SELF_HEADS) * head_scale,
                                    jnp.tile(b_gk[j], DSA_KV_HEADS), ones(iqw), ones(kvw), qm_gain])
            kinds = [2, 2, 2, 2, 0, 0, 0, 2]
            p3 = _proj(x2, mix_g[i], bf(w_main), kinds, gain, BF16, 512).reshape(B, T, -1)
            pad = LANES - IDX_DIM - IDX_HEADS
            w_idx = jnp.concatenate([w[:, c_ik:c_qm], jnp.zeros((D, pad), F32)], axis=1)
            idx_gain = jnp.concatenate([ones(IDX_DIM),
                                        jnp.full((IDX_HEADS,), (IDX_DIM * IDX_HEADS) ** -0.5, F32),
                                        jnp.zeros((pad,), F32)])
            ikw = _proj(x2, mix_g[i], bf(w_idx), [0], idx_gain, F32, LANES).reshape(B, T, LANES)
            ik = ikw[:, :, :IDX_DIM].astype(BF16)
            ik2 = jnp.concatenate([ik, ik], axis=-1)
            iw = ikw[:, :, IDX_DIM:IDX_DIM + IDX_HEADS]
            y_self = _dsa_attn(p3, ik2, iw, tab, tq)
            qm_block = (p3.shape[-1] - MEM_WIDTH) // MEM_WIDTH

        y_mem = _mem_attn(p3, kv3, qm_block)
        wo = bf(w_out[i])
        x2 = _out_proj(y_self.reshape(M, SELF_WIDTH), y_mem.reshape(M, MEM_WIDTH),
                       wo[:SELF_WIDTH], wo[SELF_WIDTH:], x2)

        x2 = _ffn(x2, ffn2_g[i], bf(ffn2_w_gate[i]), bf(ffn2_w_up[i]), bf(0.5 * ffn2_w_down[i]))

    return x2.reshape(B, T, D)
```

```python
import functools
import math
from typing import Any, Callable, NamedTuple

import jax
import jax.numpy as jnp
from jax import lax
from jax.experimental import pallas as pl
from jax.experimental.pallas import tpu as pltpu

HEAD_DIM = 128
N_SELF_HEADS = 12
N_MEM_HEADS = 4
SELF_WIDTH = N_SELF_HEADS * HEAD_DIM
MEM_WIDTH = N_MEM_HEADS * HEAD_DIM
DIFF_QK_DIM = HEAD_DIM // 2
DSA_KV_HEADS = 4
DSA_GROUP = N_SELF_HEADS // DSA_KV_HEADS
IDX_HEADS = 16
IDX_DIM = 64
TOPK_MAX = 256
NUM_BUCKETS = 32
MAX_DISTANCE = 128
EPS = 1e-6
NEG_INF = -1e30
INT32_MIN = -(2 ** 31)
LOG2E = math.log2(math.e)

FAR_TILE = 2

LANES = 128
VMEM_LIMIT = 56 * 1024 * 1024

F32 = jnp.float32
BF16 = jnp.bfloat16


def _dot(a, b):
    return jnp.dot(a, b, preferred_element_type=F32)


def _dot_nt(a, b):
    return lax.dot_general(a, b, (((1,), (1,)), ((), ())), preferred_element_type=F32)


def _params(semantics):
    return pltpu.CompilerParams(dimension_semantics=semantics, vmem_limit_bytes=VMEM_LIMIT)


def _rms_rows(x, g):
    ms = jnp.mean(x * x, axis=-1, keepdims=True)
    return x * lax.rsqrt(ms + EPS) * g


def _ffn_body(x_ref, g_ref, wg_ref, wu_ref, wd_ref, o_ref, h_ref):
    @pl.when(pl.program_id(1) == 0)
    def _():
        x = x_ref[...]
        h_ref[...] = _rms_rows(x, g_ref[...]).astype(BF16)
        o_ref[...] = x

    h = h_ref[...]
    gate = _dot(h, wg_ref[...])
    up = _dot(h, wu_ref[...])
    act = (gate * jax.nn.sigmoid(gate) * up).astype(BF16)
    o_ref[...] += _dot(act, wd_ref[...])


def _ffn(x2, g, wg, wu, wd_half, layer):
    M, D = x2.shape
    F = wg.shape[2]
    tm = min(512, M)
    tf = min(512, F)
    return pl.pallas_call(
        _ffn_body,
        grid=(M // tm, F // tf),
        in_specs=[
            pl.BlockSpec((tm, D), lambda i, f: (i, 0)),
            pl.BlockSpec((1, D), lambda i, f: (0, 0)),
            pl.BlockSpec((None, D, tf), lambda i, f: (layer, 0, f)),
            pl.BlockSpec((None, D, tf), lambda i, f: (layer, 0, f)),
            pl.BlockSpec((None, tf, D), lambda i, f: (layer, f, 0)),
        ],
        out_specs=pl.BlockSpec((tm, D), lambda i, f: (i, 0)),
        out_shape=jax.ShapeDtypeStruct((M, D), F32),
        scratch_shapes=[pltpu.VMEM((tm, D), BF16)],
        compiler_params=_params(("parallel", "arbitrary")),
        name="ffn",
    )(x2, g.reshape(1, D), wg, wu, wd_half)


def _proj_body(types_ref, x_ref, g_ref, w_ref, gmat_ref, gain_ref, o_ref, h_ref, *, tn):
    j = pl.program_id(1)

    @pl.when(j == 0)
    def _():
        h_ref[...] = _rms_rows(x_ref[...], g_ref[...]).astype(BF16)

    y = _dot(h_ref[...], w_ref[...])
    kind = types_ref[j]

    @pl.when(kind == 0)
    def _():
        o_ref[...] = (y * gain_ref[...]).astype(o_ref.dtype)

    @pl.when(kind != 0)
    def _():
        gmat = gmat_ref[...]
        for c in range(tn // LANES):
            sl = slice(c * LANES, (c + 1) * LANES)
            yc = y[:, sl]
            ms = _dot((yc * yc).astype(BF16), gmat)
            o_ref[:, sl] = (yc * lax.rsqrt(ms + EPS) * gain_ref[:, sl]).astype(o_ref.dtype)


def _group_mats():
    r = jnp.arange(LANES)
    same64 = (r[:, None] // DIFF_QK_DIM) == (r[None, :] // DIFF_QK_DIM)
    return jnp.stack([
        jnp.zeros((LANES, LANES), F32),
        jnp.where(same64, 1.0 / DIFF_QK_DIM, 0.0),
        jnp.full((LANES, LANES), 1.0 / HEAD_DIM, F32),
    ]).astype(BF16)


def _proj(x2, g, w, layer, kinds, gain, out_dtype, tn):
    M, D = x2.shape
    N = w.shape[2]
    tm = min(1024, M)
    kinds = jnp.asarray(kinds, jnp.int32)
    grid_spec = pltpu.PrefetchScalarGridSpec(
        num_scalar_prefetch=1,
        grid=(M // tm, N // tn),
        in_specs=[
            pl.BlockSpec((tm, D), lambda i, j, t: (i, 0)),
            pl.BlockSpec((1, D), lambda i, j, t: (0, 0)),
            pl.BlockSpec((None, D, tn), lambda i, j, t: (layer, 0, j)),
            pl.BlockSpec((None, LANES, LANES), lambda i, j, t: (t[j], 0, 0)),
            pl.BlockSpec((1, tn), lambda i, j, t: (0, j)),
        ],
        out_specs=pl.BlockSpec((tm, tn), lambda i, j, t: (i, j)),
        scratch_shapes=[pltpu.VMEM((tm, D), BF16)],
    )
    return pl.pallas_call(
        functools.partial(_proj_body, tn=tn),
        grid_spec=grid_spec,
        out_shape=jax.ShapeDtypeStruct((M, N), out_dtype),
        compiler_params=_params(("parallel", "arbitrary")),
        name="proj",
    )(kinds, x2, g.reshape(1, D), w, _group_mats(), gain.reshape(1, N))


def _bias_body(rb_ref, o_ref, *, tq, tk):
    h = pl.program_id(0)
    r = lax.broadcasted_iota(jnp.int32, (tq, tk), 0)
    c = lax.broadcasted_iota(jnp.int32, (tq, tk), 1)
    max_exact = NUM_BUCKETS // 2
    far = rb_ref[NUM_BUCKETS - 1, h]
    for off in range(FAR_TILE + 1):
        dist = r - c + off * tk
        n = jnp.maximum(dist, 0)
        nf = jnp.maximum(n, 1).astype(F32)
        large = max_exact + (jnp.log(nf / max_exact) / math.log(MAX_DISTANCE / max_exact)
                             * (NUM_BUCKETS - max_exact)).astype(jnp.int32)
        large = jnp.minimum(large, NUM_BUCKETS - 1)
        bucket = jnp.where(n < max_exact, n, large)
        val = jnp.zeros((tq, tk), F32)
        for b in range(NUM_BUCKETS):
            val = jnp.where(bucket == b, rb_ref[b, h], val)
        o_ref[off] = jnp.where(dist >= 0, (val - far) * LOG2E, NEG_INF)


def _bias_tiles(rel_bias, tq, tk):
    assert tq == tk and tk >= MAX_DISTANCE
    return pl.pallas_call(
        functools.partial(_bias_body, tq=tq, tk=tk),
        grid=(N_SELF_HEADS,),
        in_specs=[pl.BlockSpec(memory_space=pltpu.SMEM)],
        out_specs=pl.BlockSpec((None, FAR_TILE + 1, tq, tk), lambda h: (h, 0, 0, 0)),
        out_shape=jax.ShapeDtypeStruct((N_SELF_HEADS, FAR_TILE + 1, tq, tk), F32),
        compiler_params=_params(("arbitrary",)),
        name="bias_tiles",
    )(rel_bias)


class _Stream(NamedTuple):
    qs_ref: Any
    k_rows: Callable
    v_rows: Callable
    bias: Callable
    scratch: tuple


TILE_GROUPS = (8, 4, 2, 1)
MAX_EXP2_ARG = 96.0


def _attention_scratch(rows, n_tiles, tk, bounded, lead=()):
    if bounded:
        return [pltpu.VMEM(lead + (rows, 2 * HEAD_DIM), F32)]
    return [
        pltpu.VMEM(lead + (n_tiles, rows, tk), F32),
        pltpu.VMEM(lead + (rows, LANES), F32),
        pltpu.VMEM(lead + (rows, LANES), F32),
        pltpu.VMEM(lead + (rows, HEAD_DIM), F32),
    ]


def _attend(streams, i, tk, bounded):
    return (_bounded_attention if bounded else _two_pass_attention)(streams, i, tk)


def _lane_fold(x, op):
    out = x[:, :LANES]
    for c in range(1, x.shape[1] // LANES):
        out = op(out, x[:, c * LANES:(c + 1) * LANES])
    return out


def _key_rows(j, tk, n=1):
    return pl.ds(pl.multiple_of(j * tk, tk), n * tk)


def _for_tile_groups(lo, hi, body):
    for u in TILE_GROUPS:
        n = (hi - lo) // u

        def trip(t, c, u=u, lo=lo):
            body(lo + t * u, u)
            return c

        lax.fori_loop(0, n, trip, 0)
        lo = lo + n * u


def _two_pass_attention(streams, i, tk):
    for st in streams:
        _, m_ref, l_ref, acc_ref = st.scratch
        m_ref[...] = jnp.full(m_ref.shape, NEG_INF, F32)
        l_ref[...] = jnp.zeros(l_ref.shape, F32)
        acc_ref[...] = jnp.zeros(acc_ref.shape, F32)

    def score(j0, u):
        for st in streams:
            s_ref, m_ref = st.scratch[:2]
            s = _dot_nt(st.qs_ref[...], st.k_rows(_key_rows(j0, tk, u)))
            m = m_ref[...]
            for t in range(u):
                s_t = st.bias(s[:, t * tk:(t + 1) * tk], j0 + t)
                s_ref[j0 + t] = s_t
                m = jnp.maximum(m, _lane_fold(s_t, jnp.maximum))
            m_ref[...] = m

    _for_tile_groups(0, i + 1, score)

    for st in streams:
        m_ref = st.scratch[1]
        m = jnp.max(m_ref[...], axis=-1, keepdims=True)
        m_ref[...] = jnp.broadcast_to(m, m_ref.shape)

    def pv(j0, u):
        for st in streams:
            s_ref, m_ref, l_ref, acc_ref = st.scratch
            m = jnp.concatenate([m_ref[...]] * (tk // LANES), axis=1)
            p = jnp.concatenate([jnp.exp2(s_ref[j0 + t] - m) for t in range(u)], axis=1)
            l_ref[...] += _lane_fold(p, jnp.add)
            acc_ref[...] += _dot(p.astype(BF16), st.v_rows(_key_rows(j0, tk, u)))

    _for_tile_groups(0, i + 1, pv)
    return [st.scratch[3][...] / jnp.sum(st.scratch[2][...], axis=-1, keepdims=True) for st in streams]


def _bounded_attention(streams, i, tk):
    for st in streams:
        acc_ref, = st.scratch
        acc_ref[...] = jnp.zeros(acc_ref.shape, F32)

    def step(j0, u):
        for st in streams:
            acc_ref, = st.scratch
            rows = _key_rows(j0, tk, u)
            s = _dot_nt(st.qs_ref[...], st.k_rows(rows))
            p = jnp.concatenate(
                [jnp.exp2(st.bias(s[:, t * tk:(t + 1) * tk], j0 + t)).astype(BF16) for t in range(u)],
                axis=1)
            v = st.v_rows(rows)
            acc_ref[...] += _dot(p, jnp.concatenate([v, jnp.ones_like(v)], axis=1))

    _for_tile_groups(0, i + 1, step)
    return [st.scratch[0][:, :HEAD_DIM] / st.scratch[0][:, HEAD_DIM:] for st in streams]


def _diff_body(lq1_ref, lk1_ref, lq2_ref, lk2_ref, q_ref, k_ref, v_ref, tab_ref, gsub_ref, o_ref,
               qs_ref, *scratch, tq, tk, hp, lambda_init, bounded):
    i = pl.program_id(2)
    lane = lax.broadcasted_iota(jnp.int32, (tq, HEAD_DIM), 1)
    streams = []
    for hh in range(hp):
        cols = slice(hh * HEAD_DIM, (hh + 1) * HEAD_DIM)
        q = q_ref[:, cols]
        zero = jnp.zeros_like(q)
        qs_ref[hh, :tq] = jnp.where(lane < DIFF_QK_DIM, q, zero)
        qs_ref[hh, tq:] = jnp.where(lane >= DIFF_QK_DIM, q, zero)

        def bias(s, j, hh=hh):
            tab = tab_ref[hh, jnp.minimum(i - j, FAR_TILE)]
            return s + jnp.concatenate([tab, tab], axis=0)

        streams.append(_Stream(
            qs_ref.at[hh],
            lambda rows, cols=cols: k_ref[rows, cols],
            lambda rows, cols=cols: v_ref[rows, cols],
            bias, tuple(r.at[hh] for r in scratch)))

    outs = _attend(streams, i, tk, bounded)

    lam = (jnp.exp(jnp.sum(lq1_ref[...] * lk1_ref[...], axis=-1, keepdims=True))
           - jnp.exp(jnp.sum(lq2_ref[...] * lk2_ref[...], axis=-1, keepdims=True)) + lambda_init)
    for hh, out in enumerate(outs):
        o = out[:tq] - lam * out[tq:]
        o_ref[:, hh * HEAD_DIM:(hh + 1) * HEAD_DIM] = (
            _rms_rows(o, gsub_ref[...]) * (1.0 - lambda_init)).astype(o_ref.dtype)


def _diff_attn(p3, tab, lq1, lk1, lq2, lk2, g_sub, lambda_init, tq, bounded):
    B, T, _ = p3.shape
    H = N_SELF_HEADS
    hp = 2
    nq = T // tq
    w = hp * HEAD_DIM
    vec = lambda a: a.reshape(1, -1).astype(F32)
    small = lambda n: pl.BlockSpec((1, n), lambda b, h, i: (0, 0))
    return pl.pallas_call(
        functools.partial(_diff_body, tq=tq, tk=tq, hp=hp, lambda_init=lambda_init, bounded=bounded),
        grid=(B, H // hp, nq),
        in_specs=[
            small(DIFF_QK_DIM), small(DIFF_QK_DIM), small(DIFF_QK_DIM), small(DIFF_QK_DIM),
            pl.BlockSpec((None, tq, w), lambda b, h, i: (b, i, h)),
            pl.BlockSpec((None, T, w), lambda b, h, i: (b, 0, H // hp + h)),
            pl.BlockSpec((None, T, w), lambda b, h, i: (b, 0, 2 * (H // hp) + h)),
            pl.BlockSpec((hp, FAR_TILE + 1, tq, tq), lambda b, h, i: (h, 0, 0, 0)),
            small(HEAD_DIM),
        ],
        out_specs=pl.BlockSpec((None, tq, w), lambda b, h, i: (b, i, h)),
        out_shape=jax.ShapeDtypeStruct((B, T, SELF_WIDTH), BF16),
        scratch_shapes=[pltpu.VMEM((hp, 2 * tq, HEAD_DIM), BF16)]
        + _attention_scratch(2 * tq, nq, tq, bounded, lead=(hp,)),
        compiler_params=_params(("parallel", "parallel", "arbitrary")),
        name="diff_attn",
    )(vec(lq1), vec(lk1), vec(lq2), vec(lk2), p3, p3, p3, tab, vec(g_sub))


SEARCH_ROWS = 128


def _sortable(x):
    bits = pltpu.bitcast(x, jnp.int32)
    return jnp.where(bits < 0, bits ^ jnp.int32(0x7FFFFFFF), bits)


def _dsa_select(iq_ref, ik_ref, iw_ref, key_ref, madd_ref, iqm_ref, thr_ref, need_ref, i,
                *, tq, tk, topk):
    n_tiles = i + 1
    lane = lax.broadcasted_iota(jnp.int32, (tq, LANES), 1)
    for hh in range(IDX_HEADS):
        pair = iq_ref[:, (hh // 2) * LANES:(hh // 2 + 1) * LANES]
        keep = (lane < IDX_DIM) if hh % 2 == 0 else (lane >= IDX_DIM)
        iqm_ref[hh] = jnp.where(keep, pair, jnp.zeros_like(pair))

    row = lax.broadcasted_iota(jnp.int32, (tq, tk), 0)
    col = lax.broadcasted_iota(jnp.int32, (tq, tk), 1)

    def causal(j):
        return (i * tq + row) >= (j * tk + col)

    def score_tile(j, c):
        ik = ik_ref[_key_rows(j, tk), :]
        sc = jnp.zeros((tq, tk), F32)
        for hh in range(IDX_HEADS):
            sc = sc + jnp.maximum(_dot_nt(iqm_ref[hh], ik), 0.0) * iw_ref[:, hh:hh + 1]
        key_ref[j] = _sortable(jnp.where(causal(j), sc, NEG_INF))
        return c

    lax.fori_loop(0, n_tiles, score_tile, 0)

    kf = float(topk)
    blocks = [slice(r0, r0 + SEARCH_ROWS) for r0 in range(0, tq, SEARCH_ROWS)]

    def count(cmp, cands):
        cands_b = [jnp.broadcast_to(c, (SEARCH_ROWS, LANES)) for c in cands]

        def body(j, parts):
            out = []
            for rows, cand_b, part in zip(blocks, cands_b, parts):
                k = key_ref[j, rows, :]
                for c in range(tk // LANES):
                    hit = cmp(k[:, c * LANES:(c + 1) * LANES], cand_b)
                    part = part + jnp.where(hit, 1.0, 0.0)
                out.append(part)
            return tuple(out)

        zeros = tuple(jnp.zeros((SEARCH_ROWS, LANES), F32) for _ in blocks)
        parts = lax.fori_loop(0, n_tiles, body, zeros)
        return [jnp.sum(p, axis=-1, keepdims=True) for p in parts]

    ge = lambda k, c: k >= c
    zero = jnp.zeros((SEARCH_ROWS, 1), jnp.int32)
    bases = tuple(jnp.where(n >= kf, zero, jnp.int32(INT32_MIN))
                  for n in count(ge, [zero] * len(blocks)))

    def bit_step(b, bases):
        bit = lax.shift_left(jnp.int32(1), jnp.int32(30) - b)
        cands = [base | bit for base in bases]
        return tuple(jnp.where(n >= kf, cand, base)
                     for n, cand, base in zip(count(ge, cands), cands, bases))

    thrs = lax.fori_loop(0, 31, bit_step, bases)
    needs = [kf - n for n in count(lambda k, c: k > c, thrs)]
    for rows, thr, need in zip(blocks, thrs, needs):
        thr_ref[rows, :] = jnp.broadcast_to(thr, (SEARCH_ROWS, LANES))
        need_ref[rows, :] = jnp.broadcast_to(need, (SEARCH_ROWS, LANES))

    upper = jnp.where(row < col, 1.0, 0.0).astype(BF16)
    reps = tk // LANES

    def mask_tile(j, seen):
        key = key_ref[j]
        thr = jnp.concatenate([thr_ref[...]] * reps, axis=1)
        need = jnp.concatenate([need_ref[...]] * reps, axis=1)
        eq = jnp.where(key == thr, 1.0, 0.0)
        rank = seen + _dot(eq.astype(BF16), upper)
        take = (key > thr) | ((key == thr) & (rank < need))
        madd_ref[j] = jnp.where(take & causal(j), 0.0, NEG_INF)
        return seen + jnp.sum(eq, axis=-1, keepdims=True)

    lax.fori_loop(0, n_tiles, mask_tile, jnp.zeros((tq, 1), F32))


def _dsa_body(q_ref, k_ref, v_ref, iq_ref, ik_ref, iw_ref, tab_ref, o_ref,
              key_ref, madd_ref, iqm_ref, thr_ref, need_ref, qs_ref, *scratch, tq, tk, topk, bounded):
    i = pl.program_id(1)
    R = DSA_GROUP

    @pl.when(pl.program_id(2) == 0)
    def _():
        _dsa_select(iq_ref, ik_ref, iw_ref, key_ref, madd_ref, iqm_ref, thr_ref, need_ref, i,
                    tq=tq, tk=tk, topk=topk)

    for r in range(R):
        qs_ref[r * tq:(r + 1) * tq] = q_ref[:, r * HEAD_DIM:(r + 1) * HEAD_DIM]

    def bias(s, j):
        back = jnp.minimum(i - j, FAR_TILE)
        return s + jnp.concatenate([madd_ref[j] + tab_ref[r, back] for r in range(R)], axis=0)

    stream = _Stream(qs_ref, lambda rows: k_ref[rows, :], lambda rows: v_ref[rows, :], bias, scratch)
    out, = _attend([stream], i, tk, bounded)
    for r in range(R):
        o_ref[:, r * HEAD_DIM:(r + 1) * HEAD_DIM] = out[r * tq:(r + 1) * tq].astype(o_ref.dtype)


def _dsa_attn(p3, ik2, iw, tab, tq, bounded):
    B, T, _ = p3.shape
    H, R, G = N_SELF_HEADS, DSA_GROUP, DSA_KV_HEADS
    topk = min(TOPK_MAX, T // 4)
    nq = T // tq
    iq_w = IDX_HEADS * IDX_DIM
    k0 = H
    iq0 = (H + G) * HEAD_DIM // iq_w
    v0 = H + G + iq_w // HEAD_DIM
    assert (H + G) * HEAD_DIM % iq_w == 0
    return pl.pallas_call(
        functools.partial(_dsa_body, tq=tq, tk=tq, topk=topk, bounded=bounded),
        grid=(B, nq, G),
        in_specs=[
            pl.BlockSpec((None, tq, R * HEAD_DIM), lambda b, i, g: (b, i, g)),
            pl.BlockSpec((None, T, HEAD_DIM), lambda b, i, g: (b, 0, k0 + g)),
            pl.BlockSpec((None, T, HEAD_DIM), lambda b, i, g: (b, 0, v0 + g)),
            pl.BlockSpec((None, tq, iq_w), lambda b, i, g: (b, i, iq0)),
            pl.BlockSpec((None, T, LANES), lambda b, i, g: (b, 0, 0)),
            pl.BlockSpec((None, tq, IDX_HEADS), lambda b, i, g: (b, i, 0)),
            pl.BlockSpec((R, FAR_TILE + 1, tq, tq), lambda b, i, g: (g, 0, 0, 0)),
        ],
        out_specs=pl.BlockSpec((None, tq, R * HEAD_DIM), lambda b, i, g: (b, i, g)),
        out_shape=jax.ShapeDtypeStruct((B, T, SELF_WIDTH), BF16),
        scratch_shapes=[
            pltpu.VMEM((nq, tq, tq), jnp.int32),
            pltpu.VMEM((nq, tq, tq), F32),
            pltpu.VMEM((IDX_HEADS, tq, LANES), BF16),
            pltpu.VMEM((tq, LANES), jnp.int32),
            pltpu.VMEM((tq, LANES), F32),
            pltpu.VMEM((R * tq, HEAD_DIM), BF16),
        ] + _attention_scratch(R * tq, nq, tq, bounded),
        compiler_params=_params(("parallel", "arbitrary", "arbitrary")),
        name="dsa_attn",
    )(p3, p3, p3, p3, ik2, iw, tab)


def _mem_body(q_ref, k_ref, v_ref, o_ref):
    for h in range(N_MEM_HEADS):
        sl = slice(h * HEAD_DIM, (h + 1) * HEAD_DIM)
        s = _dot_nt(q_ref[:, sl], k_ref[:, sl])
        p = jnp.exp2(s - jnp.max(s, axis=-1, keepdims=True))
        o = _dot(p.astype(BF16), v_ref[:, sl]) / jnp.sum(p, axis=-1, keepdims=True)
        o_ref[:, sl] = o.astype(o_ref.dtype)


def _mem_attn(p3, kv3, qm_block):
    B, T, _ = p3.shape
    N = kv3.shape[1]
    tq = min(512, T)
    return pl.pallas_call(
        _mem_body,
        grid=(B, T // tq),
        in_specs=[
            pl.BlockSpec((None, tq, MEM_WIDTH), lambda b, i: (b, i, qm_block)),
            pl.BlockSpec((None, N, MEM_WIDTH), lambda b, i: (b, 0, 0)),
            pl.BlockSpec((None, N, MEM_WIDTH), lambda b, i: (b, 0, 1)),
        ],
        out_specs=pl.BlockSpec((None, tq, MEM_WIDTH), lambda b, i: (b, i, 0)),
        out_shape=jax.ShapeDtypeStruct((B, T, MEM_WIDTH), BF16),
        compiler_params=_params(("parallel", "arbitrary")),
        name="mem_attn",
    )(p3, kv3, kv3)


def _out_body(ys_ref, ym_ref, ws_ref, wm_ref, x_ref, o_ref):
    o_ref[...] = x_ref[...] + _dot(ys_ref[...], ws_ref[...]) + _dot(ym_ref[...], wm_ref[...])


def _out_proj(ys, ym, wo, layer, x2):
    M, D = x2.shape
    tm = min(512, M)
    return pl.pallas_call(
        _out_body,
        grid=(M // tm,),
        in_specs=[
            pl.BlockSpec((tm, SELF_WIDTH), lambda i: (i, 0)),
            pl.BlockSpec((tm, MEM_WIDTH), lambda i: (i, 0)),
            pl.BlockSpec((None, SELF_WIDTH, D), lambda i: (layer, 0, 0)),
            pl.BlockSpec((None, MEM_WIDTH, D), lambda i: (layer, SELF_WIDTH // MEM_WIDTH, 0)),
            pl.BlockSpec((tm, D), lambda i: (i, 0)),
        ],
        out_specs=pl.BlockSpec((tm, D), lambda i: (i, 0)),
        out_shape=jax.ShapeDtypeStruct((M, D), F32),
        compiler_params=_params(("parallel",)),
        name="out_proj",
    )(ys, ym, wo, wo, x2)


def _scores_bounded(gq, gk, dim, scale, rel_bias):
    qk = dim * jnp.max(jnp.abs(gq)) * jnp.max(jnp.abs(gk)) * (scale * LOG2E * 1.02)
    bias = jnp.max(jnp.abs(rel_bias - rel_bias[NUM_BUCKETS - 1:])) * LOG2E
    return qk + bias <= MAX_EXP2_ARG


def kernel(x, mem, rel_bias, ffn1_g, ffn1_w_gate, ffn1_w_up, ffn1_w_down, ffn2_g, ffn2_w_gate,
           ffn2_w_up, ffn2_w_down, mix_g, mem_g, mem_w_kv, mem_gq, mem_gk, w_out, a_w_in, a_gq, a_gk,
           a_lam_q1, a_lam_k1, a_lam_q2, a_lam_k2, a_g_sub, b_w_in, b_gq, b_gk):
    B, T, D = x.shape
    N = mem.shape[1]
    depth = ffn1_g.shape[0]
    M = B * T
    tq = min(256, T)
    head_scale = HEAD_DIM ** -0.5 * LOG2E
    ones = lambda n: jnp.ones((n,), F32)
    bf = lambda w: w.astype(BF16)

    x2 = x.reshape(M, D)
    mem2 = mem.reshape(B * N, D)
    tab = _bias_tiles(rel_bias, tq, tq)
    ffn_w = [(bf(ffn1_w_gate), bf(ffn1_w_up), bf(0.5 * ffn1_w_down)),
             (bf(ffn2_w_gate), bf(ffn2_w_up), bf(0.5 * ffn2_w_down))]
    ffn_g = [ffn1_g, ffn2_g]
    wo, w_kv, w_a = bf(w_out), bf(mem_w_kv), bf(a_w_in)

    for i in range(depth):
        x2 = _ffn(x2, ffn_g[0][i], *ffn_w[0], i)

        kv_gain = jnp.concatenate([jnp.tile(mem_gk[i], N_MEM_HEADS), ones(MEM_WIDTH)])
        kv3 = _proj(mem2, mem_g[i], w_kv, i, [2, 0], kv_gain, BF16, MEM_WIDTH)
        kv3 = kv3.reshape(B, N, 2 * MEM_WIDTH)
        qm_gain = jnp.tile(mem_gq[i], N_MEM_HEADS) * head_scale
        j = i // 2

        if i % 2 == 0:
            n_grp = SELF_WIDTH // DIFF_QK_DIM
            gain = jnp.concatenate([jnp.tile(a_gq[j], n_grp) * (DIFF_QK_DIM ** -0.5 * LOG2E),
                                    jnp.tile(a_gk[j], n_grp), ones(SELF_WIDTH), qm_gain])
            kinds = [1] * 6 + [0] * 3 + [2]
            p3 = _proj(x2, mix_g[i], w_a, j, kinds, gain, BF16, 512).reshape(B, T, -1)
            lambda_init = 0.8 - 0.6 * math.exp(-0.3 * i)
            attn = functools.partial(_diff_attn, p3, tab, a_lam_q1[j], a_lam_k1[j], a_lam_q2[j],
                                     a_lam_k2[j], a_g_sub[j], lambda_init, tq)
            bounded = _scores_bounded(a_gq[j], a_gk[j], DIFF_QK_DIM, DIFF_QK_DIM ** -0.5, rel_bias)
            qm_block = 3 * SELF_WIDTH // MEM_WIDTH
        else:
            w = b_w_in[j]
            kvw = DSA_KV_HEADS * HEAD_DIM
            iqw = IDX_HEADS * IDX_DIM
            c_q, c_k, c_v, c_iq = 0, SELF_WIDTH, SELF_WIDTH + kvw, SELF_WIDTH + 2 * kvw
            c_ik = c_iq + iqw
            c_iw = c_ik + IDX_DIM
            c_qm = c_iw + IDX_HEADS
            w_main = jnp.concatenate([w[:, c_q:c_k], w[:, c_k:c_v], w[:, c_iq:c_ik], w[:, c_v:c_iq],
                                      w[:, c_qm:]], axis=1)
            gain = jnp.concatenate([jnp.tile(b_gq[j], N_SELF_HEADS) * head_scale,
                                    jnp.tile(b_gk[j], DSA_KV_HEADS), ones(iqw), ones(kvw), qm_gain])
            kinds = [2, 2, 2, 2, 0, 0, 0, 2]
            p3 = _proj(x2, mix_g[i], bf(w_main)[None], 0, kinds, gain, BF16, 512).reshape(B, T, -1)
            pad = LANES - IDX_DIM - IDX_HEADS
            w_idx = jnp.concatenate([w[:, c_ik:c_qm], jnp.zeros((D, pad), F32)], axis=1)
            idx_gain = jnp.concatenate([ones(IDX_DIM),
                                        jnp.full((IDX_HEADS,), (IDX_DIM * IDX_HEADS) ** -0.5, F32),
                                        jnp.zeros((pad,), F32)])
            ikw = _proj(x2, mix_g[i], bf(w_idx)[None], 0, [0], idx_gain, F32, LANES).reshape(B, T, LANES)
            ik = ikw[:, :, :IDX_DIM].astype(BF16)
            ik2 = jnp.concatenate([ik, ik], axis=-1)
            iw = ikw[:, :, IDX_DIM:IDX_DIM + IDX_HEADS]
            attn = functools.partial(_dsa_attn, p3, ik2, iw, tab, tq)
            bounded = _scores_bounded(b_gq[j], b_gk[j], HEAD_DIM, HEAD_DIM ** -0.5, rel_bias)
            qm_block = (p3.shape[-1] - MEM_WIDTH) // MEM_WIDTH

        y_self = lax.cond(bounded, lambda: attn(True), lambda: attn(False))
        y_mem = _mem_attn(p3, kv3, qm_block)
        x2 = _out_proj(y_self.reshape(M, SELF_WIDTH), y_mem.reshape(M, MEM_WIDTH), wo, i, x2)

        x2 = _ffn(x2, ffn_g[1][i], *ffn_w[1], i)

    return x2.reshape(B, T, D)
```

```python
import functools
import math
from typing import Any, Callable, NamedTuple

import jax
import jax.numpy as jnp
from jax import lax
from jax.experimental import pallas as pl
from jax.experimental.pallas import tpu as pltpu

HEAD_DIM = 128
N_SELF_HEADS = 12
N_MEM_HEADS = 4
SELF_WIDTH = N_SELF_HEADS * HEAD_DIM
MEM_WIDTH = N_MEM_HEADS * HEAD_DIM
DIFF_QK_DIM = HEAD_DIM // 2
DSA_KV_HEADS = 4
DSA_GROUP = N_SELF_HEADS // DSA_KV_HEADS
IDX_HEADS = 16
IDX_DIM = 64
TOPK_MAX = 256
NUM_BUCKETS = 32
MAX_DISTANCE = 128
EPS = 1e-6
NEG_INF = -1e30
INT32_MIN = -(2 ** 31)
LOG2E = math.log2(math.e)

FAR_TILE = 2

LANES = 128
VMEM_LIMIT = 56 * 1024 * 1024

F32 = jnp.float32
BF16 = jnp.bfloat16


def _dot(a, b):
    return jnp.dot(a, b, preferred_element_type=F32)


def _dot_nt(a, b):
    return lax.dot_general(a, b, (((1,), (1,)), ((), ())), preferred_element_type=F32)


def _params(semantics):
    return pltpu.CompilerParams(dimension_semantics=semantics, vmem_limit_bytes=VMEM_LIMIT)


def _rms_rows(x, g):
    ms = jnp.mean(x * x, axis=-1, keepdims=True)
    return x * lax.rsqrt(ms + EPS) * g


def _ffn_body(x_ref, g_ref, wg_ref, wu_ref, wd_ref, o_ref, h_ref):
    @pl.when(pl.program_id(1) == 0)
    def _():
        x = x_ref[...]
        h_ref[...] = _rms_rows(x, g_ref[...]).astype(BF16)
        o_ref[...] = x

    h = h_ref[...]
    gate = _dot(h, wg_ref[...])
    up = _dot(h, wu_ref[...])
    act = (gate * jax.nn.sigmoid(gate) * up).astype(BF16)
    o_ref[...] += _dot(act, wd_ref[...])


def _ffn(x2, g, wg, wu, wd_half, layer):
    M, D = x2.shape
    F = wg.shape[2]
    tm = min(1024, M)
    tf = min(512, F)
    return pl.pallas_call(
        _ffn_body,
        grid=(M // tm, F // tf),
        in_specs=[
            pl.BlockSpec((tm, D), lambda i, f: (i, 0)),
            pl.BlockSpec((1, D), lambda i, f: (0, 0)),
            pl.BlockSpec((None, D, tf), lambda i, f: (layer, 0, f)),
            pl.BlockSpec((None, D, tf), lambda i, f: (layer, 0, f)),
            pl.BlockSpec((None, tf, D), lambda i, f: (layer, f, 0)),
        ],
        out_specs=pl.BlockSpec((tm, D), lambda i, f: (i, 0)),
        out_shape=jax.ShapeDtypeStruct((M, D), F32),
        scratch_shapes=[pltpu.VMEM((tm, D), BF16)],
        compiler_params=_params(("parallel", "arbitrary")),
        name="ffn",
    )(x2, g.reshape(1, D), wg, wu, wd_half)


def _proj_body(types_ref, x_ref, g_ref, w_ref, gmat_ref, gain_ref, o_ref, h_ref, *, tn):
    j = pl.program_id(1)

    @pl.when(j == 0)
    def _():
        h_ref[...] = _rms_rows(x_ref[...], g_ref[...]).astype(BF16)

    y = _dot(h_ref[...], w_ref[...])
    kind = types_ref[j]

    @pl.when(kind == 0)
    def _():
        o_ref[...] = (y * gain_ref[...]).astype(o_ref.dtype)

    @pl.when(kind != 0)
    def _():
        gmat = gmat_ref[...]
        for c in range(tn // LANES):
            sl = slice(c * LANES, (c + 1) * LANES)
            yc = y[:, sl]
            ms = _dot((yc * yc).astype(BF16), gmat)
            o_ref[:, sl] = (yc * lax.rsqrt(ms + EPS) * gain_ref[:, sl]).astype(o_ref.dtype)


def _group_mats():
    r = jnp.arange(LANES)
    same64 = (r[:, None] // DIFF_QK_DIM) == (r[None, :] // DIFF_QK_DIM)
    return jnp.stack([
        jnp.zeros((LANES, LANES), F32),
        jnp.where(same64, 1.0 / DIFF_QK_DIM, 0.0),
        jnp.full((LANES, LANES), 1.0 / HEAD_DIM, F32),
    ]).astype(BF16)


def _proj(x2, g, w, layer, kinds, gain, out_dtype, tn):
    M, D = x2.shape
    N = w.shape[2]
    tm = min(1024, M)
    kinds = jnp.asarray(kinds, jnp.int32)
    grid_spec = pltpu.PrefetchScalarGridSpec(
        num_scalar_prefetch=1,
        grid=(M // tm, N // tn),
        in_specs=[
            pl.BlockSpec((tm, D), lambda i, j, t: (i, 0)),
            pl.BlockSpec((1, D), lambda i, j, t: (0, 0)),
            pl.BlockSpec((None, D, tn), lambda i, j, t: (layer, 0, j)),
            pl.BlockSpec((None, LANES, LANES), lambda i, j, t: (t[j], 0, 0)),
            pl.BlockSpec((1, tn), lambda i, j, t: (0, j)),
        ],
        out_specs=pl.BlockSpec((tm, tn), lambda i, j, t: (i, j)),
        scratch_shapes=[pltpu.VMEM((tm, D), BF16)],
    )
    return pl.pallas_call(
        functools.partial(_proj_body, tn=tn),
        grid_spec=grid_spec,
        out_shape=jax.ShapeDtypeStruct((M, N), out_dtype),
        compiler_params=_params(("parallel", "arbitrary")),
        name="proj",
    )(kinds, x2, g.reshape(1, D), w, _group_mats(), gain.reshape(1, N))


def _bias_body(rb_ref, o_ref, *, tq, tk):
    h = pl.program_id(0)
    r = lax.broadcasted_iota(jnp.int32, (tq, tk), 0)
    c = lax.broadcasted_iota(jnp.int32, (tq, tk), 1)
    max_exact = NUM_BUCKETS // 2
    far = rb_ref[NUM_BUCKETS - 1, h]
    for off in range(FAR_TILE + 1):
        dist = r - c + off * tk
        n = jnp.maximum(dist, 0)
        nf = jnp.maximum(n, 1).astype(F32)
        large = max_exact + (jnp.log(nf / max_exact) / math.log(MAX_DISTANCE / max_exact)
                             * (NUM_BUCKETS - max_exact)).astype(jnp.int32)
        large = jnp.minimum(large, NUM_BUCKETS - 1)
        bucket = jnp.where(n < max_exact, n, large)
        val = jnp.zeros((tq, tk), F32)
        for b in range(NUM_BUCKETS):
            val = jnp.where(bucket == b, rb_ref[b, h], val)
        o_ref[off] = jnp.where(dist >= 0, (val - far) * LOG2E, NEG_INF)


def _bias_tiles(rel_bias, tq, tk):
    assert tq == tk and tk >= MAX_DISTANCE
    return pl.pallas_call(
        functools.partial(_bias_body, tq=tq, tk=tk),
        grid=(N_SELF_HEADS,),
        in_specs=[pl.BlockSpec(memory_space=pltpu.SMEM)],
        out_specs=pl.BlockSpec((None, FAR_TILE + 1, tq, tk), lambda h: (h, 0, 0, 0)),
        out_shape=jax.ShapeDtypeStruct((N_SELF_HEADS, FAR_TILE + 1, tq, tk), F32),
        compiler_params=_params(("arbitrary",)),
        name="bias_tiles",
    )(rel_bias)


class _Stream(NamedTuple):
    qs_ref: Any
    k_rows: Callable
    v_rows: Callable
    bias: Callable
    scratch: tuple


TILE_GROUPS = (8, 4, 2, 1)
MAX_EXP2_ARG = 96.0


def _attention_scratch(rows, n_tiles, tk, bounded, lead=()):
    if bounded:
        return [pltpu.VMEM(lead + (rows, 2 * HEAD_DIM), F32)]
    return [
        pltpu.VMEM(lead + (n_tiles, rows, tk), F32),
        pltpu.VMEM(lead + (rows, LANES), F32),
        pltpu.VMEM(lead + (rows, LANES), F32),
        pltpu.VMEM(lead + (rows, HEAD_DIM), F32),
    ]


def _attend(streams, i, tk, bounded):
    return (_bounded_attention if bounded else _two_pass_attention)(streams, i, tk)


def _lane_fold(x, op):
    out = x[:, :LANES]
    for c in range(1, x.shape[1] // LANES):
        out = op(out, x[:, c * LANES:(c + 1) * LANES])
    return out


def _key_rows(j, tk, n=1):
    return pl.ds(pl.multiple_of(j * tk, tk), n * tk)


def _for_tile_groups(lo, hi, body):
    for u in TILE_GROUPS:
        n = (hi - lo) // u

        def trip(t, c, u=u, lo=lo):
            body(lo + t * u, u)
            return c

        lax.fori_loop(0, n, trip, 0)
        lo = lo + n * u


def _two_pass_attention(streams, i, tk):
    for st in streams:
        _, m_ref, l_ref, acc_ref = st.scratch
        m_ref[...] = jnp.full(m_ref.shape, NEG_INF, F32)
        l_ref[...] = jnp.zeros(l_ref.shape, F32)
        acc_ref[...] = jnp.zeros(acc_ref.shape, F32)

    def score(j0, u):
        for st in streams:
            s_ref, m_ref = st.scratch[:2]
            s = _dot_nt(st.qs_ref[...], st.k_rows(_key_rows(j0, tk, u)))
            m = m_ref[...]
            for t in range(u):
                s_t = st.bias(s[:, t * tk:(t + 1) * tk], j0 + t)
                s_ref[j0 + t] = s_t
                m = jnp.maximum(m, _lane_fold(s_t, jnp.maximum))
            m_ref[...] = m

    _for_tile_groups(0, i + 1, score)

    for st in streams:
        m_ref = st.scratch[1]
        m = jnp.max(m_ref[...], axis=-1, keepdims=True)
        m_ref[...] = jnp.broadcast_to(m, m_ref.shape)

    def pv(j0, u):
        for st in streams:
            s_ref, m_ref, l_ref, acc_ref = st.scratch
            m = jnp.concatenate([m_ref[...]] * (tk // LANES), axis=1)
            p = jnp.concatenate([jnp.exp2(s_ref[j0 + t] - m) for t in range(u)], axis=1)
            l_ref[...] += _lane_fold(p, jnp.add)
            acc_ref[...] += _dot(p.astype(BF16), st.v_rows(_key_rows(j0, tk, u)))

    _for_tile_groups(0, i + 1, pv)
    return [st.scratch[3][...] / jnp.sum(st.scratch[2][...], axis=-1, keepdims=True) for st in streams]


def _bounded_attention(streams, i, tk):
    for st in streams:
        acc_ref, = st.scratch
        acc_ref[...] = jnp.zeros(acc_ref.shape, F32)

    def step(j0, u):
        for st in streams:
            acc_ref, = st.scratch
            rows = _key_rows(j0, tk, u)
            s = _dot_nt(st.qs_ref[...], st.k_rows(rows))
            p = jnp.concatenate(
                [jnp.exp2(st.bias(s[:, t * tk:(t + 1) * tk], j0 + t)).astype(BF16) for t in range(u)],
                axis=1)
            v = st.v_rows(rows)
            acc_ref[...] += _dot(p, jnp.concatenate([v, jnp.ones_like(v)], axis=1))

    _for_tile_groups(0, i + 1, step)
    return [st.scratch[0][:, :HEAD_DIM] / st.scratch[0][:, HEAD_DIM:] for st in streams]


def _diff_body(lq1_ref, lk1_ref, lq2_ref, lk2_ref, q_ref, k_ref, v_ref, tab_ref, gsub_ref, o_ref,
               qs_ref, *scratch, tq, tk, hp, lambda_init, bounded):
    i = pl.program_id(2)
    lane = lax.broadcasted_iota(jnp.int32, (tq, HEAD_DIM), 1)
    streams = []
    for hh in range(hp):
        cols = slice(hh * HEAD_DIM, (hh + 1) * HEAD_DIM)
        q = q_ref[:, cols]
        zero = jnp.zeros_like(q)
        qs_ref[hh, :tq] = jnp.where(lane < DIFF_QK_DIM, q, zero)
        qs_ref[hh, tq:] = jnp.where(lane >= DIFF_QK_DIM, q, zero)

        def bias(s, j, hh=hh):
            tab = tab_ref[hh, jnp.minimum(i - j, FAR_TILE)]
            return s + jnp.concatenate([tab, tab], axis=0)

        streams.append(_Stream(
            qs_ref.at[hh],
            lambda rows, cols=cols: k_ref[rows, cols],
            lambda rows, cols=cols: v_ref[rows, cols],
            bias, tuple(r.at[hh] for r in scratch)))

    outs = _attend(streams, i, tk, bounded)

    lam = (jnp.exp(jnp.sum(lq1_ref[...] * lk1_ref[...], axis=-1, keepdims=True))
           - jnp.exp(jnp.sum(lq2_ref[...] * lk2_ref[...], axis=-1, keepdims=True)) + lambda_init)
    for hh, out in enumerate(outs):
        o = out[:tq] - lam * out[tq:]
        o_ref[:, hh * HEAD_DIM:(hh + 1) * HEAD_DIM] = (
            _rms_rows(o, gsub_ref[...]) * (1.0 - lambda_init)).astype(o_ref.dtype)


def _diff_attn(p3, tab, lq1, lk1, lq2, lk2, g_sub, lambda_init, tq, bounded):
    B, T, _ = p3.shape
    H = N_SELF_HEADS
    hp = 4 if bounded else 2
    nq = T // tq
    w = hp * HEAD_DIM
    vec = lambda a: a.reshape(1, -1).astype(F32)
    small = lambda n: pl.BlockSpec((1, n), lambda b, h, i: (0, 0))
    return pl.pallas_call(
        functools.partial(_diff_body, tq=tq, tk=tq, hp=hp, lambda_init=lambda_init, bounded=bounded),
        grid=(B, H // hp, nq),
        in_specs=[
            small(DIFF_QK_DIM), small(DIFF_QK_DIM), small(DIFF_QK_DIM), small(DIFF_QK_DIM),
            pl.BlockSpec((None, tq, w), lambda b, h, i: (b, i, h)),
            pl.BlockSpec((None, T, w), lambda b, h, i: (b, 0, H // hp + h)),
            pl.BlockSpec((None, T, w), lambda b, h, i: (b, 0, 2 * (H // hp) + h)),
            pl.BlockSpec((hp, FAR_TILE + 1, tq, tq), lambda b, h, i: (h, 0, 0, 0)),
            small(HEAD_DIM),
        ],
        out_specs=pl.BlockSpec((None, tq, w), lambda b, h, i: (b, i, h)),
        out_shape=jax.ShapeDtypeStruct((B, T, SELF_WIDTH), BF16),
        scratch_shapes=[pltpu.VMEM((hp, 2 * tq, HEAD_DIM), BF16)]
        + _attention_scratch(2 * tq, nq, tq, bounded, lead=(hp,)),
        compiler_params=_params(("parallel", "parallel", "arbitrary")),
        name="diff_attn",
    )(vec(lq1), vec(lk1), vec(lq2), vec(lk2), p3, p3, p3, tab, vec(g_sub))


SEARCH_ROWS = 128


def _sortable(x):
    bits = pltpu.bitcast(x, jnp.int32)
    return jnp.where(bits < 0, bits ^ jnp.int32(0x7FFFFFFF), bits)


def _dsa_select(iq_ref, ik_ref, iw_ref, key_ref, madd_ref, iqm_ref, thr_ref, need_ref, i,
                *, tq, tk, topk):
    n_tiles = i + 1
    lane = lax.broadcasted_iota(jnp.int32, (tq, LANES), 1)
    for hh in range(IDX_HEADS):
        pair = iq_ref[:, (hh // 2) * LANES:(hh // 2 + 1) * LANES]
        keep = (lane < IDX_DIM) if hh % 2 == 0 else (lane >= IDX_DIM)
        iqm_ref[hh] = jnp.where(keep, pair, jnp.zeros_like(pair))

    row = lax.broadcasted_iota(jnp.int32, (tq, tk), 0)
    col = lax.broadcasted_iota(jnp.int32, (tq, tk), 1)

    def causal(j):
        return (i * tq + row) >= (j * tk + col)

    def score_tile(j, c):
        ik = ik_ref[_key_rows(j, tk), :]
        sc = jnp.zeros((tq, tk), F32)
        for hh in range(IDX_HEADS):
            sc = sc + jnp.maximum(_dot_nt(iqm_ref[hh], ik), 0.0) * iw_ref[:, hh:hh + 1]
        key_ref[j] = _sortable(jnp.where(causal(j), sc, NEG_INF))
        return c

    lax.fori_loop(0, n_tiles, score_tile, 0)

    kf = float(topk)
    blocks = [slice(r0, r0 + SEARCH_ROWS) for r0 in range(0, tq, SEARCH_ROWS)]

    def count(cmp, cands):
        cands_b = [jnp.broadcast_to(c, (SEARCH_ROWS, LANES)) for c in cands]

        def body(j, parts):
            out = []
            for rows, cand_b, part in zip(blocks, cands_b, parts):
                k = key_ref[j, rows, :]
                for c in range(tk // LANES):
                    hit = cmp(k[:, c * LANES:(c + 1) * LANES], cand_b)
                    part = part + jnp.where(hit, 1.0, 0.0)
                out.append(part)
            return tuple(out)

        zeros = tuple(jnp.zeros((SEARCH_ROWS, LANES), F32) for _ in blocks)
        parts = lax.fori_loop(0, n_tiles, body, zeros)
        return [jnp.sum(p, axis=-1, keepdims=True) for p in parts]

    ge = lambda k, c: k >= c
    zero = jnp.zeros((SEARCH_ROWS, 1), jnp.int32)
    total = (n_tiles * tk).astype(F32)
    n_pos = count(ge, [zero] * len(blocks))
    state = tuple((jnp.where(n >= kf, zero, jnp.int32(INT32_MIN)), jnp.where(n >= kf, n, total))
                  for n in n_pos)

    def bit_step(b, state):
        bit = lax.shift_left(jnp.int32(1), jnp.int32(30) - b)
        cands = [base | bit for base, _ in state]
        return tuple((jnp.where(n >= kf, cand, base), jnp.where(n >= kf, n, n_base))
                     for n, cand, (base, n_base) in zip(count(ge, cands), cands, state))

    state = lax.fori_loop(0, 31, bit_step, state)
    thrs = [thr for thr, _ in state]
    for rows, thr in zip(blocks, thrs):
        thr_ref[rows, :] = jnp.broadcast_to(thr, (SEARCH_ROWS, LANES))
    reps = tk // LANES
    surplus = sum(jnp.maximum(jnp.max(n_thr - kf), 0.0) for _, n_thr in state)

    @pl.when(surplus == 0.0)
    def _():
        def mask_tile(j, c):
            thr = jnp.concatenate([thr_ref[...]] * reps, axis=1)
            madd_ref[j] = jnp.where((key_ref[j] >= thr) & causal(j), 0.0, NEG_INF)
            return c

        lax.fori_loop(0, n_tiles, mask_tile, 0)

    @pl.when(surplus != 0.0)
    def _():
        needs = [kf - n for n in count(lambda k, c: k > c, thrs)]
        for rows, need in zip(blocks, needs):
            need_ref[rows, :] = jnp.broadcast_to(need, (SEARCH_ROWS, LANES))
        upper = jnp.where(row < col, 1.0, 0.0).astype(BF16)

        def mask_tile(j, seen):
            key = key_ref[j]
            thr = jnp.concatenate([thr_ref[...]] * reps, axis=1)
            need = jnp.concatenate([need_ref[...]] * reps, axis=1)
            eq = jnp.where(key == thr, 1.0, 0.0)
            rank = seen + _dot(eq.astype(BF16), upper)
            take = (key > thr) | ((key == thr) & (rank < need))
            madd_ref[j] = jnp.where(take & causal(j), 0.0, NEG_INF)
            return seen + jnp.sum(eq, axis=-1, keepdims=True)

        lax.fori_loop(0, n_tiles, mask_tile, jnp.zeros((tq, 1), F32))


def _dsa_body(q_ref, k_ref, v_ref, iq_ref, ik_ref, iw_ref, tab_ref, o_ref,
              key_ref, madd_ref, iqm_ref, thr_ref, need_ref, qs_ref, *scratch,
              tq, tk, topk, gp, bounded):
    i = pl.program_id(1)
    R = DSA_GROUP

    @pl.when(pl.program_id(2) == 0)
    def _():
        _dsa_select(iq_ref, ik_ref, iw_ref, key_ref, madd_ref, iqm_ref, thr_ref, need_ref, i,
                    tq=tq, tk=tk, topk=topk)

    streams = []
    for g in range(gp):
        for r in range(R):
            h = g * R + r
            qs_ref[g, r * tq:(r + 1) * tq] = q_ref[:, h * HEAD_DIM:(h + 1) * HEAD_DIM]
        cols = slice(g * HEAD_DIM, (g + 1) * HEAD_DIM)

        def bias(s, j, g=g):
            back = jnp.minimum(i - j, FAR_TILE)
            return s + jnp.concatenate([madd_ref[j] + tab_ref[g * R + r, back] for r in range(R)], axis=0)

        streams.append(_Stream(
            qs_ref.at[g],
            lambda rows, cols=cols: k_ref[rows, cols],
            lambda rows, cols=cols: v_ref[rows, cols],
            bias, tuple(ref.at[g] for ref in scratch)))

    for g, out in enumerate(_attend(streams, i, tk, bounded)):
        for r in range(R):
            h = g * R + r
            o_ref[:, h * HEAD_DIM:(h + 1) * HEAD_DIM] = out[r * tq:(r + 1) * tq].astype(o_ref.dtype)


def _dsa_attn(p3, ik2, iw, tab, tq, bounded):
    B, T, _ = p3.shape
    H, R, G = N_SELF_HEADS, DSA_GROUP, DSA_KV_HEADS
    topk = min(TOPK_MAX, T // 4)
    nq = T // tq
    iq_w = IDX_HEADS * IDX_DIM
    gp = 2 if bounded else 1
    kw = gp * HEAD_DIM
    k0 = H * HEAD_DIM // kw
    iq0 = (H + G) * HEAD_DIM // iq_w
    v0 = ((H + G) * HEAD_DIM + iq_w) // kw
    assert (H + G) * HEAD_DIM % iq_w == 0
    return pl.pallas_call(
        functools.partial(_dsa_body, tq=tq, tk=tq, topk=topk, gp=gp, bounded=bounded),
        grid=(B, nq, G // gp),
        in_specs=[
            pl.BlockSpec((None, tq, gp * R * HEAD_DIM), lambda b, i, g: (b, i, g)),
            pl.BlockSpec((None, T, kw), lambda b, i, g: (b, 0, k0 + g)),
            pl.BlockSpec((None, T, kw), lambda b, i, g: (b, 0, v0 + g)),
            pl.BlockSpec((None, tq, iq_w), lambda b, i, g: (b, i, iq0)),
            pl.BlockSpec((None, T, LANES), lambda b, i, g: (b, 0, 0)),
            pl.BlockSpec((None, tq, IDX_HEADS), lambda b, i, g: (b, i, 0)),
            pl.BlockSpec((gp * R, FAR_TILE + 1, tq, tq), lambda b, i, g: (g, 0, 0, 0)),
        ],
        out_specs=pl.BlockSpec((None, tq, gp * R * HEAD_DIM), lambda b, i, g: (b, i, g)),
        out_shape=jax.ShapeDtypeStruct((B, T, SELF_WIDTH), BF16),
        scratch_shapes=[
            pltpu.VMEM((nq, tq, tq), jnp.int32),
            pltpu.VMEM((nq, tq, tq), F32),
            pltpu.VMEM((IDX_HEADS, tq, LANES), BF16),
            pltpu.VMEM((tq, LANES), jnp.int32),
            pltpu.VMEM((tq, LANES), F32),
            pltpu.VMEM((gp, R * tq, HEAD_DIM), BF16),
        ] + _attention_scratch(R * tq, nq, tq, bounded, lead=(gp,)),
        compiler_params=_params(("parallel", "arbitrary", "arbitrary")),
        name="dsa_attn",
    )(p3, p3, p3, p3, ik2, iw, tab)


def _mem_body(q_ref, k_ref, v_ref, o_ref):
    for h in range(N_MEM_HEADS):
        sl = slice(h * HEAD_DIM, (h + 1) * HEAD_DIM)
        s = _dot_nt(q_ref[:, sl], k_ref[:, sl])
        p = jnp.exp2(s - jnp.max(s, axis=-1, keepdims=True))
        o = _dot(p.astype(BF16), v_ref[:, sl]) / jnp.sum(p, axis=-1, keepdims=True)
        o_ref[:, sl] = o.astype(o_ref.dtype)


def _mem_attn(p3, kv3, qm_block):
    B, T, _ = p3.shape
    N = kv3.shape[1]
    tq = min(512, T)
    return pl.pallas_call(
        _mem_body,
        grid=(B, T // tq),
        in_specs=[
            pl.BlockSpec((None, tq, MEM_WIDTH), lambda b, i: (b, i, qm_block)),
            pl.BlockSpec((None, N, MEM_WIDTH), lambda b, i: (b, 0, 0)),
            pl.BlockSpec((None, N, MEM_WIDTH), lambda b, i: (b, 0, 1)),
        ],
        out_specs=pl.BlockSpec((None, tq, MEM_WIDTH), lambda b, i: (b, i, 0)),
        out_shape=jax.ShapeDtypeStruct((B, T, MEM_WIDTH), BF16),
        compiler_params=_params(("parallel", "arbitrary")),
        name="mem_attn",
    )(p3, kv3, kv3)


def _out_body(ys_ref, ym_ref, ws_ref, wm_ref, x_ref, o_ref):
    o_ref[...] = x_ref[...] + _dot(ys_ref[...], ws_ref[...]) + _dot(ym_ref[...], wm_ref[...])


def _out_proj(ys, ym, wo, layer, x2):
    M, D = x2.shape
    tm = min(512, M)
    return pl.pallas_call(
        _out_body,
        grid=(M // tm,),
        in_specs=[
            pl.BlockSpec((tm, SELF_WIDTH), lambda i: (i, 0)),
            pl.BlockSpec((tm, MEM_WIDTH), lambda i: (i, 0)),
            pl.BlockSpec((None, SELF_WIDTH, D), lambda i: (layer, 0, 0)),
            pl.BlockSpec((None, MEM_WIDTH, D), lambda i: (layer, SELF_WIDTH // MEM_WIDTH, 0)),
            pl.BlockSpec((tm, D), lambda i: (i, 0)),
        ],
        out_specs=pl.BlockSpec((tm, D), lambda i: (i, 0)),
        out_shape=jax.ShapeDtypeStruct((M, D), F32),
        compiler_params=_params(("parallel",)),
        name="out_proj",
    )(ys, ym, wo, wo, x2)


def _scores_bounded(gq, gk, dim, scale, rel_bias):
    qk = dim * jnp.max(jnp.abs(gq)) * jnp.max(jnp.abs(gk)) * (scale * LOG2E * 1.02)
    bias = jnp.max(jnp.abs(rel_bias - rel_bias[NUM_BUCKETS - 1:])) * LOG2E
    return qk + bias <= MAX_EXP2_ARG


def kernel(x, mem, rel_bias, ffn1_g, ffn1_w_gate, ffn1_w_up, ffn1_w_down, ffn2_g, ffn2_w_gate,
           ffn2_w_up, ffn2_w_down, mix_g, mem_g, mem_w_kv, mem_gq, mem_gk, w_out, a_w_in, a_gq, a_gk,
           a_lam_q1, a_lam_k1, a_lam_q2, a_lam_k2, a_g_sub, b_w_in, b_gq, b_gk):
    B, T, D = x.shape
    N = mem.shape[1]
    depth = ffn1_g.shape[0]
    M = B * T
    tq = min(256, T)
    head_scale = HEAD_DIM ** -0.5 * LOG2E
    ones = lambda n: jnp.ones((n,), F32)
    bf = lambda w: w.astype(BF16)

    x2 = x.reshape(M, D)
    mem2 = mem.reshape(B * N, D)
    tab = _bias_tiles(rel_bias, tq, tq)
    ffn_w = [(bf(ffn1_w_gate), bf(ffn1_w_up), bf(0.5 * ffn1_w_down)),
             (bf(ffn2_w_gate), bf(ffn2_w_up), bf(0.5 * ffn2_w_down))]
    ffn_g = [ffn1_g, ffn2_g]
    wo, w_kv, w_a = bf(w_out), bf(mem_w_kv), bf(a_w_in)

    for i in range(depth):
        x2 = _ffn(x2, ffn_g[0][i], *ffn_w[0], i)

        kv_gain = jnp.concatenate([jnp.tile(mem_gk[i], N_MEM_HEADS), ones(MEM_WIDTH)])
        kv3 = _proj(mem2, mem_g[i], w_kv, i, [2, 0], kv_gain, BF16, MEM_WIDTH)
        kv3 = kv3.reshape(B, N, 2 * MEM_WIDTH)
        qm_gain = jnp.tile(mem_gq[i], N_MEM_HEADS) * head_scale
        j = i // 2

        if i % 2 == 0:
            n_grp = SELF_WIDTH // DIFF_QK_DIM
            gain = jnp.concatenate([jnp.tile(a_gq[j], n_grp) * (DIFF_QK_DIM ** -0.5 * LOG2E),
                                    jnp.tile(a_gk[j], n_grp), ones(SELF_WIDTH), qm_gain])
            kinds = [1] * 6 + [0] * 3 + [2]
            p3 = _proj(x2, mix_g[i], w_a, j, kinds, gain, BF16, 512).reshape(B, T, -1)
            lambda_init = 0.8 - 0.6 * math.exp(-0.3 * i)
            attn = functools.partial(_diff_attn, p3, tab, a_lam_q1[j], a_lam_k1[j], a_lam_q2[j],
                                     a_lam_k2[j], a_g_sub[j], lambda_init, tq)
            bounded = _scores_bounded(a_gq[j], a_gk[j], DIFF_QK_DIM, DIFF_QK_DIM ** -0.5, rel_bias)
            qm_block = 3 * SELF_WIDTH // MEM_WIDTH
        else:
            w = b_w_in[j]
            kvw = DSA_KV_HEADS * HEAD_DIM
            iqw = IDX_HEADS * IDX_DIM
            c_q, c_k, c_v, c_iq = 0, SELF_WIDTH, SELF_WIDTH + kvw, SELF_WIDTH + 2 * kvw
            c_ik = c_iq + iqw
            c_iw = c_ik + IDX_DIM
            c_qm = c_iw + IDX_HEADS
            w_main = jnp.concatenate([w[:, c_q:c_k], w[:, c_k:c_v], w[:, c_iq:c_ik], w[:, c_v:c_iq],
                                      w[:, c_qm:]], axis=1)
            gain = jnp.concatenate([jnp.tile(b_gq[j], N_SELF_HEADS) * head_scale,
                                    jnp.tile(b_gk[j], DSA_KV_HEADS), ones(iqw), ones(kvw), qm_gain])
            kinds = [2, 2, 2, 2, 0, 0, 0, 2]
            p3 = _proj(x2, mix_g[i], bf(w_main)[None], 0, kinds, gain, BF16, 512).reshape(B, T, -1)
            pad = LANES - IDX_DIM - IDX_HEADS
            w_idx = jnp.concatenate([w[:, c_ik:c_qm], jnp.zeros((D, pad), F32)], axis=1)
            idx_gain = jnp.concatenate([ones(IDX_DIM),
                                        jnp.full((IDX_HEADS,), (IDX_DIM * IDX_HEADS) ** -0.5, F32),
                                        jnp.zeros((pad,), F32)])
            ikw = _proj(x2, mix_g[i], bf(w_idx)[None], 0, [0], idx_gain, F32, LANES).reshape(B, T, LANES)
            ik = ikw[:, :, :IDX_DIM].astype(BF16)
            ik2 = jnp.concatenate([ik, ik], axis=-1)
            iw = ikw[:, :, IDX_DIM:IDX_DIM + IDX_HEADS]
            attn = functools.partial(_dsa_attn, p3, ik2, iw, tab, tq)
            bounded = _scores_bounded(b_gq[j], b_gk[j], HEAD_DIM, HEAD_DIM ** -0.5, rel_bias)
            qm_block = (p3.shape[-1] - MEM_WIDTH) // MEM_WIDTH

        y_self = lax.cond(bounded, lambda: attn(True), lambda: attn(False))
        y_mem = _mem_attn(p3, kv3, qm_block)
        x2 = _out_proj(y_self.reshape(M, SELF_WIDTH), y_mem.reshape(M, MEM_WIDTH), wo, i, x2)

        x2 = _ffn(x2, ffn_g[1][i], *ffn_w[1], i)

    return x2.reshape(B, T, D)
```

```python
import functools
import math
from typing import Any, Callable, NamedTuple

import jax
import jax.numpy as jnp
from jax import lax
from jax.experimental import pallas as pl
from jax.experimental.pallas import tpu as pltpu

HEAD_DIM = 128
N_SELF_HEADS = 12
N_MEM_HEADS = 4
SELF_WIDTH = N_SELF_HEADS * HEAD_DIM
MEM_WIDTH = N_MEM_HEADS * HEAD_DIM
DIFF_QK_DIM = HEAD_DIM // 2
DSA_KV_HEADS = 4
DSA_GROUP = N_SELF_HEADS // DSA_KV_HEADS
IDX_HEADS = 16
IDX_DIM = 64
TOPK_MAX = 256
NUM_BUCKETS = 32
MAX_DISTANCE = 128
EPS = 1e-6
NEG_INF = -1e30
INT32_MIN = -(2 ** 31)
LOG2E = math.log2(math.e)

FAR_TILE = 2

LANES = 128
VMEM_LIMIT = 56 * 1024 * 1024

F32 = jnp.float32
BF16 = jnp.bfloat16


def _dot(a, b):
    return jnp.dot(a, b, preferred_element_type=F32)


def _dot_nt(a, b):
    return lax.dot_general(a, b, (((1,), (1,)), ((), ())), preferred_element_type=F32)


def _params(semantics):
    return pltpu.CompilerParams(dimension_semantics=semantics, vmem_limit_bytes=VMEM_LIMIT)


def _rms_rows(x, g):
    ms = jnp.mean(x * x, axis=-1, keepdims=True)
    return x * lax.rsqrt(ms + EPS) * g


def _ffn_body(x_ref, g_ref, wg_ref, wu_ref, wd_ref, o_ref, h_ref):
    @pl.when(pl.program_id(1) == 0)
    def _():
        x = x_ref[...]
        h_ref[...] = _rms_rows(x, g_ref[...]).astype(BF16)
        o_ref[...] = x

    h = h_ref[...]
    gate = _dot(h, wg_ref[...])
    up = _dot(h, wu_ref[...])
    act = (gate * jax.nn.sigmoid(gate) * up).astype(BF16)
    o_ref[...] += _dot(act, wd_ref[...])


def _ffn(x2, g, wg, wu, wd_half, layer):
    M, D = x2.shape
    F = wg.shape[2]
    tm = min(1024, M)
    tf = min(512, F)
    return pl.pallas_call(
        _ffn_body,
        grid=(M // tm, F // tf),
        in_specs=[
            pl.BlockSpec((tm, D), lambda i, f: (i, 0)),
            pl.BlockSpec((1, D), lambda i, f: (0, 0)),
            pl.BlockSpec((None, D, tf), lambda i, f: (layer, 0, f)),
            pl.BlockSpec((None, D, tf), lambda i, f: (layer, 0, f)),
            pl.BlockSpec((None, tf, D), lambda i, f: (layer, f, 0)),
        ],
        out_specs=pl.BlockSpec((tm, D), lambda i, f: (i, 0)),
        out_shape=jax.ShapeDtypeStruct((M, D), F32),
        scratch_shapes=[pltpu.VMEM((tm, D), BF16)],
        compiler_params=_params(("parallel", "arbitrary")),
        name="ffn",
    )(x2, g.reshape(1, D), wg, wu, wd_half)


def _proj_body(types_ref, x_ref, g_ref, w_ref, gmat_ref, gain_ref, o_ref, h_ref, *, tn, kw):
    j = pl.program_id(1)

    @pl.when(j == 0)
    def _():
        h_ref[...] = _rms_rows(x_ref[...], g_ref[...]).astype(BF16)

    y = _dot(h_ref[...], w_ref[...])
    for sec in range(tn // kw):
        kind = types_ref[j * (tn // kw) + sec]

        @pl.when(kind == 0)
        def _(sec=sec):
            cols = slice(sec * kw, (sec + 1) * kw)
            o_ref[:, cols] = (y[:, cols] * gain_ref[:, cols]).astype(o_ref.dtype)

        @pl.when(kind != 0)
        def _(sec=sec, kind=kind):
            gmat = gmat_ref[kind]
            for c in range(sec * kw // LANES, (sec + 1) * kw // LANES):
                sl = slice(c * LANES, (c + 1) * LANES)
                yc = y[:, sl]
                ms = _dot((yc * yc).astype(BF16), gmat)
                o_ref[:, sl] = (yc * lax.rsqrt(ms + EPS) * gain_ref[:, sl]).astype(o_ref.dtype)


def _group_mats():
    r = jnp.arange(LANES)
    same64 = (r[:, None] // DIFF_QK_DIM) == (r[None, :] // DIFF_QK_DIM)
    return jnp.stack([
        jnp.zeros((LANES, LANES), F32),
        jnp.where(same64, 1.0 / DIFF_QK_DIM, 0.0),
        jnp.full((LANES, LANES), 1.0 / HEAD_DIM, F32),
    ]).astype(BF16)


def _proj(x2, g, w, layer, kinds, gain, out_dtype, kw):
    M, D = x2.shape
    N = w.shape[2]
    tm = min(1024, M)
    tn = 2 * kw if N % (2 * kw) == 0 else kw
    kinds = jnp.asarray(kinds, jnp.int32)
    grid_spec = pltpu.PrefetchScalarGridSpec(
        num_scalar_prefetch=1,
        grid=(M // tm, N // tn),
        in_specs=[
            pl.BlockSpec((tm, D), lambda i, j, t: (i, 0)),
            pl.BlockSpec((1, D), lambda i, j, t: (0, 0)),
            pl.BlockSpec((None, D, tn), lambda i, j, t: (layer, 0, j)),
            pl.BlockSpec((3, LANES, LANES), lambda i, j, t: (0, 0, 0)),
            pl.BlockSpec((1, tn), lambda i, j, t: (0, j)),
        ],
        out_specs=pl.BlockSpec((tm, tn), lambda i, j, t: (i, j)),
        scratch_shapes=[pltpu.VMEM((tm, D), BF16)],
    )
    return pl.pallas_call(
        functools.partial(_proj_body, tn=tn, kw=kw),
        grid_spec=grid_spec,
        out_shape=jax.ShapeDtypeStruct((M, N), out_dtype),
        compiler_params=_params(("parallel", "arbitrary")),
        name="proj",
    )(kinds, x2, g.reshape(1, D), w, _group_mats(), gain.reshape(1, N))


def _bias_body(rb_ref, o_ref, *, tq, tk):
    h = pl.program_id(0)
    r = lax.broadcasted_iota(jnp.int32, (tq, tk), 0)
    c = lax.broadcasted_iota(jnp.int32, (tq, tk), 1)
    max_exact = NUM_BUCKETS // 2
    far = rb_ref[NUM_BUCKETS - 1, h]
    for off in range(FAR_TILE + 1):
        dist = r - c + off * tk
        n = jnp.maximum(dist, 0)
        nf = jnp.maximum(n, 1).astype(F32)
        large = max_exact + (jnp.log(nf / max_exact) / math.log(MAX_DISTANCE / max_exact)
                             * (NUM_BUCKETS - max_exact)).astype(jnp.int32)
        large = jnp.minimum(large, NUM_BUCKETS - 1)
        bucket = jnp.where(n < max_exact, n, large)
        val = jnp.zeros((tq, tk), F32)
        for b in range(NUM_BUCKETS):
            val = jnp.where(bucket == b, rb_ref[b, h], val)
        o_ref[off] = jnp.where(dist >= 0, (val - far) * LOG2E, NEG_INF)


def _bias_tiles(rel_bias, tq, tk):
    assert tq == tk and tk >= MAX_DISTANCE
    return pl.pallas_call(
        functools.partial(_bias_body, tq=tq, tk=tk),
        grid=(N_SELF_HEADS,),
        in_specs=[pl.BlockSpec(memory_space=pltpu.SMEM)],
        out_specs=pl.BlockSpec((None, FAR_TILE + 1, tq, tk), lambda h: (h, 0, 0, 0)),
        out_shape=jax.ShapeDtypeStruct((N_SELF_HEADS, FAR_TILE + 1, tq, tk), F32),
        compiler_params=_params(("arbitrary",)),
        name="bias_tiles",
    )(rel_bias)


class _Stream(NamedTuple):
    qs_ref: Any
    k_rows: Callable
    v_rows: Callable
    bias: Callable
    scratch: tuple


TILE_GROUPS = (8, 4, 2, 1)
MAX_EXP2_ARG = 96.0


def _attention_scratch(rows, n_tiles, tk, bounded, lead=()):
    if bounded:
        return [pltpu.VMEM(lead + (rows, 2 * HEAD_DIM), F32)]
    return [
        pltpu.VMEM(lead + (n_tiles, rows, tk), F32),
        pltpu.VMEM(lead + (rows, LANES), F32),
        pltpu.VMEM(lead + (rows, LANES), F32),
        pltpu.VMEM(lead + (rows, HEAD_DIM), F32),
    ]


def _attend(streams, i, tk, bounded):
    return (_bounded_attention if bounded else _two_pass_attention)(streams, i, tk)


def _lane_fold(x, op):
    out = x[:, :LANES]
    for c in range(1, x.shape[1] // LANES):
        out = op(out, x[:, c * LANES:(c + 1) * LANES])
    return out


def _key_rows(j, tk, n=1):
    return pl.ds(pl.multiple_of(j * tk, tk), n * tk)


def _fold_tile_groups(lo, hi, body, carry, groups=TILE_GROUPS):
    for u in groups:
        n = (hi - lo) // u

        def trip(t, c, u=u, lo=lo):
            return body(lo + t * u, u, c)

        carry = lax.fori_loop(0, n, trip, carry)
        lo = lo + n * u
    return carry


def _for_tile_groups(lo, hi, body):
    _fold_tile_groups(lo, hi, lambda j0, u, c: (body(j0, u), c)[1], 0)


def _two_pass_attention(streams, i, tk):
    for st in streams:
        _, m_ref, l_ref, acc_ref = st.scratch
        m_ref[...] = jnp.full(m_ref.shape, NEG_INF, F32)
        l_ref[...] = jnp.zeros(l_ref.shape, F32)
        acc_ref[...] = jnp.zeros(acc_ref.shape, F32)

    def score(j0, u):
        for st in streams:
            s_ref, m_ref = st.scratch[:2]
            s = _dot_nt(st.qs_ref[...], st.k_rows(_key_rows(j0, tk, u)))
            m = m_ref[...]
            for t in range(u):
                s_t = st.bias(s[:, t * tk:(t + 1) * tk], j0 + t)
                s_ref[j0 + t] = s_t
                m = jnp.maximum(m, _lane_fold(s_t, jnp.maximum))
            m_ref[...] = m

    _for_tile_groups(0, i + 1, score)

    for st in streams:
        m_ref = st.scratch[1]
        m = jnp.max(m_ref[...], axis=-1, keepdims=True)
        m_ref[...] = jnp.broadcast_to(m, m_ref.shape)

    def pv(j0, u):
        for st in streams:
            s_ref, m_ref, l_ref, acc_ref = st.scratch
            m = jnp.concatenate([m_ref[...]] * (tk // LANES), axis=1)
            p = jnp.concatenate([jnp.exp2(s_ref[j0 + t] - m) for t in range(u)], axis=1)
            l_ref[...] += _lane_fold(p, jnp.add)
            acc_ref[...] += _dot(p.astype(BF16), st.v_rows(_key_rows(j0, tk, u)))

    _for_tile_groups(0, i + 1, pv)
    return [st.scratch[3][...] / jnp.sum(st.scratch[2][...], axis=-1, keepdims=True) for st in streams]


def _bounded_attention(streams, i, tk):
    for st in streams:
        acc_ref, = st.scratch
        acc_ref[...] = jnp.zeros(acc_ref.shape, F32)

    def step(j0, u):
        for st in streams:
            acc_ref, = st.scratch
            rows = _key_rows(j0, tk, u)
            s = _dot_nt(st.qs_ref[...], st.k_rows(rows))
            p = jnp.concatenate(
                [jnp.exp2(st.bias(s[:, t * tk:(t + 1) * tk], j0 + t)).astype(BF16) for t in range(u)],
                axis=1)
            v = st.v_rows(rows)
            acc_ref[...] += _dot(p, jnp.concatenate([v, jnp.ones_like(v)], axis=1))

    _for_tile_groups(0, i + 1, step)
    return [st.scratch[0][:, :HEAD_DIM] / st.scratch[0][:, HEAD_DIM:] for st in streams]


def _diff_body(lq1_ref, lk1_ref, lq2_ref, lk2_ref, q_ref, k_ref, v_ref, tab_ref, gsub_ref, o_ref,
               qs_ref, *scratch, tq, tk, hp, lambda_init, bounded):
    i = pl.program_id(2)
    lane = lax.broadcasted_iota(jnp.int32, (tq, HEAD_DIM), 1)
    streams = []
    for hh in range(hp):
        cols = slice(hh * HEAD_DIM, (hh + 1) * HEAD_DIM)
        q = q_ref[:, cols]
        zero = jnp.zeros_like(q)
        qs_ref[hh, :tq] = jnp.where(lane < DIFF_QK_DIM, q, zero)
        qs_ref[hh, tq:] = jnp.where(lane >= DIFF_QK_DIM, q, zero)

        def bias(s, j, hh=hh):
            tab = tab_ref[hh, jnp.minimum(i - j, FAR_TILE)]
            return s + jnp.concatenate([tab, tab], axis=0)

        streams.append(_Stream(
            qs_ref.at[hh],
            lambda rows, cols=cols: k_ref[rows, cols],
            lambda rows, cols=cols: v_ref[rows, cols],
            bias, tuple(r.at[hh] for r in scratch)))

    outs = _attend(streams, i, tk, bounded)

    lam = (jnp.exp(jnp.sum(lq1_ref[...] * lk1_ref[...], axis=-1, keepdims=True))
           - jnp.exp(jnp.sum(lq2_ref[...] * lk2_ref[...], axis=-1, keepdims=True)) + lambda_init)
    for hh, out in enumerate(outs):
        o = out[:tq] - lam * out[tq:]
        o_ref[:, hh * HEAD_DIM:(hh + 1) * HEAD_DIM] = (
            _rms_rows(o, gsub_ref[...]) * (1.0 - lambda_init)).astype(o_ref.dtype)


def _diff_attn(p3, tab, lq1, lk1, lq2, lk2, g_sub, lambda_init, tq, bounded):
    B, T, _ = p3.shape
    H = N_SELF_HEADS
    hp = 4 if bounded else 2
    nq = T // tq
    w = hp * HEAD_DIM
    vec = lambda a: a.reshape(1, -1).astype(F32)
    small = lambda n: pl.BlockSpec((1, n), lambda b, h, i: (0, 0))
    return pl.pallas_call(
        functools.partial(_diff_body, tq=tq, tk=tq, hp=hp, lambda_init=lambda_init, bounded=bounded),
        grid=(B, H // hp, nq),
        in_specs=[
            small(DIFF_QK_DIM), small(DIFF_QK_DIM), small(DIFF_QK_DIM), small(DIFF_QK_DIM),
            pl.BlockSpec((None, tq, w), lambda b, h, i: (b, i, h)),
            pl.BlockSpec((None, T, w), lambda b, h, i: (b, 0, H // hp + h)),
            pl.BlockSpec((None, T, w), lambda b, h, i: (b, 0, 2 * (H // hp) + h)),
            pl.BlockSpec((hp, FAR_TILE + 1, tq, tq), lambda b, h, i: (h, 0, 0, 0)),
            small(HEAD_DIM),
        ],
        out_specs=pl.BlockSpec((None, tq, w), lambda b, h, i: (b, i, h)),
        out_shape=jax.ShapeDtypeStruct((B, T, SELF_WIDTH), BF16),
        scratch_shapes=[pltpu.VMEM((hp, 2 * tq, HEAD_DIM), BF16)]
        + _attention_scratch(2 * tq, nq, tq, bounded, lead=(hp,)),
        compiler_params=_params(("parallel", "parallel", "arbitrary")),
        name="diff_attn",
    )(vec(lq1), vec(lk1), vec(lq2), vec(lk2), p3, p3, p3, tab, vec(g_sub))


SEARCH_ROWS = 128


def _sortable(x):
    bits = pltpu.bitcast(x, jnp.int32)
    return jnp.where(bits < 0, bits ^ jnp.int32(0x7FFFFFFF), bits)


def _dsa_select(iq_ref, ik_ref, iw_ref, key_ref, madd_ref, iqm_ref, thr_ref, need_ref, i,
                *, tq, tk, topk):
    n_tiles = i + 1
    lane = lax.broadcasted_iota(jnp.int32, (tq, LANES), 1)
    for hh in range(IDX_HEADS):
        pair = iq_ref[:, (hh // 2) * LANES:(hh // 2 + 1) * LANES]
        keep = (lane < IDX_DIM) if hh % 2 == 0 else (lane >= IDX_DIM)
        iqm_ref[hh] = jnp.where(keep, pair, jnp.zeros_like(pair))

    row = lax.broadcasted_iota(jnp.int32, (tq, tk), 0)
    col = lax.broadcasted_iota(jnp.int32, (tq, tk), 1)

    def causal(j):
        return (i * tq + row) >= (j * tk + col)

    def score_tiles(j0, u, c):
        for j in [j0 + t for t in range(u)]:
            ik = ik_ref[_key_rows(j, tk), :]
            sc = jnp.zeros((tq, tk), F32)
            for hh in range(IDX_HEADS):
                sc = sc + jnp.maximum(_dot_nt(iqm_ref[hh], ik), 0.0) * iw_ref[:, hh:hh + 1]
            key_ref[j] = _sortable(jnp.where(causal(j), sc, NEG_INF))
        return c

    _fold_tile_groups(0, n_tiles, score_tiles, 0, groups=(2, 1))

    kf = float(topk)
    blocks = [slice(r0, r0 + SEARCH_ROWS) for r0 in range(0, tq, SEARCH_ROWS)]

    def count(cmp, cands):
        cands_b = [jnp.broadcast_to(c, (SEARCH_ROWS, LANES)) for c in cands]

        def body(j0, u, parts):
            out = []
            for rows, cand_b, part in zip(blocks, cands_b, parts):
                for t in range(u):
                    k = key_ref[j0 + t, rows, :]
                    for c in range(tk // LANES):
                        hit = cmp(k[:, c * LANES:(c + 1) * LANES], cand_b)
                        part = part + jnp.where(hit, 1.0, 0.0)
                out.append(part)
            return tuple(out)

        zeros = tuple(jnp.zeros((SEARCH_ROWS, LANES), F32) for _ in blocks)
        parts = _fold_tile_groups(0, n_tiles, body, zeros, groups=(4, 2, 1))
        return [jnp.sum(p, axis=-1, keepdims=True) for p in parts]

    ge = lambda k, c: k >= c
    zero = jnp.zeros((SEARCH_ROWS, 1), jnp.int32)
    total = (n_tiles * tk).astype(F32)
    n_pos = count(ge, [zero] * len(blocks))
    state = tuple((jnp.where(n >= kf, zero, jnp.int32(INT32_MIN)), jnp.where(n >= kf, n, total))
                  for n in n_pos)

    def bit_step(b, state):
        bit = lax.shift_left(jnp.int32(1), jnp.int32(30) - b)
        cands = [base | bit for base, _ in state]
        return tuple((jnp.where(n >= kf, cand, base), jnp.where(n >= kf, n, n_base))
                     for n, cand, (base, n_base) in zip(count(ge, cands), cands, state))

    state = lax.fori_loop(0, 31, bit_step, state)
    thrs = [thr for thr, _ in state]
    for rows, thr in zip(blocks, thrs):
        thr_ref[rows, :] = jnp.broadcast_to(thr, (SEARCH_ROWS, LANES))
    reps = tk // LANES
    surplus = sum(jnp.maximum(jnp.max(n_thr - kf), 0.0) for _, n_thr in state)

    @pl.when(surplus == 0.0)
    def _():
        def mask_tile(j, c):
            thr = jnp.concatenate([thr_ref[...]] * reps, axis=1)
            madd_ref[j] = jnp.where((key_ref[j] >= thr) & causal(j), 0.0, NEG_INF)
            return c

        lax.fori_loop(0, n_tiles, mask_tile, 0)

    @pl.when(surplus != 0.0)
    def _():
        needs = [kf - n for n in count(lambda k, c: k > c, thrs)]
        for rows, need in zip(blocks, needs):
            need_ref[rows, :] = jnp.broadcast_to(need, (SEARCH_ROWS, LANES))
        upper = jnp.where(row < col, 1.0, 0.0).astype(BF16)

        def mask_tile(j, seen):
            key = key_ref[j]
            thr = jnp.concatenate([thr_ref[...]] * reps, axis=1)
            need = jnp.concatenate([need_ref[...]] * reps, axis=1)
            eq = jnp.where(key == thr, 1.0, 0.0)
            rank = seen + _dot(eq.astype(BF16), upper)
            take = (key > thr) | ((key == thr) & (rank < need))
            madd_ref[j] = jnp.where(take & causal(j), 0.0, NEG_INF)
            return seen + jnp.sum(eq, axis=-1, keepdims=True)

        lax.fori_loop(0, n_tiles, mask_tile, jnp.zeros((tq, 1), F32))


def _dsa_body(q_ref, k_ref, v_ref, iq_ref, ik_ref, iw_ref, tab_ref, o_ref,
              key_ref, madd_ref, iqm_ref, thr_ref, need_ref, qs_ref, *scratch,
              tq, tk, topk, gp, bounded):
    i = pl.program_id(1)
    R = DSA_GROUP

    @pl.when(pl.program_id(2) == 0)
    def _():
        _dsa_select(iq_ref, ik_ref, iw_ref, key_ref, madd_ref, iqm_ref, thr_ref, need_ref, i,
                    tq=tq, tk=tk, topk=topk)

    streams = []
    for g in range(gp):
        for r in range(R):
            h = g * R + r
            qs_ref[g, r * tq:(r + 1) * tq] = q_ref[:, h * HEAD_DIM:(h + 1) * HEAD_DIM]
        cols = slice(g * HEAD_DIM, (g + 1) * HEAD_DIM)

        def bias(s, j, g=g):
            back = jnp.minimum(i - j, FAR_TILE)
            return s + jnp.concatenate([madd_ref[j] + tab_ref[g * R + r, back] for r in range(R)], axis=0)

        streams.append(_Stream(
            qs_ref.at[g],
            lambda rows, cols=cols: k_ref[rows, cols],
            lambda rows, cols=cols: v_ref[rows, cols],
            bias, tuple(ref.at[g] for ref in scratch)))

    for g, out in enumerate(_attend(streams, i, tk, bounded)):
        for r in range(R):
            h = g * R + r
            o_ref[:, h * HEAD_DIM:(h + 1) * HEAD_DIM] = out[r * tq:(r + 1) * tq].astype(o_ref.dtype)


def _dsa_attn(p3, ik2, iw, tab, tq, bounded):
    B, T, _ = p3.shape
    H, R, G = N_SELF_HEADS, DSA_GROUP, DSA_KV_HEADS
    topk = min(TOPK_MAX, T // 4)
    nq = T // tq
    iq_w = IDX_HEADS * IDX_DIM
    gp = 2 if bounded else 1
    kw = gp * HEAD_DIM
    k0 = H * HEAD_DIM // kw
    iq0 = (H + G) * HEAD_DIM // iq_w
    v0 = ((H + G) * HEAD_DIM + iq_w) // kw
    assert (H + G) * HEAD_DIM % iq_w == 0
    return pl.pallas_call(
        functools.partial(_dsa_body, tq=tq, tk=tq, topk=topk, gp=gp, bounded=bounded),
        grid=(B, nq, G // gp),
        in_specs=[
            pl.BlockSpec((None, tq, gp * R * HEAD_DIM), lambda b, i, g: (b, i, g)),
            pl.BlockSpec((None, T, kw), lambda b, i, g: (b, 0, k0 + g)),
            pl.BlockSpec((None, T, kw), lambda b, i, g: (b, 0, v0 + g)),
            pl.BlockSpec((None, tq, iq_w), lambda b, i, g: (b, i, iq0)),
            pl.BlockSpec((None, T, LANES), lambda b, i, g: (b, 0, 0)),
            pl.BlockSpec((None, tq, IDX_HEADS), lambda b, i, g: (b, i, 0)),
            pl.BlockSpec((gp * R, FAR_TILE + 1, tq, tq), lambda b, i, g: (g, 0, 0, 0)),
        ],
        out_specs=pl.BlockSpec((None, tq, gp * R * HEAD_DIM), lambda b, i, g: (b, i, g)),
        out_shape=jax.ShapeDtypeStruct((B, T, SELF_WIDTH), BF16),
        scratch_shapes=[
            pltpu.VMEM((nq, tq, tq), jnp.int32),
            pltpu.VMEM((nq, tq, tq), F32),
            pltpu.VMEM((IDX_HEADS, tq, LANES), BF16),
            pltpu.VMEM((tq, LANES), jnp.int32),
            pltpu.VMEM((tq, LANES), F32),
            pltpu.VMEM((gp, R * tq, HEAD_DIM), BF16),
        ] + _attention_scratch(R * tq, nq, tq, bounded, lead=(gp,)),
        compiler_params=_params(("parallel", "arbitrary", "arbitrary")),
        name="dsa_attn",
    )(p3, p3, p3, p3, ik2, iw, tab)


def _mem_body(q_ref, k_ref, v_ref, o_ref):
    for h in range(N_MEM_HEADS):
        sl = slice(h * HEAD_DIM, (h + 1) * HEAD_DIM)
        s = _dot_nt(q_ref[:, sl], k_ref[:, sl])
        p = jnp.exp2(s - jnp.max(s, axis=-1, keepdims=True))
        o = _dot(p.astype(BF16), v_ref[:, sl]) / jnp.sum(p, axis=-1, keepdims=True)
        o_ref[:, sl] = o.astype(o_ref.dtype)


def _mem_attn(p3, kv3, qm_block):
    B, T, _ = p3.shape
    N = kv3.shape[1]
    tq = min(512, T)
    return pl.pallas_call(
        _mem_body,
        grid=(B, T // tq),
        in_specs=[
            pl.BlockSpec((None, tq, MEM_WIDTH), lambda b, i: (b, i, qm_block)),
            pl.BlockSpec((None, N, MEM_WIDTH), lambda b, i: (b, 0, 0)),
            pl.BlockSpec((None, N, MEM_WIDTH), lambda b, i: (b, 0, 1)),
        ],
        out_specs=pl.BlockSpec((None, tq, MEM_WIDTH), lambda b, i: (b, i, 0)),
        out_shape=jax.ShapeDtypeStruct((B, T, MEM_WIDTH), BF16),
        compiler_params=_params(("parallel", "arbitrary")),
        name="mem_attn",
    )(p3, kv3, kv3)


def _out_body(ys_ref, ym_ref, ws_ref, wm_ref, x_ref, o_ref):
    o_ref[...] = x_ref[...] + _dot(ys_ref[...], ws_ref[...]) + _dot(ym_ref[...], wm_ref[...])


def _out_proj(ys, ym, wo, layer, x2):
    M, D = x2.shape
    tm = min(512, M)
    return pl.pallas_call(
        _out_body,
        grid=(M // tm,),
        in_specs=[
            pl.BlockSpec((tm, SELF_WIDTH), lambda i: (i, 0)),
            pl.BlockSpec((tm, MEM_WIDTH), lambda i: (i, 0)),
            pl.BlockSpec((None, SELF_WIDTH, D), lambda i: (layer, 0, 0)),
            pl.BlockSpec((None, MEM_WIDTH, D), lambda i: (layer, SELF_WIDTH // MEM_WIDTH, 0)),
            pl.BlockSpec((tm, D), lambda i: (i, 0)),
        ],
        out_specs=pl.BlockSpec((tm, D), lambda i: (i, 0)),
        out_shape=jax.ShapeDtypeStruct((M, D), F32),
        compiler_params=_params(("parallel",)),
        name="out_proj",
    )(ys, ym, wo, wo, x2)


def _scores_bounded(gq, gk, dim, scale, rel_bias):
    qk = dim * jnp.max(jnp.abs(gq)) * jnp.max(jnp.abs(gk)) * (scale * LOG2E * 1.02)
    bias = jnp.max(jnp.abs(rel_bias - rel_bias[NUM_BUCKETS - 1:])) * LOG2E
    return qk + bias <= MAX_EXP2_ARG


def kernel(x, mem, rel_bias, ffn1_g, ffn1_w_gate, ffn1_w_up, ffn1_w_down, ffn2_g, ffn2_w_gate,
           ffn2_w_up, ffn2_w_down, mix_g, mem_g, mem_w_kv, mem_gq, mem_gk, w_out, a_w_in, a_gq, a_gk,
           a_lam_q1, a_lam_k1, a_lam_q2, a_lam_k2, a_g_sub, b_w_in, b_gq, b_gk):
    B, T, D = x.shape
    N = mem.shape[1]
    depth = ffn1_g.shape[0]
    M = B * T
    tq = min(256, T)
    head_scale = HEAD_DIM ** -0.5 * LOG2E
    ones = lambda n: jnp.ones((n,), F32)
    bf = lambda w: w.astype(BF16)

    x2 = x.reshape(M, D)
    mem2 = mem.reshape(B * N, D)
    tab = _bias_tiles(rel_bias, tq, tq)
    ffn_w = [(bf(ffn1_w_gate), bf(ffn1_w_up), bf(0.5 * ffn1_w_down)),
             (bf(ffn2_w_gate), bf(ffn2_w_up), bf(0.5 * ffn2_w_down))]
    ffn_g = [ffn1_g, ffn2_g]
    wo, w_kv, w_a = bf(w_out), bf(mem_w_kv), bf(a_w_in)

    for i in range(depth):
        x2 = _ffn(x2, ffn_g[0][i], *ffn_w[0], i)

        kv_gain = jnp.concatenate([jnp.tile(mem_gk[i], N_MEM_HEADS), ones(MEM_WIDTH)])
        kv3 = _proj(mem2, mem_g[i], w_kv, i, [2, 0], kv_gain, BF16, MEM_WIDTH)
        kv3 = kv3.reshape(B, N, 2 * MEM_WIDTH)
        qm_gain = jnp.tile(mem_gq[i], N_MEM_HEADS) * head_scale
        j = i // 2

        if i % 2 == 0:
            n_grp = SELF_WIDTH // DIFF_QK_DIM
            gain = jnp.concatenate([jnp.tile(a_gq[j], n_grp) * (DIFF_QK_DIM ** -0.5 * LOG2E),
                                    jnp.tile(a_gk[j], n_grp), ones(SELF_WIDTH), qm_gain])
            kinds = [1] * 6 + [0] * 3 + [2]
            p3 = _proj(x2, mix_g[i], w_a, j, kinds, gain, BF16, 512).reshape(B, T, -1)
            lambda_init = 0.8 - 0.6 * math.exp(-0.3 * i)
            attn = functools.partial(_diff_attn, p3, tab, a_lam_q1[j], a_lam_k1[j], a_lam_q2[j],
                                     a_lam_k2[j], a_g_sub[j], lambda_init, tq)
            bounded = _scores_bounded(a_gq[j], a_gk[j], DIFF_QK_DIM, DIFF_QK_DIM ** -0.5, rel_bias)
            qm_block = 3 * SELF_WIDTH // MEM_WIDTH
        else:
            w = b_w_in[j]
            kvw = DSA_KV_HEADS * HEAD_DIM
            iqw = IDX_HEADS * IDX_DIM
            c_q, c_k, c_v, c_iq = 0, SELF_WIDTH, SELF_WIDTH + kvw, SELF_WIDTH + 2 * kvw
            c_ik = c_iq + iqw
            c_iw = c_ik + IDX_DIM
            c_qm = c_iw + IDX_HEADS
            w_main = jnp.concatenate([w[:, c_q:c_k], w[:, c_k:c_v], w[:, c_iq:c_ik], w[:, c_v:c_iq],
                                      w[:, c_qm:]], axis=1)
            gain = jnp.concatenate([jnp.tile(b_gq[j], N_SELF_HEADS) * head_scale,
                                    jnp.tile(b_gk[j], DSA_KV_HEADS), ones(iqw), ones(kvw), qm_gain])
            kinds = [2, 2, 2, 2, 0, 0, 0, 2]
            p3 = _proj(x2, mix_g[i], bf(w_main)[None], 0, kinds, gain, BF16, 512).reshape(B, T, -1)
            pad = LANES - IDX_DIM - IDX_HEADS
            w_idx = jnp.concatenate([w[:, c_ik:c_qm], jnp.zeros((D, pad), F32)], axis=1)
            idx_gain = jnp.concatenate([ones(IDX_DIM),
                                        jnp.full((IDX_HEADS,), (IDX_DIM * IDX_HEADS) ** -0.5, F32),
                                        jnp.zeros((pad,), F32)])
            ikw = _proj(x2, mix_g[i], bf(w_idx)[None], 0, [0], idx_gain, F32, LANES).reshape(B, T, LANES)
            ik = ikw[:, :, :IDX_DIM].astype(BF16)
            ik2 = jnp.concatenate([ik, ik], axis=-1)
            iw = ikw[:, :, IDX_DIM:IDX_DIM + IDX_HEADS]
            attn = functools.partial(_dsa_attn, p3, ik2, iw, tab, tq)
            bounded = _scores_bounded(b_gq[j], b_gk[j], HEAD_DIM, HEAD_DIM ** -0.5, rel_bias)
            qm_block = (p3.shape[-1] - MEM_WIDTH) // MEM_WIDTH

        y_self = lax.cond(bounded, lambda: attn(True), lambda: attn(False))
        y_mem = _mem_attn(p3, kv3, qm_block)
        x2 = _out_proj(y_self.reshape(M, SELF_WIDTH), y_mem.reshape(M, MEM_WIDTH), wo, i, x2)

        x2 = _ffn(x2, ffn_g[1][i], *ffn_w[1], i)

    return x2.reshape(B, T, D)
```

```python
import functools
import math
from typing import Any, Callable, NamedTuple

import jax
import jax.numpy as jnp
from jax import lax
from jax.experimental import pallas as pl
from jax.experimental.pallas import tpu as pltpu

HEAD_DIM = 128
N_SELF_HEADS = 12
N_MEM_HEADS = 4
SELF_WIDTH = N_SELF_HEADS * HEAD_DIM
MEM_WIDTH = N_MEM_HEADS * HEAD_DIM
DIFF_QK_DIM = HEAD_DIM // 2
DSA_KV_HEADS = 4
DSA_GROUP = N_SELF_HEADS // DSA_KV_HEADS
IDX_HEADS = 16
IDX_DIM = 64
TOPK_MAX = 256
NUM_BUCKETS = 32
MAX_DISTANCE = 128
EPS = 1e-6
NEG_INF = -1e30
INT32_MIN = -(2 ** 31)
LOG2E = math.log2(math.e)

FAR_TILE = 2

LANES = 128
VMEM_LIMIT = 56 * 1024 * 1024

F32 = jnp.float32
BF16 = jnp.bfloat16


def _dot(a, b):
    return jnp.dot(a, b, preferred_element_type=F32)


def _dot_nt(a, b):
    return lax.dot_general(a, b, (((1,), (1,)), ((), ())), preferred_element_type=F32)


def _params(semantics):
    return pltpu.CompilerParams(dimension_semantics=semantics, vmem_limit_bytes=VMEM_LIMIT)


def _rms_rows(x, g):
    ms = jnp.mean(x * x, axis=-1, keepdims=True)
    return x * lax.rsqrt(ms + EPS) * g


def _ffn_body(x_ref, g_ref, wg_ref, wu_ref, wd_ref, o_ref, h_ref):
    @pl.when(pl.program_id(1) == 0)
    def _():
        x = x_ref[...]
        h_ref[...] = _rms_rows(x, g_ref[...]).astype(BF16)
        o_ref[...] = x

    h = h_ref[...]
    gate = _dot(h, wg_ref[...])
    up = _dot(h, wu_ref[...])
    act = (gate * jax.nn.sigmoid(gate) * up).astype(BF16)
    o_ref[...] += _dot(act, wd_ref[...])


def _ffn(x2, g, wg, wu, wd_half, layer):
    M, D = x2.shape
    F = wg.shape[2]
    tm = min(1024, M)
    tf = min(512, F)
    return pl.pallas_call(
        _ffn_body,
        grid=(M // tm, F // tf),
        in_specs=[
            pl.BlockSpec((tm, D), lambda i, f: (i, 0)),
            pl.BlockSpec((1, D), lambda i, f: (0, 0)),
            pl.BlockSpec((None, D, tf), lambda i, f: (layer, 0, f)),
            pl.BlockSpec((None, D, tf), lambda i, f: (layer, 0, f)),
            pl.BlockSpec((None, tf, D), lambda i, f: (layer, f, 0)),
        ],
        out_specs=pl.BlockSpec((tm, D), lambda i, f: (i, 0)),
        out_shape=jax.ShapeDtypeStruct((M, D), F32),
        scratch_shapes=[pltpu.VMEM((tm, D), BF16)],
        compiler_params=_params(("parallel", "arbitrary")),
        name="ffn",
    )(x2, g.reshape(1, D), wg, wu, wd_half)


def _proj_body(types_ref, x_ref, g_ref, w_ref, gmat_ref, gain_ref, o_ref, h_ref, *, tn, kw):
    j = pl.program_id(1)

    @pl.when(j == 0)
    def _():
        h_ref[...] = _rms_rows(x_ref[...], g_ref[...]).astype(BF16)

    y = _dot(h_ref[...], w_ref[...])
    for sec in range(tn // kw):
        kind = types_ref[j * (tn // kw) + sec]

        @pl.when(kind == 0)
        def _(sec=sec):
            cols = slice(sec * kw, (sec + 1) * kw)
            o_ref[:, cols] = (y[:, cols] * gain_ref[:, cols]).astype(o_ref.dtype)

        @pl.when(kind != 0)
        def _(sec=sec, kind=kind):
            gmat = gmat_ref[kind]
            for c in range(sec * kw // LANES, (sec + 1) * kw // LANES):
                sl = slice(c * LANES, (c + 1) * LANES)
                yc = y[:, sl]
                ms = _dot((yc * yc).astype(BF16), gmat)
                o_ref[:, sl] = (yc * lax.rsqrt(ms + EPS) * gain_ref[:, sl]).astype(o_ref.dtype)


def _group_mats():
    r = jnp.arange(LANES)
    same64 = (r[:, None] // DIFF_QK_DIM) == (r[None, :] // DIFF_QK_DIM)
    return jnp.stack([
        jnp.zeros((LANES, LANES), F32),
        jnp.where(same64, 1.0 / DIFF_QK_DIM, 0.0),
        jnp.full((LANES, LANES), 1.0 / HEAD_DIM, F32),
    ]).astype(BF16)


def _proj(x2, g, w, layer, kinds, gain, out_dtype, kw):
    M, D = x2.shape
    N = w.shape[2]
    tm = min(1024, M)
    tn = 2 * kw if N % (2 * kw) == 0 else kw
    kinds = jnp.asarray(kinds, jnp.int32)
    grid_spec = pltpu.PrefetchScalarGridSpec(
        num_scalar_prefetch=1,
        grid=(M // tm, N // tn),
        in_specs=[
            pl.BlockSpec((tm, D), lambda i, j, t: (i, 0)),
            pl.BlockSpec((1, D), lambda i, j, t: (0, 0)),
            pl.BlockSpec((None, D, tn), lambda i, j, t: (layer, 0, j)),
            pl.BlockSpec((3, LANES, LANES), lambda i, j, t: (0, 0, 0)),
            pl.BlockSpec((1, tn), lambda i, j, t: (0, j)),
        ],
        out_specs=pl.BlockSpec((tm, tn), lambda i, j, t: (i, j)),
        scratch_shapes=[pltpu.VMEM((tm, D), BF16)],
    )
    return pl.pallas_call(
        functools.partial(_proj_body, tn=tn, kw=kw),
        grid_spec=grid_spec,
        out_shape=jax.ShapeDtypeStruct((M, N), out_dtype),
        compiler_params=_params(("parallel", "arbitrary")),
        name="proj",
    )(kinds, x2, g.reshape(1, D), w, _group_mats(), gain.reshape(1, N))


def _bias_body(rb_ref, o_ref, *, tq, tk):
    h = pl.program_id(0)
    r = lax.broadcasted_iota(jnp.int32, (tq, tk), 0)
    c = lax.broadcasted_iota(jnp.int32, (tq, tk), 1)
    max_exact = NUM_BUCKETS // 2
    far = rb_ref[NUM_BUCKETS - 1, h]
    for off in range(FAR_TILE + 1):
        dist = r - c + off * tk
        n = jnp.maximum(dist, 0)
        nf = jnp.maximum(n, 1).astype(F32)
        large = max_exact + (jnp.log(nf / max_exact) / math.log(MAX_DISTANCE / max_exact)
                             * (NUM_BUCKETS - max_exact)).astype(jnp.int32)
        large = jnp.minimum(large, NUM_BUCKETS - 1)
        bucket = jnp.where(n < max_exact, n, large)
        val = jnp.zeros((tq, tk), F32)
        for b in range(NUM_BUCKETS):
            val = jnp.where(bucket == b, rb_ref[b, h], val)
        o_ref[off] = jnp.where(dist >= 0, (val - far) * LOG2E, NEG_INF)


def _bias_tiles(rel_bias, tq, tk):
    assert tq == tk and tk >= MAX_DISTANCE
    return pl.pallas_call(
        functools.partial(_bias_body, tq=tq, tk=tk),
        grid=(N_SELF_HEADS,),
        in_specs=[pl.BlockSpec(memory_space=pltpu.SMEM)],
        out_specs=pl.BlockSpec((None, FAR_TILE + 1, tq, tk), lambda h: (h, 0, 0, 0)),
        out_shape=jax.ShapeDtypeStruct((N_SELF_HEADS, FAR_TILE + 1, tq, tk), F32),
        compiler_params=_params(("arbitrary",)),
        name="bias_tiles",
    )(rel_bias)


class _Stream(NamedTuple):
    qs_ref: Any
    k_rows: Callable
    v_rows: Callable
    bias: Callable
    scratch: tuple


TILE_GROUPS = (8, 4, 2, 1)
MAX_EXP2_ARG = 96.0


def _attention_scratch(rows, n_tiles, tk, bounded, lead=()):
    if bounded:
        return [pltpu.VMEM(lead + (rows, 2 * HEAD_DIM), F32)]
    return [
        pltpu.VMEM(lead + (n_tiles, rows, tk), F32),
        pltpu.VMEM(lead + (rows, LANES), F32),
        pltpu.VMEM(lead + (rows, LANES), F32),
        pltpu.VMEM(lead + (rows, HEAD_DIM), F32),
    ]


def _attend(streams, i, tk, bounded):
    return (_bounded_attention if bounded else _two_pass_attention)(streams, i, tk)


def _lane_fold(x, op):
    out = x[:, :LANES]
    for c in range(1, x.shape[1] // LANES):
        out = op(out, x[:, c * LANES:(c + 1) * LANES])
    return out


def _key_rows(j, tk, n=1):
    return pl.ds(pl.multiple_of(j * tk, tk), n * tk)


def _fold_tile_groups(lo, hi, body, carry, groups=TILE_GROUPS):
    for u in groups:
        n = (hi - lo) // u

        def trip(t, c, u=u, lo=lo):
            return body(lo + t * u, u, c)

        carry = lax.fori_loop(0, n, trip, carry)
        lo = lo + n * u
    return carry


def _for_tile_groups(lo, hi, body):
    _fold_tile_groups(lo, hi, lambda j0, u, c: (body(j0, u), c)[1], 0)


def _two_pass_attention(streams, i, tk):
    for st in streams:
        _, m_ref, l_ref, acc_ref = st.scratch
        m_ref[...] = jnp.full(m_ref.shape, NEG_INF, F32)
        l_ref[...] = jnp.zeros(l_ref.shape, F32)
        acc_ref[...] = jnp.zeros(acc_ref.shape, F32)

    def score(j0, u):
        for st in streams:
            s_ref, m_ref = st.scratch[:2]
            s = _dot_nt(st.qs_ref[...], st.k_rows(_key_rows(j0, tk, u)))
            m = m_ref[...]
            for t in range(u):
                s_t = st.bias(s[:, t * tk:(t + 1) * tk], j0 + t)
                s_ref[j0 + t] = s_t
                m = jnp.maximum(m, _lane_fold(s_t, jnp.maximum))
            m_ref[...] = m

    _for_tile_groups(0, i + 1, score)

    for st in streams:
        m_ref = st.scratch[1]
        m = jnp.max(m_ref[...], axis=-1, keepdims=True)
        m_ref[...] = jnp.broadcast_to(m, m_ref.shape)

    def pv(j0, u):
        for st in streams:
            s_ref, m_ref, l_ref, acc_ref = st.scratch
            m = jnp.concatenate([m_ref[...]] * (tk // LANES), axis=1)
            p = jnp.concatenate([jnp.exp2(s_ref[j0 + t] - m) for t in range(u)], axis=1)
            l_ref[...] += _lane_fold(p, jnp.add)
            acc_ref[...] += _dot(p.astype(BF16), st.v_rows(_key_rows(j0, tk, u)))

    _for_tile_groups(0, i + 1, pv)
    return [st.scratch[3][...] / jnp.sum(st.scratch[2][...], axis=-1, keepdims=True) for st in streams]


def _bounded_attention(streams, i, tk):
    for st in streams:
        acc_ref, = st.scratch
        acc_ref[...] = jnp.zeros(acc_ref.shape, F32)

    def step(j0, u):
        for st in streams:
            acc_ref, = st.scratch
            rows = _key_rows(j0, tk, u)
            s = _dot_nt(st.qs_ref[...], st.k_rows(rows))
            p = jnp.concatenate(
                [jnp.exp2(st.bias(s[:, t * tk:(t + 1) * tk], j0 + t)).astype(BF16) for t in range(u)],
                axis=1)
            v = st.v_rows(rows)
            acc_ref[...] += _dot(p, jnp.concatenate([v, jnp.ones_like(v)], axis=1))

    _for_tile_groups(0, i + 1, step)
    return [st.scratch[0][:, :HEAD_DIM] / st.scratch[0][:, HEAD_DIM:] for st in streams]


def _diff_body(lq1_ref, lk1_ref, lq2_ref, lk2_ref, q_ref, k_ref, v_ref, tab_ref, gsub_ref, o_ref,
               qs_ref, *scratch, tq, tk, hp, lambda_init, bounded):
    i = pl.program_id(2)
    lane = lax.broadcasted_iota(jnp.int32, (tq, HEAD_DIM), 1)
    streams = []
    for hh in range(hp):
        cols = slice(hh * HEAD_DIM, (hh + 1) * HEAD_DIM)
        q = q_ref[:, cols]
        zero = jnp.zeros_like(q)
        qs_ref[hh, :tq] = jnp.where(lane < DIFF_QK_DIM, q, zero)
        qs_ref[hh, tq:] = jnp.where(lane >= DIFF_QK_DIM, q, zero)

        def bias(s, j, hh=hh):
            tab = tab_ref[hh, jnp.minimum(i - j, FAR_TILE)]
            return s + jnp.concatenate([tab, tab], axis=0)

        streams.append(_Stream(
            qs_ref.at[hh],
            lambda rows, cols=cols: k_ref[rows, cols],
            lambda rows, cols=cols: v_ref[rows, cols],
            bias, tuple(r.at[hh] for r in scratch)))

    outs = _attend(streams, i, tk, bounded)

    lam = (jnp.exp(jnp.sum(lq1_ref[...] * lk1_ref[...], axis=-1, keepdims=True))
           - jnp.exp(jnp.sum(lq2_ref[...] * lk2_ref[...], axis=-1, keepdims=True)) + lambda_init)
    for hh, out in enumerate(outs):
        o = out[:tq] - lam * out[tq:]
        o_ref[:, hh * HEAD_DIM:(hh + 1) * HEAD_DIM] = (
            _rms_rows(o, gsub_ref[...]) * (1.0 - lambda_init)).astype(o_ref.dtype)


def _diff_attn(p3, tab, lq1, lk1, lq2, lk2, g_sub, lambda_init, tq, bounded):
    B, T, _ = p3.shape
    H = N_SELF_HEADS
    hp = 4 if bounded else 2
    nq = T // tq
    w = hp * HEAD_DIM
    vec = lambda a: a.reshape(1, -1).astype(F32)
    small = lambda n: pl.BlockSpec((1, n), lambda b, h, i: (0, 0))
    return pl.pallas_call(
        functools.partial(_diff_body, tq=tq, tk=tq, hp=hp, lambda_init=lambda_init, bounded=bounded),
        grid=(B, H // hp, nq),
        in_specs=[
            small(DIFF_QK_DIM), small(DIFF_QK_DIM), small(DIFF_QK_DIM), small(DIFF_QK_DIM),
            pl.BlockSpec((None, tq, w), lambda b, h, i: (b, i, h)),
            pl.BlockSpec((None, T, w), lambda b, h, i: (b, 0, H // hp + h)),
            pl.BlockSpec((None, T, w), lambda b, h, i: (b, 0, 2 * (H // hp) + h)),
            pl.BlockSpec((hp, FAR_TILE + 1, tq, tq), lambda b, h, i: (h, 0, 0, 0)),
            small(HEAD_DIM),
        ],
        out_specs=pl.BlockSpec((None, tq, w), lambda b, h, i: (b, i, h)),
        out_shape=jax.ShapeDtypeStruct((B, T, SELF_WIDTH), BF16),
        scratch_shapes=[pltpu.VMEM((hp, 2 * tq, HEAD_DIM), BF16)]
        + _attention_scratch(2 * tq, nq, tq, bounded, lead=(hp,)),
        compiler_params=_params(("parallel", "parallel", "arbitrary")),
        name="diff_attn",
    )(vec(lq1), vec(lk1), vec(lq2), vec(lk2), p3, p3, p3, tab, vec(g_sub))


SUBLANES = 8


def _sortable(x):
    bits = pltpu.bitcast(x, jnp.int32)
    return jnp.where(bits < 0, bits ^ jnp.int32(0x7FFFFFFF), bits)


def _dsa_select(iq_ref, ik_ref, iwt_ref, key_ref, madd_ref, iqm_ref, thr_ref, need_ref, i,
                *, tq, tk, topk):
    n_tiles = i + 1
    lane = lax.broadcasted_iota(jnp.int32, (tq, LANES), 1)
    for hh in range(IDX_HEADS):
        pair = iq_ref[:, (hh // 2) * LANES:(hh // 2 + 1) * LANES]
        keep = (lane < IDX_DIM) if hh % 2 == 0 else (lane >= IDX_DIM)
        iqm_ref[hh] = jnp.where(keep, pair, jnp.zeros_like(pair))

    key_pos = lax.broadcasted_iota(jnp.int32, (tk, tq), 0)
    qry_pos = lax.broadcasted_iota(jnp.int32, (tk, tq), 1)

    def causal(j):
        return (i * tq + qry_pos) >= (j * tk + key_pos)

    def score_tiles(j0, u, c):
        for j in [j0 + t for t in range(u)]:
            ik = ik_ref[_key_rows(j, tk), :]
            sc = jnp.zeros((tk, tq), F32)
            for hh in range(IDX_HEADS):
                sc = sc + jnp.maximum(_dot_nt(ik, iqm_ref[hh]), 0.0) * iwt_ref[hh:hh + 1, :]
            key_ref[j] = _sortable(jnp.where(causal(j), sc, NEG_INF))
        return c

    _fold_tile_groups(0, n_tiles, score_tiles, 0, groups=(2, 1))

    kf = float(topk)

    def count(cmp, cand):
        def body(j0, u, part):
            for t in range(u):
                hit = jnp.where(cmp(key_ref[j0 + t], cand), 1.0, 0.0)
                part = part + jnp.sum(hit.reshape(tk // SUBLANES, SUBLANES, tq), axis=0)
            return part

        part = _fold_tile_groups(0, n_tiles, body, jnp.zeros((SUBLANES, tq), F32), groups=(4, 2, 1))
        return jnp.sum(part, axis=0, keepdims=True)

    ge = lambda k, c: k >= c
    zero = jnp.zeros((1, tq), jnp.int32)
    total = (n_tiles * tk).astype(F32)
    n_pos = count(ge, zero)
    state = (jnp.where(n_pos >= kf, zero, jnp.int32(INT32_MIN)), jnp.where(n_pos >= kf, n_pos, total))

    def bit_step(b, state):
        base, n_base = state
        cand = base | lax.shift_left(jnp.int32(1), jnp.int32(30) - b)
        n = count(ge, cand)
        return jnp.where(n >= kf, cand, base), jnp.where(n >= kf, n, n_base)

    thr, n_thr = lax.fori_loop(0, 31, bit_step, state)
    thr_ref[...] = jnp.broadcast_to(thr, thr_ref.shape)
    surplus = jnp.maximum(jnp.max(n_thr - kf), 0.0)

    @pl.when(surplus == 0.0)
    def _():
        def mask_tile(j, c):
            take = (key_ref[j] >= thr_ref[0:1, :]) & causal(j)
            madd_ref[j] = jnp.where(take, 0.0, NEG_INF).T
            return c

        lax.fori_loop(0, n_tiles, mask_tile, 0)

    @pl.when(surplus != 0.0)
    def _():
        need_ref[...] = jnp.broadcast_to(kf - count(lambda k, c: k > c, thr_ref[0:1, :]), need_ref.shape)
        lower = jnp.where(lax.broadcasted_iota(jnp.int32, (tk, tk), 1)
                          < lax.broadcasted_iota(jnp.int32, (tk, tk), 0), 1.0, 0.0).astype(BF16)

        def mask_tile(j, seen):
            key = key_ref[j]
            thr = thr_ref[0:1, :]
            eq = jnp.where(key == thr, 1.0, 0.0)
            rank = seen + _dot(lower, eq.astype(BF16))
            take = (key > thr) | ((key == thr) & (rank < need_ref[0:1, :]))
            madd_ref[j] = jnp.where(take & causal(j), 0.0, NEG_INF).T
            return seen + jnp.sum(eq, axis=0, keepdims=True)

        lax.fori_loop(0, n_tiles, mask_tile, jnp.zeros((1, tq), F32))


def _dsa_body(q_ref, k_ref, v_ref, iq_ref, ik_ref, iw_ref, tab_ref, o_ref,
              key_ref, madd_ref, iqm_ref, thr_ref, need_ref, qs_ref, *scratch,
              tq, tk, topk, gp, bounded):
    i = pl.program_id(1)
    R = DSA_GROUP

    @pl.when(pl.program_id(2) == 0)
    def _():
        _dsa_select(iq_ref, ik_ref, iw_ref, key_ref, madd_ref, iqm_ref, thr_ref, need_ref, i,
                    tq=tq, tk=tk, topk=topk)

    streams = []
    for g in range(gp):
        for r in range(R):
            h = g * R + r
            qs_ref[g, r * tq:(r + 1) * tq] = q_ref[:, h * HEAD_DIM:(h + 1) * HEAD_DIM]
        cols = slice(g * HEAD_DIM, (g + 1) * HEAD_DIM)

        def bias(s, j, g=g):
            back = jnp.minimum(i - j, FAR_TILE)
            return s + jnp.concatenate([madd_ref[j] + tab_ref[g * R + r, back] for r in range(R)], axis=0)

        streams.append(_Stream(
            qs_ref.at[g],
            lambda rows, cols=cols: k_ref[rows, cols],
            lambda rows, cols=cols: v_ref[rows, cols],
            bias, tuple(ref.at[g] for ref in scratch)))

    for g, out in enumerate(_attend(streams, i, tk, bounded)):
        for r in range(R):
            h = g * R + r
            o_ref[:, h * HEAD_DIM:(h + 1) * HEAD_DIM] = out[r * tq:(r + 1) * tq].astype(o_ref.dtype)


def _dsa_attn(p3, ik2, iw, tab, tq, bounded):
    B, T, _ = p3.shape
    H, R, G = N_SELF_HEADS, DSA_GROUP, DSA_KV_HEADS
    topk = min(TOPK_MAX, T // 4)
    nq = T // tq
    iq_w = IDX_HEADS * IDX_DIM
    gp = 2 if bounded else 1
    kw = gp * HEAD_DIM
    k0 = H * HEAD_DIM // kw
    iq0 = (H + G) * HEAD_DIM // iq_w
    v0 = ((H + G) * HEAD_DIM + iq_w) // kw
    assert (H + G) * HEAD_DIM % iq_w == 0
    return pl.pallas_call(
        functools.partial(_dsa_body, tq=tq, tk=tq, topk=topk, gp=gp, bounded=bounded),
        grid=(B, nq, G // gp),
        in_specs=[
            pl.BlockSpec((None, tq, gp * R * HEAD_DIM), lambda b, i, g: (b, i, g)),
            pl.BlockSpec((None, T, kw), lambda b, i, g: (b, 0, k0 + g)),
            pl.BlockSpec((None, T, kw), lambda b, i, g: (b, 0, v0 + g)),
            pl.BlockSpec((None, tq, iq_w), lambda b, i, g: (b, i, iq0)),
            pl.BlockSpec((None, T, LANES), lambda b, i, g: (b, 0, 0)),
            pl.BlockSpec((None, IDX_HEADS, tq), lambda b, i, g: (b, 0, i)),
            pl.BlockSpec((gp * R, FAR_TILE + 1, tq, tq), lambda b, i, g: (g, 0, 0, 0)),
        ],
        out_specs=pl.BlockSpec((None, tq, gp * R * HEAD_DIM), lambda b, i, g: (b, i, g)),
        out_shape=jax.ShapeDtypeStruct((B, T, SELF_WIDTH), BF16),
        scratch_shapes=[
            pltpu.VMEM((nq, tq, tq), jnp.int32),
            pltpu.VMEM((nq, tq, tq), F32),
            pltpu.VMEM((IDX_HEADS, tq, LANES), BF16),
            pltpu.VMEM((SUBLANES, tq), jnp.int32),
            pltpu.VMEM((SUBLANES, tq), F32),
            pltpu.VMEM((gp, R * tq, HEAD_DIM), BF16),
        ] + _attention_scratch(R * tq, nq, tq, bounded, lead=(gp,)),
        compiler_params=_params(("parallel", "arbitrary", "arbitrary")),
        name="dsa_attn",
    )(p3, p3, p3, p3, ik2, iw, tab)


def _mem_body(q_ref, k_ref, v_ref, o_ref):
    for h in range(N_MEM_HEADS):
        sl = slice(h * HEAD_DIM, (h + 1) * HEAD_DIM)
        s = _dot_nt(q_ref[:, sl], k_ref[:, sl])
        p = jnp.exp2(s - jnp.max(s, axis=-1, keepdims=True))
        o = _dot(p.astype(BF16), v_ref[:, sl]) / jnp.sum(p, axis=-1, keepdims=True)
        o_ref[:, sl] = o.astype(o_ref.dtype)


def _mem_attn(p3, kv3, qm_block):
    B, T, _ = p3.shape
    N = kv3.shape[1]
    tq = min(512, T)
    return pl.pallas_call(
        _mem_body,
        grid=(B, T // tq),
        in_specs=[
            pl.BlockSpec((None, tq, MEM_WIDTH), lambda b, i: (b, i, qm_block)),
            pl.BlockSpec((None, N, MEM_WIDTH), lambda b, i: (b, 0, 0)),
            pl.BlockSpec((None, N, MEM_WIDTH), lambda b, i: (b, 0, 1)),
        ],
        out_specs=pl.BlockSpec((None, tq, MEM_WIDTH), lambda b, i: (b, i, 0)),
        out_shape=jax.ShapeDtypeStruct((B, T, MEM_WIDTH), BF16),
        compiler_params=_params(("parallel", "arbitrary")),
        name="mem_attn",
    )(p3, kv3, kv3)


def _out_body(ys_ref, ym_ref, ws_ref, wm_ref, x_ref, o_ref):
    o_ref[...] = x_ref[...] + _dot(ys_ref[...], ws_ref[...]) + _dot(ym_ref[...], wm_ref[...])


def _out_proj(ys, ym, wo, layer, x2):
    M, D = x2.shape
    tm = min(512, M)
    return pl.pallas_call(
        _out_body,
        grid=(M // tm,),
        in_specs=[
            pl.BlockSpec((tm, SELF_WIDTH), lambda i: (i, 0)),
            pl.BlockSpec((tm, MEM_WIDTH), lambda i: (i, 0)),
            pl.BlockSpec((None, SELF_WIDTH, D), lambda i: (layer, 0, 0)),
            pl.BlockSpec((None, MEM_WIDTH, D), lambda i: (layer, SELF_WIDTH // MEM_WIDTH, 0)),
            pl.BlockSpec((tm, D), lambda i: (i, 0)),
        ],
        out_specs=pl.BlockSpec((tm, D), lambda i: (i, 0)),
        out_shape=jax.ShapeDtypeStruct((M, D), F32),
        compiler_params=_params(("parallel",)),
        name="out_proj",
    )(ys, ym, wo, wo, x2)


def _scores_bounded(gq, gk, dim, scale, rel_bias):
    qk = dim * jnp.max(jnp.abs(gq)) * jnp.max(jnp.abs(gk)) * (scale * LOG2E * 1.02)
    bias = jnp.max(jnp.abs(rel_bias - rel_bias[NUM_BUCKETS - 1:])) * LOG2E
    return qk + bias <= MAX_EXP2_ARG


def kernel(x, mem, rel_bias, ffn1_g, ffn1_w_gate, ffn1_w_up, ffn1_w_down, ffn2_g, ffn2_w_gate,
           ffn2_w_up, ffn2_w_down, mix_g, mem_g, mem_w_kv, mem_gq, mem_gk, w_out, a_w_in, a_gq, a_gk,
           a_lam_q1, a_lam_k1, a_lam_q2, a_lam_k2, a_g_sub, b_w_in, b_gq, b_gk):
    B, T, D = x.shape
    N = mem.shape[1]
    depth = ffn1_g.shape[0]
    M = B * T
    tq = min(256, T)
    head_scale = HEAD_DIM ** -0.5 * LOG2E
    ones = lambda n: jnp.ones((n,), F32)
    bf = lambda w: w.astype(BF16)

    x2 = x.reshape(M, D)
    mem2 = mem.reshape(B * N, D)
    tab = _bias_tiles(rel_bias, tq, tq)
    ffn_w = [(bf(ffn1_w_gate), bf(ffn1_w_up), bf(0.5 * ffn1_w_down)),
             (bf(ffn2_w_gate), bf(ffn2_w_up), bf(0.5 * ffn2_w_down))]
    ffn_g = [ffn1_g, ffn2_g]
    wo, w_kv, w_a = bf(w_out), bf(mem_w_kv), bf(a_w_in)

    for i in range(depth):
        x2 = _ffn(x2, ffn_g[0][i], *ffn_w[0], i)

        kv_gain = jnp.concatenate([jnp.tile(mem_gk[i], N_MEM_HEADS), ones(MEM_WIDTH)])
        kv3 = _proj(mem2, mem_g[i], w_kv, i, [2, 0], kv_gain, BF16, MEM_WIDTH)
        kv3 = kv3.reshape(B, N, 2 * MEM_WIDTH)
        qm_gain = jnp.tile(mem_gq[i], N_MEM_HEADS) * head_scale
        j = i // 2

        if i % 2 == 0:
            n_grp = SELF_WIDTH // DIFF_QK_DIM
            gain = jnp.concatenate([jnp.tile(a_gq[j], n_grp) * (DIFF_QK_DIM ** -0.5 * LOG2E),
                                    jnp.tile(a_gk[j], n_grp), ones(SELF_WIDTH), qm_gain])
            kinds = [1] * 6 + [0] * 3 + [2]
            p3 = _proj(x2, mix_g[i], w_a, j, kinds, gain, BF16, 512).reshape(B, T, -1)
            lambda_init = 0.8 - 0.6 * math.exp(-0.3 * i)
            attn = functools.partial(_diff_attn, p3, tab, a_lam_q1[j], a_lam_k1[j], a_lam_q2[j],
                                     a_lam_k2[j], a_g_sub[j], lambda_init, tq)
            bounded = _scores_bounded(a_gq[j], a_gk[j], DIFF_QK_DIM, DIFF_QK_DIM ** -0.5, rel_bias)
            qm_block = 3 * SELF_WIDTH // MEM_WIDTH
        else:
            w = b_w_in[j]
            kvw = DSA_KV_HEADS * HEAD_DIM
            iqw = IDX_HEADS * IDX_DIM
            c_q, c_k, c_v, c_iq = 0, SELF_WIDTH, SELF_WIDTH + kvw, SELF_WIDTH + 2 * kvw
            c_ik = c_iq + iqw
            c_iw = c_ik + IDX_DIM
            c_qm = c_iw + IDX_HEADS
            w_main = jnp.concatenate([w[:, c_q:c_k], w[:, c_k:c_v], w[:, c_iq:c_ik], w[:, c_v:c_iq],
                                      w[:, c_qm:]], axis=1)
            gain = jnp.concatenate([jnp.tile(b_gq[j], N_SELF_HEADS) * head_scale,
                                    jnp.tile(b_gk[j], DSA_KV_HEADS), ones(iqw), ones(kvw), qm_gain])
            kinds = [2, 2, 2, 2, 0, 0, 0, 2]
            p3 = _proj(x2, mix_g[i], bf(w_main)[None], 0, kinds, gain, BF16, 512).reshape(B, T, -1)
            pad = LANES - IDX_DIM - IDX_HEADS
            w_idx = jnp.concatenate([w[:, c_ik:c_qm], jnp.zeros((D, pad), F32)], axis=1)
            idx_gain = jnp.concatenate([ones(IDX_DIM),
                                        jnp.full((IDX_HEADS,), (IDX_DIM * IDX_HEADS) ** -0.5, F32),
                                        jnp.zeros((pad,), F32)])
            ikw = _proj(x2, mix_g[i], bf(w_idx)[None], 0, [0], idx_gain, F32, LANES).reshape(B, T, LANES)
            ik = ikw[:, :, :IDX_DIM].astype(BF16)
            ik2 = jnp.concatenate([ik, ik], axis=-1)
            iw = jnp.swapaxes(ikw[:, :, IDX_DIM:IDX_DIM + IDX_HEADS], 1, 2)
            attn = functools.partial(_dsa_attn, p3, ik2, iw, tab, tq)
            bounded = _scores_bounded(b_gq[j], b_gk[j], HEAD_DIM, HEAD_DIM ** -0.5, rel_bias)
            qm_block = (p3.shape[-1] - MEM_WIDTH) // MEM_WIDTH

        y_self = lax.cond(bounded, lambda: attn(True), lambda: attn(False))
        y_mem = _mem_attn(p3, kv3, qm_block)
        x2 = _out_proj(y_self.reshape(M, SELF_WIDTH), y_mem.reshape(M, MEM_WIDTH), wo, i, x2)

        x2 = _ffn(x2, ffn_g[1][i], *ffn_w[1], i)

    return x2.reshape(B, T, D)
```

```python
import functools
import math
from typing import Any, Callable, NamedTuple

import jax
import jax.numpy as jnp
from jax import lax
from jax.experimental import pallas as pl
from jax.experimental.pallas import tpu as pltpu

HEAD_DIM = 128
N_SELF_HEADS = 12
N_MEM_HEADS = 4
SELF_WIDTH = N_SELF_HEADS * HEAD_DIM
MEM_WIDTH = N_MEM_HEADS * HEAD_DIM
DIFF_QK_DIM = HEAD_DIM // 2
DSA_KV_HEADS = 4
DSA_GROUP = N_SELF_HEADS // DSA_KV_HEADS
IDX_HEADS = 16
IDX_DIM = 64
TOPK_MAX = 256
NUM_BUCKETS = 32
MAX_DISTANCE = 128
EPS = 1e-6
NEG_INF = -1e30
INT32_MIN = -(2 ** 31)
LOG2E = math.log2(math.e)

FAR_TILE = 2

LANES = 128
VMEM_LIMIT = 56 * 1024 * 1024

F32 = jnp.float32
BF16 = jnp.bfloat16


def _dot(a, b):
    return jnp.dot(a, b, preferred_element_type=F32)


def _dot_nt(a, b):
    return lax.dot_general(a, b, (((1,), (1,)), ((), ())), preferred_element_type=F32)


def _params(semantics):
    return pltpu.CompilerParams(dimension_semantics=semantics, vmem_limit_bytes=VMEM_LIMIT)


def _rms_rows(x, g):
    ms = jnp.mean(x * x, axis=-1, keepdims=True)
    return x * lax.rsqrt(ms + EPS) * g


def _ffn_body(x_ref, g_ref, wg_ref, wu_ref, wd_ref, o_ref, h_ref):
    @pl.when(pl.program_id(1) == 0)
    def _():
        x = x_ref[...]
        h_ref[...] = _rms_rows(x, g_ref[...]).astype(BF16)
        o_ref[...] = x

    h = h_ref[...]
    gate = _dot(h, wg_ref[...])
    up = _dot(h, wu_ref[...])
    act = (gate * jax.nn.sigmoid(gate) * up).astype(BF16)
    o_ref[...] += _dot(act, wd_ref[...])


def _ffn(x2, g, wg, wu, wd_half, layer):
    M, D = x2.shape
    F = wg.shape[2]
    tm = min(1024, M)
    tf = min(512, F)
    return pl.pallas_call(
        _ffn_body,
        grid=(M // tm, F // tf),
        in_specs=[
            pl.BlockSpec((tm, D), lambda i, f: (i, 0)),
            pl.BlockSpec((1, D), lambda i, f: (0, 0)),
            pl.BlockSpec((None, D, tf), lambda i, f: (layer, 0, f)),
            pl.BlockSpec((None, D, tf), lambda i, f: (layer, 0, f)),
            pl.BlockSpec((None, tf, D), lambda i, f: (layer, f, 0)),
        ],
        out_specs=pl.BlockSpec((tm, D), lambda i, f: (i, 0)),
        out_shape=jax.ShapeDtypeStruct((M, D), F32),
        scratch_shapes=[pltpu.VMEM((tm, D), BF16)],
        compiler_params=_params(("parallel", "arbitrary")),
        name="ffn",
    )(x2, g.reshape(1, D), wg, wu, wd_half)


def _proj_body(types_ref, x_ref, g_ref, w_ref, gmat_ref, gain_ref, o_ref, h_ref, *, tn, kw):
    j = pl.program_id(1)

    @pl.when(j == 0)
    def _():
        h_ref[...] = _rms_rows(x_ref[...], g_ref[...]).astype(BF16)

    y = _dot(h_ref[...], w_ref[...])
    for sec in range(tn // kw):
        kind = types_ref[j * (tn // kw) + sec]

        @pl.when(kind == 0)
        def _(sec=sec):
            cols = slice(sec * kw, (sec + 1) * kw)
            o_ref[:, cols] = (y[:, cols] * gain_ref[:, cols]).astype(o_ref.dtype)

        @pl.when(kind != 0)
        def _(sec=sec, kind=kind):
            gmat = gmat_ref[kind]
            for c in range(sec * kw // LANES, (sec + 1) * kw // LANES):
                sl = slice(c * LANES, (c + 1) * LANES)
                yc = y[:, sl]
                ms = _dot((yc * yc).astype(BF16), gmat)
                o_ref[:, sl] = (yc * lax.rsqrt(ms + EPS) * gain_ref[:, sl]).astype(o_ref.dtype)


def _group_mats():
    r = jnp.arange(LANES)
    same64 = (r[:, None] // DIFF_QK_DIM) == (r[None, :] // DIFF_QK_DIM)
    return jnp.stack([
        jnp.zeros((LANES, LANES), F32),
        jnp.where(same64, 1.0 / DIFF_QK_DIM, 0.0),
        jnp.full((LANES, LANES), 1.0 / HEAD_DIM, F32),
    ]).astype(BF16)


def _proj(x2, g, w, layer, kinds, gain, out_dtype, kw):
    M, D = x2.shape
    N = w.shape[2]
    tm = min(1024, M)
    tn = 2 * kw if N % (2 * kw) == 0 else kw
    kinds = jnp.asarray(kinds, jnp.int32)
    grid_spec = pltpu.PrefetchScalarGridSpec(
        num_scalar_prefetch=1,
        grid=(M // tm, N // tn),
        in_specs=[
            pl.BlockSpec((tm, D), lambda i, j, t: (i, 0)),
            pl.BlockSpec((1, D), lambda i, j, t: (0, 0)),
            pl.BlockSpec((None, D, tn), lambda i, j, t: (layer, 0, j)),
            pl.BlockSpec((3, LANES, LANES), lambda i, j, t: (0, 0, 0)),
            pl.BlockSpec((1, tn), lambda i, j, t: (0, j)),
        ],
        out_specs=pl.BlockSpec((tm, tn), lambda i, j, t: (i, j)),
        scratch_shapes=[pltpu.VMEM((tm, D), BF16)],
    )
    return pl.pallas_call(
        functools.partial(_proj_body, tn=tn, kw=kw),
        grid_spec=grid_spec,
        out_shape=jax.ShapeDtypeStruct((M, N), out_dtype),
        compiler_params=_params(("parallel", "arbitrary")),
        name="proj",
    )(kinds, x2, g.reshape(1, D), w, _group_mats(), gain.reshape(1, N))


def _bias_body(rb_ref, o_ref, *, tq, tk):
    h = pl.program_id(0)
    r = lax.broadcasted_iota(jnp.int32, (tq, tk), 0)
    c = lax.broadcasted_iota(jnp.int32, (tq, tk), 1)
    max_exact = NUM_BUCKETS // 2
    far = rb_ref[NUM_BUCKETS - 1, h]
    for off in range(FAR_TILE + 1):
        dist = r - c + off * tk
        n = jnp.maximum(dist, 0)
        nf = jnp.maximum(n, 1).astype(F32)
        large = max_exact + (jnp.log(nf / max_exact) / math.log(MAX_DISTANCE / max_exact)
                             * (NUM_BUCKETS - max_exact)).astype(jnp.int32)
        large = jnp.minimum(large, NUM_BUCKETS - 1)
        bucket = jnp.where(n < max_exact, n, large)
        val = jnp.zeros((tq, tk), F32)
        for b in range(NUM_BUCKETS):
            val = jnp.where(bucket == b, rb_ref[b, h], val)
        o_ref[off] = jnp.where(dist >= 0, (val - far) * LOG2E, NEG_INF)


def _bias_tiles(rel_bias, tq, tk):
    assert tq == tk and tk >= MAX_DISTANCE
    return pl.pallas_call(
        functools.partial(_bias_body, tq=tq, tk=tk),
        grid=(N_SELF_HEADS,),
        in_specs=[pl.BlockSpec(memory_space=pltpu.SMEM)],
        out_specs=pl.BlockSpec((None, FAR_TILE + 1, tq, tk), lambda h: (h, 0, 0, 0)),
        out_shape=jax.ShapeDtypeStruct((N_SELF_HEADS, FAR_TILE + 1, tq, tk), F32),
        compiler_params=_params(("arbitrary",)),
        name="bias_tiles",
    )(rel_bias)


class _Stream(NamedTuple):
    qs_ref: Any
    k_rows: Callable
    v_rows: Callable
    bias: Callable
    scratch: tuple


TILE_GROUPS = (8, 4, 2, 1)
MAX_EXP2_ARG = 96.0


def _attention_scratch(rows, n_tiles, tk, bounded, lead=()):
    if bounded:
        return [pltpu.VMEM(lead + (rows, 2 * HEAD_DIM), F32)]
    return [
        pltpu.VMEM(lead + (n_tiles, rows, tk), F32),
        pltpu.VMEM(lead + (rows, LANES), F32),
        pltpu.VMEM(lead + (rows, LANES), F32),
        pltpu.VMEM(lead + (rows, HEAD_DIM), F32),
    ]


def _attend(streams, i, tk, bounded):
    return (_bounded_attention if bounded else _two_pass_attention)(streams, i, tk)


def _lane_fold(x, op):
    out = x[:, :LANES]
    for c in range(1, x.shape[1] // LANES):
        out = op(out, x[:, c * LANES:(c + 1) * LANES])
    return out


def _key_rows(j, tk, n=1):
    return pl.ds(pl.multiple_of(j * tk, tk), n * tk)


def _fold_tile_groups(lo, hi, body, carry, groups=TILE_GROUPS):
    for u in groups:
        n = (hi - lo) // u

        def trip(t, c, u=u, lo=lo):
            return body(lo + t * u, u, c)

        carry = lax.fori_loop(0, n, trip, carry)
        lo = lo + n * u
    return carry


def _for_tile_groups(lo, hi, body):
    _fold_tile_groups(lo, hi, lambda j0, u, c: (body(j0, u), c)[1], 0)


def _two_pass_attention(streams, i, tk):
    for st in streams:
        _, m_ref, l_ref, acc_ref = st.scratch
        m_ref[...] = jnp.full(m_ref.shape, NEG_INF, F32)
        l_ref[...] = jnp.zeros(l_ref.shape, F32)
        acc_ref[...] = jnp.zeros(acc_ref.shape, F32)

    def score(j0, u):
        for st in streams:
            s_ref, m_ref = st.scratch[:2]
            s = _dot_nt(st.qs_ref[...], st.k_rows(_key_rows(j0, tk, u)))
            m = m_ref[...]
            for t in range(u):
                s_t = st.bias(s[:, t * tk:(t + 1) * tk], j0 + t)
                s_ref[j0 + t] = s_t
                m = jnp.maximum(m, _lane_fold(s_t, jnp.maximum))
            m_ref[...] = m

    _for_tile_groups(0, i + 1, score)

    for st in streams:
        m_ref = st.scratch[1]
        m = jnp.max(m_ref[...], axis=-1, keepdims=True)
        m_ref[...] = jnp.broadcast_to(m, m_ref.shape)

    def pv(j0, u):
        for st in streams:
            s_ref, m_ref, l_ref, acc_ref = st.scratch
            m = jnp.concatenate([m_ref[...]] * (tk // LANES), axis=1)
            p = jnp.concatenate([jnp.exp2(s_ref[j0 + t] - m) for t in range(u)], axis=1)
            l_ref[...] += _lane_fold(p, jnp.add)
            acc_ref[...] += _dot(p.astype(BF16), st.v_rows(_key_rows(j0, tk, u)))

    _for_tile_groups(0, i + 1, pv)
    return [st.scratch[3][...] / jnp.sum(st.scratch[2][...], axis=-1, keepdims=True) for st in streams]


def _bounded_attention(streams, i, tk):
    for st in streams:
        acc_ref, = st.scratch
        acc_ref[...] = jnp.zeros(acc_ref.shape, F32)

    def step(j0, u):
        for st in streams:
            acc_ref, = st.scratch
            rows = _key_rows(j0, tk, u)
            s = _dot_nt(st.qs_ref[...], st.k_rows(rows))
            p = jnp.concatenate(
                [jnp.exp2(st.bias(s[:, t * tk:(t + 1) * tk], j0 + t)).astype(BF16) for t in range(u)],
                axis=1)
            v = st.v_rows(rows)
            acc_ref[...] += _dot(p, jnp.concatenate([v, jnp.ones_like(v)], axis=1))

    _for_tile_groups(0, i + 1, step)
    return [st.scratch[0][:, :HEAD_DIM] / st.scratch[0][:, HEAD_DIM:] for st in streams]


def _diff_body(lq1_ref, lk1_ref, lq2_ref, lk2_ref, q_ref, k_ref, v_ref, tab_ref, gsub_ref, o_ref,
               qs_ref, *scratch, tq, tk, hp, lambda_init, bounded):
    i = pl.program_id(2)
    lane = lax.broadcasted_iota(jnp.int32, (tq, HEAD_DIM), 1)
    streams = []
    for hh in range(hp):
        cols = slice(hh * HEAD_DIM, (hh + 1) * HEAD_DIM)
        q = q_ref[:, cols]
        zero = jnp.zeros_like(q)
        qs_ref[hh, :tq] = jnp.where(lane < DIFF_QK_DIM, q, zero)
        qs_ref[hh, tq:] = jnp.where(lane >= DIFF_QK_DIM, q, zero)

        def bias(s, j, hh=hh):
            tab = tab_ref[hh, jnp.minimum(i - j, FAR_TILE)]
            return s + jnp.concatenate([tab, tab], axis=0)

        streams.append(_Stream(
            qs_ref.at[hh],
            lambda rows, cols=cols: k_ref[rows, cols],
            lambda rows, cols=cols: v_ref[rows, cols],
            bias, tuple(r.at[hh] for r in scratch)))

    outs = _attend(streams, i, tk, bounded)

    lam = (jnp.exp(jnp.sum(lq1_ref[...] * lk1_ref[...], axis=-1, keepdims=True))
           - jnp.exp(jnp.sum(lq2_ref[...] * lk2_ref[...], axis=-1, keepdims=True)) + lambda_init)
    for hh, out in enumerate(outs):
        o = out[:tq] - lam * out[tq:]
        o_ref[:, hh * HEAD_DIM:(hh + 1) * HEAD_DIM] = (
            _rms_rows(o, gsub_ref[...]) * (1.0 - lambda_init)).astype(o_ref.dtype)


def _diff_attn(p3, tab, lq1, lk1, lq2, lk2, g_sub, lambda_init, tq, bounded):
    B, T, _ = p3.shape
    H = N_SELF_HEADS
    hp = 4 if bounded else 2
    nq = T // tq
    w = hp * HEAD_DIM
    vec = lambda a: a.reshape(1, -1).astype(F32)
    small = lambda n: pl.BlockSpec((1, n), lambda b, h, i: (0, 0))
    return pl.pallas_call(
        functools.partial(_diff_body, tq=tq, tk=tq, hp=hp, lambda_init=lambda_init, bounded=bounded),
        grid=(B, H // hp, nq),
        in_specs=[
            small(DIFF_QK_DIM), small(DIFF_QK_DIM), small(DIFF_QK_DIM), small(DIFF_QK_DIM),
            pl.BlockSpec((None, tq, w), lambda b, h, i: (b, i, h)),
            pl.BlockSpec((None, T, w), lambda b, h, i: (b, 0, H // hp + h)),
            pl.BlockSpec((None, T, w), lambda b, h, i: (b, 0, 2 * (H // hp) + h)),
            pl.BlockSpec((hp, FAR_TILE + 1, tq, tq), lambda b, h, i: (h, 0, 0, 0)),
            small(HEAD_DIM),
        ],
        out_specs=pl.BlockSpec((None, tq, w), lambda b, h, i: (b, i, h)),
        out_shape=jax.ShapeDtypeStruct((B, T, SELF_WIDTH), BF16),
        scratch_shapes=[pltpu.VMEM((hp, 2 * tq, HEAD_DIM), BF16)]
        + _attention_scratch(2 * tq, nq, tq, bounded, lead=(hp,)),
        compiler_params=_params(("parallel", "parallel", "arbitrary")),
        name="diff_attn",
    )(vec(lq1), vec(lk1), vec(lq2), vec(lk2), p3, p3, p3, tab, vec(g_sub))


SUBLANES = 8


def _sortable(x):
    bits = pltpu.bitcast(x, jnp.int32)
    return jnp.where(bits < 0, bits ^ jnp.int32(0x7FFFFFFF), bits)


def _dsa_select(iq_ref, ik_ref, iwt_ref, key_ref, madd_ref, iqm_ref, thr_ref, need_ref, i,
                *, tq, tk, topk):
    n_tiles = i + 1
    lane = lax.broadcasted_iota(jnp.int32, (tq, LANES), 1)
    for hh in range(IDX_HEADS):
        pair = iq_ref[:, (hh // 2) * LANES:(hh // 2 + 1) * LANES]
        keep = (lane < IDX_DIM) if hh % 2 == 0 else (lane >= IDX_DIM)
        iqm_ref[hh] = jnp.where(keep, pair, jnp.zeros_like(pair))

    key_pos = lax.broadcasted_iota(jnp.int32, (tk, tq), 0)
    qry_pos = lax.broadcasted_iota(jnp.int32, (tk, tq), 1)

    def causal(j):
        return (i * tq + qry_pos) >= (j * tk + key_pos)

    def score_tiles(j0, u, c):
        for j in [j0 + t for t in range(u)]:
            ik = ik_ref[_key_rows(j, tk), :]
            sc = jnp.zeros((tk, tq), F32)
            for hh in range(IDX_HEADS):
                sc = sc + jnp.maximum(_dot_nt(ik, iqm_ref[hh]), 0.0) * iwt_ref[hh:hh + 1, :]
            key_ref[j] = _sortable(jnp.where(causal(j), sc, NEG_INF))
        return c

    _fold_tile_groups(0, n_tiles, score_tiles, 0, groups=(4, 2, 1))

    kf = float(topk)

    def count(cmp, cand):
        def body(j0, u, part):
            for t in range(u):
                hit = jnp.where(cmp(key_ref[j0 + t], cand), 1.0, 0.0)
                part = part + jnp.sum(hit.reshape(tk // SUBLANES, SUBLANES, tq), axis=0)
            return part

        part = _fold_tile_groups(0, n_tiles, body, jnp.zeros((SUBLANES, tq), F32), groups=(4, 2, 1))
        return jnp.sum(part, axis=0, keepdims=True)

    ge = lambda k, c: k >= c
    zero = jnp.zeros((1, tq), jnp.int32)
    total = (n_tiles * tk).astype(F32)
    n_pos = count(ge, zero)
    state = (jnp.where(n_pos >= kf, zero, jnp.int32(INT32_MIN)), jnp.where(n_pos >= kf, n_pos, total))

    def bit_step(b, state):
        base, n_base = state
        cand = base | lax.shift_left(jnp.int32(1), jnp.int32(30) - b)
        n = count(ge, cand)
        return jnp.where(n >= kf, cand, base), jnp.where(n >= kf, n, n_base)

    thr, n_thr = lax.fori_loop(0, 31, bit_step, state)
    thr_ref[...] = jnp.broadcast_to(thr, thr_ref.shape)
    surplus = jnp.maximum(jnp.max(n_thr - kf), 0.0)

    @pl.when(surplus == 0.0)
    def _():
        def mask_tile(j, c):
            take = (key_ref[j] >= thr_ref[0:1, :]) & causal(j)
            madd_ref[j] = jnp.where(take, 0.0, NEG_INF).T
            return c

        lax.fori_loop(0, n_tiles, mask_tile, 0)

    @pl.when(surplus != 0.0)
    def _():
        need_ref[...] = jnp.broadcast_to(kf - count(lambda k, c: k > c, thr_ref[0:1, :]), need_ref.shape)
        lower = jnp.where(lax.broadcasted_iota(jnp.int32, (tk, tk), 1)
                          < lax.broadcasted_iota(jnp.int32, (tk, tk), 0), 1.0, 0.0).astype(BF16)

        def mask_tile(j, seen):
            key = key_ref[j]
            thr = thr_ref[0:1, :]
            eq = jnp.where(key == thr, 1.0, 0.0)
            rank = seen + _dot(lower, eq.astype(BF16))
            take = (key > thr) | ((key == thr) & (rank < need_ref[0:1, :]))
            madd_ref[j] = jnp.where(take & causal(j), 0.0, NEG_INF).T
            return seen + jnp.sum(eq, axis=0, keepdims=True)

        lax.fori_loop(0, n_tiles, mask_tile, jnp.zeros((1, tq), F32))


def _dsa_body(q_ref, k_ref, v_ref, iq_ref, ik_ref, iw_ref, tab_ref, o_ref,
              key_ref, madd_ref, iqm_ref, thr_ref, need_ref, qs_ref, *scratch,
              tq, tk, topk, gp, bounded):
    i = pl.program_id(1)
    R = DSA_GROUP

    @pl.when(pl.program_id(2) == 0)
    def _():
        _dsa_select(iq_ref, ik_ref, iw_ref, key_ref, madd_ref, iqm_ref, thr_ref, need_ref, i,
                    tq=tq, tk=tk, topk=topk)

    streams = []
    for g in range(gp):
        for r in range(R):
            h = g * R + r
            qs_ref[g, r * tq:(r + 1) * tq] = q_ref[:, h * HEAD_DIM:(h + 1) * HEAD_DIM]
        cols = slice(g * HEAD_DIM, (g + 1) * HEAD_DIM)

        def bias(s, j, g=g):
            back = jnp.minimum(i - j, FAR_TILE)
            return s + jnp.concatenate([madd_ref[j] + tab_ref[g * R + r, back] for r in range(R)], axis=0)

        streams.append(_Stream(
            qs_ref.at[g],
            lambda rows, cols=cols: k_ref[rows, cols],
            lambda rows, cols=cols: v_ref[rows, cols],
            bias, tuple(ref.at[g] for ref in scratch)))

    for g, out in enumerate(_attend(streams, i, tk, bounded)):
        for r in range(R):
            h = g * R + r
            o_ref[:, h * HEAD_DIM:(h + 1) * HEAD_DIM] = out[r * tq:(r + 1) * tq].astype(o_ref.dtype)


def _dsa_attn(p3, ik2, iw, tab, tq, bounded):
    B, T, _ = p3.shape
    H, R, G = N_SELF_HEADS, DSA_GROUP, DSA_KV_HEADS
    topk = min(TOPK_MAX, T // 4)
    nq = T // tq
    iq_w = IDX_HEADS * IDX_DIM
    gp = G if bounded else 1
    kw = gp * HEAD_DIM
    k0 = H * HEAD_DIM // kw
    iq0 = (H + G) * HEAD_DIM // iq_w
    v0 = ((H + G) * HEAD_DIM + iq_w) // kw
    assert (H + G) * HEAD_DIM % iq_w == 0
    return pl.pallas_call(
        functools.partial(_dsa_body, tq=tq, tk=tq, topk=topk, gp=gp, bounded=bounded),
        grid=(B, nq, G // gp),
        in_specs=[
            pl.BlockSpec((None, tq, gp * R * HEAD_DIM), lambda b, i, g: (b, i, g)),
            pl.BlockSpec((None, T, kw), lambda b, i, g: (b, 0, k0 + g)),
            pl.BlockSpec((None, T, kw), lambda b, i, g: (b, 0, v0 + g)),
            pl.BlockSpec((None, tq, iq_w), lambda b, i, g: (b, i, iq0)),
            pl.BlockSpec((None, T, LANES), lambda b, i, g: (b, 0, 0)),
            pl.BlockSpec((None, IDX_HEADS, tq), lambda b, i, g: (b, 0, i)),
            pl.BlockSpec((gp * R, FAR_TILE + 1, tq, tq), lambda b, i, g: (g, 0, 0, 0)),
        ],
        out_specs=pl.BlockSpec((None, tq, gp * R * HEAD_DIM), lambda b, i, g: (b, i, g)),
        out_shape=jax.ShapeDtypeStruct((B, T, SELF_WIDTH), BF16),
        scratch_shapes=[
            pltpu.VMEM((nq, tq, tq), jnp.int32),
            pltpu.VMEM((nq, tq, tq), F32),
            pltpu.VMEM((IDX_HEADS, tq, LANES), BF16),
            pltpu.VMEM((SUBLANES, tq), jnp.int32),
            pltpu.VMEM((SUBLANES, tq), F32),
            pltpu.VMEM((gp, R * tq, HEAD_DIM), BF16),
        ] + _attention_scratch(R * tq, nq, tq, bounded, lead=(gp,)),
        compiler_params=_params(("parallel", "arbitrary", "arbitrary")),
        name="dsa_attn",
    )(p3, p3, p3, p3, ik2, iw, tab)


def _out_body(ys_ref, qm_ref, km_ref, vm_ref, ws_ref, wm_ref, x_ref, o_ref):
    acc = x_ref[...] + _dot(ys_ref[...], ws_ref[...])
    heads = []
    for h in range(N_MEM_HEADS):
        sl = slice(h * HEAD_DIM, (h + 1) * HEAD_DIM)
        s = _dot_nt(qm_ref[:, sl], km_ref[:, sl])
        p = jnp.exp2(s - jnp.max(s, axis=-1, keepdims=True))
        o = _dot(p.astype(BF16), vm_ref[:, sl]) / jnp.sum(p, axis=-1, keepdims=True)
        heads.append(o.astype(BF16))
    o_ref[...] = acc + _dot(jnp.concatenate(heads, axis=1), wm_ref[...])


def _out_proj(ys, p2, qm_block, kv3, wo, layer, x2):
    M, D = x2.shape
    B, N, _ = kv3.shape
    tm = min(512, M // B)
    per_batch = M // B // tm
    return pl.pallas_call(
        _out_body,
        grid=(M // tm,),
        in_specs=[
            pl.BlockSpec((tm, SELF_WIDTH), lambda i: (i, 0)),
            pl.BlockSpec((tm, MEM_WIDTH), lambda i: (i, qm_block)),
            pl.BlockSpec((None, N, MEM_WIDTH), lambda i: (i // per_batch, 0, 0)),
            pl.BlockSpec((None, N, MEM_WIDTH), lambda i: (i // per_batch, 0, 1)),
            pl.BlockSpec((None, SELF_WIDTH, D), lambda i: (layer, 0, 0)),
            pl.BlockSpec((None, MEM_WIDTH, D), lambda i: (layer, SELF_WIDTH // MEM_WIDTH, 0)),
            pl.BlockSpec((tm, D), lambda i: (i, 0)),
        ],
        out_specs=pl.BlockSpec((tm, D), lambda i: (i, 0)),
        out_shape=jax.ShapeDtypeStruct((M, D), F32),
        compiler_params=_params(("parallel",)),
        name="out_proj",
    )(ys, p2, kv3, kv3, wo, wo, x2)


def _scores_bounded(gq, gk, dim, scale, rel_bias):
    qk = dim * jnp.max(jnp.abs(gq)) * jnp.max(jnp.abs(gk)) * (scale * LOG2E * 1.02)
    bias = jnp.max(jnp.abs(rel_bias - rel_bias[NUM_BUCKETS - 1:])) * LOG2E
    return qk + bias <= MAX_EXP2_ARG


def kernel(x, mem, rel_bias, ffn1_g, ffn1_w_gate, ffn1_w_up, ffn1_w_down, ffn2_g, ffn2_w_gate,
           ffn2_w_up, ffn2_w_down, mix_g, mem_g, mem_w_kv, mem_gq, mem_gk, w_out, a_w_in, a_gq, a_gk,
           a_lam_q1, a_lam_k1, a_lam_q2, a_lam_k2, a_g_sub, b_w_in, b_gq, b_gk):
    B, T, D = x.shape
    N = mem.shape[1]
    depth = ffn1_g.shape[0]
    M = B * T
    tq = min(256, T)
    head_scale = HEAD_DIM ** -0.5 * LOG2E
    ones = lambda n: jnp.ones((n,), F32)
    bf = lambda w: w.astype(BF16)

    x2 = x.reshape(M, D)
    mem2 = mem.reshape(B * N, D)
    tab = _bias_tiles(rel_bias, tq, tq)
    ffn_w = [(bf(ffn1_w_gate), bf(ffn1_w_up), bf(0.5 * ffn1_w_down)),
             (bf(ffn2_w_gate), bf(ffn2_w_up), bf(0.5 * ffn2_w_down))]
    ffn_g = [ffn1_g, ffn2_g]
    wo, w_kv, w_a = bf(w_out), bf(mem_w_kv), bf(a_w_in)

    for i in range(depth):
        x2 = _ffn(x2, ffn_g[0][i], *ffn_w[0], i)

        kv_gain = jnp.concatenate([jnp.tile(mem_gk[i], N_MEM_HEADS), ones(MEM_WIDTH)])
        kv3 = _proj(mem2, mem_g[i], w_kv, i, [2, 0], kv_gain, BF16, MEM_WIDTH)
        kv3 = kv3.reshape(B, N, 2 * MEM_WIDTH)
        qm_gain = jnp.tile(mem_gq[i], N_MEM_HEADS) * head_scale
        j = i // 2

        if i % 2 == 0:
            n_grp = SELF_WIDTH // DIFF_QK_DIM
            gain = jnp.concatenate([jnp.tile(a_gq[j], n_grp) * (DIFF_QK_DIM ** -0.5 * LOG2E),
                                    jnp.tile(a_gk[j], n_grp), ones(SELF_WIDTH), qm_gain])
            kinds = [1] * 6 + [0] * 3 + [2]
            p3 = _proj(x2, mix_g[i], w_a, j, kinds, gain, BF16, 512).reshape(B, T, -1)
            lambda_init = 0.8 - 0.6 * math.exp(-0.3 * i)
            attn = functools.partial(_diff_attn, p3, tab, a_lam_q1[j], a_lam_k1[j], a_lam_q2[j],
                                     a_lam_k2[j], a_g_sub[j], lambda_init, tq)
            bounded = _scores_bounded(a_gq[j], a_gk[j], DIFF_QK_DIM, DIFF_QK_DIM ** -0.5, rel_bias)
            qm_block = 3 * SELF_WIDTH // MEM_WIDTH
        else:
            w = b_w_in[j]
            kvw = DSA_KV_HEADS * HEAD_DIM
            iqw = IDX_HEADS * IDX_DIM
            c_q, c_k, c_v, c_iq = 0, SELF_WIDTH, SELF_WIDTH + kvw, SELF_WIDTH + 2 * kvw
            c_ik = c_iq + iqw
            c_iw = c_ik + IDX_DIM
            c_qm = c_iw + IDX_HEADS
            w_main = jnp.concatenate([w[:, c_q:c_k], w[:, c_k:c_v], w[:, c_iq:c_ik], w[:, c_v:c_iq],
                                      w[:, c_qm:]], axis=1)
            gain = jnp.concatenate([jnp.tile(b_gq[j], N_SELF_HEADS) * head_scale,
                                    jnp.tile(b_gk[j], DSA_KV_HEADS), ones(iqw), ones(kvw), qm_gain])
            kinds = [2, 2, 2, 2, 0, 0, 0, 2]
            p3 = _proj(x2, mix_g[i], bf(w_main)[None], 0, kinds, gain, BF16, 512).reshape(B, T, -1)
            pad = LANES - IDX_DIM - IDX_HEADS
            w_idx = jnp.concatenate([w[:, c_ik:c_qm], jnp.zeros((D, pad), F32)], axis=1)
            idx_gain = jnp.concatenate([ones(IDX_DIM),
                                        jnp.full((IDX_HEADS,), (IDX_DIM * IDX_HEADS) ** -0.5, F32),
                                        jnp.zeros((pad,), F32)])
            ikw = _proj(x2, mix_g[i], bf(w_idx)[None], 0, [0], idx_gain, F32, LANES).reshape(B, T, LANES)
            ik = ikw[:, :, :IDX_DIM].astype(BF16)
            ik2 = jnp.concatenate([ik, ik], axis=-1)
            iw = jnp.swapaxes(ikw[:, :, IDX_DIM:IDX_DIM + IDX_HEADS], 1, 2)
            attn = functools.partial(_dsa_attn, p3, ik2, iw, tab, tq)
            bounded = _scores_bounded(b_gq[j], b_gk[j], HEAD_DIM, HEAD_DIM ** -0.5, rel_bias)
            qm_block = (p3.shape[-1] - MEM_WIDTH) // MEM_WIDTH

        y_self = lax.cond(bounded, lambda: attn(True), lambda: attn(False))
        x2 = _out_proj(y_self.reshape(M, SELF_WIDTH), p3.reshape(M, -1), qm_block, kv3, wo, i, x2)

        x2 = _ffn(x2, ffn_g[1][i], *ffn_w[1], i)

    return x2.reshape(B, T, D)
```

```python
import functools
import math
from typing import Any, Callable, NamedTuple

import jax
import jax.numpy as jnp
from jax import lax
from jax.experimental import pallas as pl
from jax.experimental.pallas import tpu as pltpu

HEAD_DIM = 128
N_SELF_HEADS = 12
N_MEM_HEADS = 4
SELF_WIDTH = N_SELF_HEADS * HEAD_DIM
MEM_WIDTH = N_MEM_HEADS * HEAD_DIM
DIFF_QK_DIM = HEAD_DIM // 2
DSA_KV_HEADS = 4
DSA_GROUP = N_SELF_HEADS // DSA_KV_HEADS
IDX_HEADS = 16
IDX_DIM = 64
TOPK_MAX = 256
NUM_BUCKETS = 32
MAX_DISTANCE = 128
EPS = 1e-6
NEG_INF = -1e30
INT32_MIN = -(2 ** 31)
LOG2E = math.log2(math.e)

FAR_TILE = 2

LANES = 128
VMEM_LIMIT = 56 * 1024 * 1024

F32 = jnp.float32
BF16 = jnp.bfloat16


def _dot(a, b):
    return jnp.dot(a, b, preferred_element_type=F32)


def _dot_nt(a, b):
    return lax.dot_general(a, b, (((1,), (1,)), ((), ())), preferred_element_type=F32)


def _params(semantics):
    return pltpu.CompilerParams(dimension_semantics=semantics, vmem_limit_bytes=VMEM_LIMIT)


def _rms_rows(x, g):
    ms = jnp.mean(x * x, axis=-1, keepdims=True)
    return x * lax.rsqrt(ms + EPS) * g


def _ffn_body(x_ref, g_ref, wg_ref, wu_ref, wd_ref, o_ref, h_ref):
    @pl.when(pl.program_id(1) == 0)
    def _():
        x = x_ref[...]
        h_ref[...] = _rms_rows(x, g_ref[...]).astype(BF16)
        o_ref[...] = x

    h = h_ref[...]
    gate = _dot(h, wg_ref[...])
    up = _dot(h, wu_ref[...])
    act = (gate * jax.nn.sigmoid(gate) * up).astype(BF16)
    o_ref[...] += _dot(act, wd_ref[...])


def _ffn(x2, g, wg, wu, wd_half, layer):
    M, D = x2.shape
    F = wg.shape[2]
    tm = min(1024, M)
    tf = min(512, F)
    return pl.pallas_call(
        _ffn_body,
        grid=(M // tm, F // tf),
        in_specs=[
            pl.BlockSpec((tm, D), lambda i, f: (i, 0)),
            pl.BlockSpec((1, D), lambda i, f: (0, 0)),
            pl.BlockSpec((None, D, tf), lambda i, f: (layer, 0, f)),
            pl.BlockSpec((None, D, tf), lambda i, f: (layer, 0, f)),
            pl.BlockSpec((None, tf, D), lambda i, f: (layer, f, 0)),
        ],
        out_specs=pl.BlockSpec((tm, D), lambda i, f: (i, 0)),
        out_shape=jax.ShapeDtypeStruct((M, D), F32),
        scratch_shapes=[pltpu.VMEM((tm, D), BF16)],
        compiler_params=_params(("parallel", "arbitrary")),
        name="ffn",
    )(x2, g.reshape(1, D), wg, wu, wd_half)


def _proj_body(types_ref, x_ref, g_ref, w_ref, gmat_ref, gain_ref, o_ref, h_ref, *, tn, kw):
    j = pl.program_id(1)

    @pl.when(j == 0)
    def _():
        h_ref[...] = _rms_rows(x_ref[...], g_ref[...]).astype(BF16)

    y = _dot(h_ref[...], w_ref[...])
    for sec in range(tn // kw):
        kind = types_ref[j * (tn // kw) + sec]

        @pl.when(kind == 0)
        def _(sec=sec):
            cols = slice(sec * kw, (sec + 1) * kw)
            o_ref[:, cols] = (y[:, cols] * gain_ref[:, cols]).astype(o_ref.dtype)

        @pl.when(kind != 0)
        def _(sec=sec, kind=kind):
            gmat = gmat_ref[kind]
            for c in range(sec * kw // LANES, (sec + 1) * kw // LANES):
                sl = slice(c * LANES, (c + 1) * LANES)
                yc = y[:, sl]
                ms = _dot((yc * yc).astype(BF16), gmat)
                o_ref[:, sl] = (yc * lax.rsqrt(ms + EPS) * gain_ref[:, sl]).astype(o_ref.dtype)


def _group_mats():
    r = jnp.arange(LANES)
    same64 = (r[:, None] // DIFF_QK_DIM) == (r[None, :] // DIFF_QK_DIM)
    return jnp.stack([
        jnp.zeros((LANES, LANES), F32),
        jnp.where(same64, 1.0 / DIFF_QK_DIM, 0.0),
        jnp.full((LANES, LANES), 1.0 / HEAD_DIM, F32),
    ]).astype(BF16)


def _proj(x2, g, w, layer, kinds, gain, out_dtype, kw):
    M, D = x2.shape
    N = w.shape[2]
    tm = min(1024, M)
    tn = 2 * kw if N % (2 * kw) == 0 else kw
    kinds = jnp.asarray(kinds, jnp.int32)
    grid_spec = pltpu.PrefetchScalarGridSpec(
        num_scalar_prefetch=1,
        grid=(M // tm, N // tn),
        in_specs=[
            pl.BlockSpec((tm, D), lambda i, j, t: (i, 0)),
            pl.BlockSpec((1, D), lambda i, j, t: (0, 0)),
            pl.BlockSpec((None, D, tn), lambda i, j, t: (layer, 0, j)),
            pl.BlockSpec((3, LANES, LANES), lambda i, j, t: (0, 0, 0)),
            pl.BlockSpec((1, tn), lambda i, j, t: (0, j)),
        ],
        out_specs=pl.BlockSpec((tm, tn), lambda i, j, t: (i, j)),
        scratch_shapes=[pltpu.VMEM((tm, D), BF16)],
    )
    return pl.pallas_call(
        functools.partial(_proj_body, tn=tn, kw=kw),
        grid_spec=grid_spec,
        out_shape=jax.ShapeDtypeStruct((M, N), out_dtype),
        compiler_params=_params(("parallel", "arbitrary")),
        name="proj",
    )(kinds, x2, g.reshape(1, D), w, _group_mats(), gain.reshape(1, N))


def _bias_body(rb_ref, o_ref, *, tq, tk):
    h = pl.program_id(0)
    r = lax.broadcasted_iota(jnp.int32, (tq, tk), 0)
    c = lax.broadcasted_iota(jnp.int32, (tq, tk), 1)
    max_exact = NUM_BUCKETS // 2
    far = rb_ref[NUM_BUCKETS - 1, h]
    for off in range(FAR_TILE + 1):
        dist = r - c + off * tk
        n = jnp.maximum(dist, 0)
        nf = jnp.maximum(n, 1).astype(F32)
        large = max_exact + (jnp.log(nf / max_exact) / math.log(MAX_DISTANCE / max_exact)
                             * (NUM_BUCKETS - max_exact)).astype(jnp.int32)
        large = jnp.minimum(large, NUM_BUCKETS - 1)
        bucket = jnp.where(n < max_exact, n, large)
        val = jnp.zeros((tq, tk), F32)
        for b in range(NUM_BUCKETS):
            val = jnp.where(bucket == b, rb_ref[b, h], val)
        o_ref[off] = jnp.where(dist >= 0, (val - far) * LOG2E, NEG_INF)


def _bias_tiles(rel_bias, tq, tk):
    assert tq == tk and tk >= MAX_DISTANCE
    return pl.pallas_call(
        functools.partial(_bias_body, tq=tq, tk=tk),
        grid=(N_SELF_HEADS,),
        in_specs=[pl.BlockSpec(memory_space=pltpu.SMEM)],
        out_specs=pl.BlockSpec((None, FAR_TILE + 1, tq, tk), lambda h: (h, 0, 0, 0)),
        out_shape=jax.ShapeDtypeStruct((N_SELF_HEADS, FAR_TILE + 1, tq, tk), F32),
        compiler_params=_params(("arbitrary",)),
        name="bias_tiles",
    )(rel_bias)


class _Stream(NamedTuple):
    qs_ref: Any
    k_rows: Callable
    v_rows: Callable
    bias: Callable
    scratch: tuple


TILE_GROUPS = (8, 4, 2, 1)
MAX_EXP2_ARG = 60.0


def _attention_scratch(rows, n_tiles, tk, bounded, lead=()):
    if bounded:
        return [pltpu.VMEM(lead + (rows, 2 * HEAD_DIM), F32)]
    return [
        pltpu.VMEM(lead + (n_tiles, rows, tk), F32),
        pltpu.VMEM(lead + (rows, LANES), F32),
        pltpu.VMEM(lead + (rows, LANES), F32),
        pltpu.VMEM(lead + (rows, HEAD_DIM), F32),
    ]


def _attend(streams, i, tk, bounded):
    return (_bounded_attention if bounded else _two_pass_attention)(streams, i, tk)


def _lane_fold(x, op):
    out = x[:, :LANES]
    for c in range(1, x.shape[1] // LANES):
        out = op(out, x[:, c * LANES:(c + 1) * LANES])
    return out


def _key_rows(j, tk, n=1):
    return pl.ds(pl.multiple_of(j * tk, tk), n * tk)


def _fold_tile_groups(lo, hi, body, carry, groups=TILE_GROUPS):
    for u in groups:
        n = (hi - lo) // u

        def trip(t, c, u=u, lo=lo):
            return body(lo + t * u, u, c)

        carry = lax.fori_loop(0, n, trip, carry)
        lo = lo + n * u
    return carry


def _for_tile_groups(lo, hi, body):
    _fold_tile_groups(lo, hi, lambda j0, u, c: (body(j0, u), c)[1], 0)


def _two_pass_attention(streams, i, tk):
    for st in streams:
        _, m_ref, l_ref, acc_ref = st.scratch
        m_ref[...] = jnp.full(m_ref.shape, NEG_INF, F32)
        l_ref[...] = jnp.zeros(l_ref.shape, F32)
        acc_ref[...] = jnp.zeros(acc_ref.shape, F32)

    def score(j0, u):
        for st in streams:
            s_ref, m_ref = st.scratch[:2]
            s = _dot_nt(st.qs_ref[...], st.k_rows(_key_rows(j0, tk, u)))
            m = m_ref[...]
            for t in range(u):
                s_t = st.bias(s[:, t * tk:(t + 1) * tk], j0 + t)
                s_ref[j0 + t] = s_t
                m = jnp.maximum(m, _lane_fold(s_t, jnp.maximum))
            m_ref[...] = m

    _for_tile_groups(0, i + 1, score)

    for st in streams:
        m_ref = st.scratch[1]
        m = jnp.max(m_ref[...], axis=-1, keepdims=True)
        m_ref[...] = jnp.broadcast_to(m, m_ref.shape)

    def pv(j0, u):
        for st in streams:
            s_ref, m_ref, l_ref, acc_ref = st.scratch
            m = jnp.concatenate([m_ref[...]] * (tk // LANES), axis=1)
            p = jnp.concatenate([jnp.exp2(s_ref[j0 + t] - m) for t in range(u)], axis=1)
            l_ref[...] += _lane_fold(p, jnp.add)
            acc_ref[...] += _dot(p.astype(BF16), st.v_rows(_key_rows(j0, tk, u)))

    _for_tile_groups(0, i + 1, pv)
    return [st.scratch[3][...] / jnp.sum(st.scratch[2][...], axis=-1, keepdims=True) for st in streams]


def _bounded_attention(streams, i, tk):
    for st in streams:
        acc_ref, = st.scratch
        acc_ref[...] = jnp.zeros(acc_ref.shape, F32)

    def step(j0, u):
        for st in streams:
            acc_ref, = st.scratch
            rows = _key_rows(j0, tk, u)
            s = _dot_nt(st.qs_ref[...], st.k_rows(rows))
            p = jnp.concatenate(
                [jnp.exp2(st.bias(s[:, t * tk:(t + 1) * tk], j0 + t)).astype(BF16) for t in range(u)],
                axis=1)
            v = st.v_rows(rows)
            acc_ref[...] += _dot(p, jnp.concatenate([v, jnp.ones_like(v)], axis=1))

    _for_tile_groups(0, i + 1, step)
    return [st.scratch[0][:, :HEAD_DIM] / st.scratch[0][:, HEAD_DIM:] for st in streams]


def _diff_body(lq1_ref, lk1_ref, lq2_ref, lk2_ref, q_ref, k_ref, v_ref, tab_ref, gsub_ref, o_ref,
               qs_ref, *scratch, tq, tk, hp, lambda_init, bounded):
    i = pl.program_id(2)
    lane = lax.broadcasted_iota(jnp.int32, (tq, HEAD_DIM), 1)
    streams = []
    for hh in range(hp):
        cols = slice(hh * HEAD_DIM, (hh + 1) * HEAD_DIM)
        q = q_ref[:, cols]
        zero = jnp.zeros_like(q)
        qs_ref[hh, :tq] = jnp.where(lane < DIFF_QK_DIM, q, zero)
        qs_ref[hh, tq:] = jnp.where(lane >= DIFF_QK_DIM, q, zero)

        def bias(s, j, hh=hh):
            tab = tab_ref[hh, jnp.minimum(i - j, FAR_TILE)]
            return s + jnp.concatenate([tab, tab], axis=0)

        streams.append(_Stream(
            qs_ref.at[hh],
            lambda rows, cols=cols: k_ref[rows, cols],
            lambda rows, cols=cols: v_ref[rows, cols],
            bias, tuple(r.at[hh] for r in scratch)))

    outs = _attend(streams, i, tk, bounded)

    lam = (jnp.exp(jnp.sum(lq1_ref[...] * lk1_ref[...], axis=-1, keepdims=True))
           - jnp.exp(jnp.sum(lq2_ref[...] * lk2_ref[...], axis=-1, keepdims=True)) + lambda_init)
    for hh, out in enumerate(outs):
        o = out[:tq] - lam * out[tq:]
        o_ref[:, hh * HEAD_DIM:(hh + 1) * HEAD_DIM] = (
            _rms_rows(o, gsub_ref[...]) * (1.0 - lambda_init)).astype(o_ref.dtype)


def _diff_attn(p3, tab, lq1, lk1, lq2, lk2, g_sub, lambda_init, tq, bounded):
    B, T, _ = p3.shape
    H = N_SELF_HEADS
    hp = 6 if bounded else 2
    nq = T // tq
    w = hp * HEAD_DIM
    vec = lambda a: a.reshape(1, -1).astype(F32)
    small = lambda n: pl.BlockSpec((1, n), lambda b, h, i: (0, 0))
    return pl.pallas_call(
        functools.partial(_diff_body, tq=tq, tk=tq, hp=hp, lambda_init=lambda_init, bounded=bounded),
        grid=(B, H // hp, nq),
        in_specs=[
            small(DIFF_QK_DIM), small(DIFF_QK_DIM), small(DIFF_QK_DIM), small(DIFF_QK_DIM),
            pl.BlockSpec((None, tq, w), lambda b, h, i: (b, i, h)),
            pl.BlockSpec((None, T, w), lambda b, h, i: (b, 0, H // hp + h)),
            pl.BlockSpec((None, T, w), lambda b, h, i: (b, 0, 2 * (H // hp) + h)),
            pl.BlockSpec((hp, FAR_TILE + 1, tq, tq), lambda b, h, i: (h, 0, 0, 0)),
            small(HEAD_DIM),
        ],
        out_specs=pl.BlockSpec((None, tq, w), lambda b, h, i: (b, i, h)),
        out_shape=jax.ShapeDtypeStruct((B, T, SELF_WIDTH), BF16),
        scratch_shapes=[pltpu.VMEM((hp, 2 * tq, HEAD_DIM), BF16)]
        + _attention_scratch(2 * tq, nq, tq, bounded, lead=(hp,)),
        compiler_params=_params(("parallel", "parallel", "arbitrary")),
        name="diff_attn",
    )(vec(lq1), vec(lk1), vec(lq2), vec(lk2), p3, p3, p3, tab, vec(g_sub))


SUBLANES = 8


def _sortable(x):
    bits = pltpu.bitcast(x, jnp.int32)
    return jnp.where(bits < 0, bits ^ jnp.int32(0x7FFFFFFF), bits)


def _dsa_select(iq_ref, ik_ref, iwt_ref, key_ref, madd_ref, iqm_ref, thr_ref, need_ref, i,
                *, tq, tk, topk):
    n_tiles = i + 1
    lane = lax.broadcasted_iota(jnp.int32, (tq, LANES), 1)
    for hh in range(IDX_HEADS):
        pair = iq_ref[:, (hh // 2) * LANES:(hh // 2 + 1) * LANES]
        keep = (lane < IDX_DIM) if hh % 2 == 0 else (lane >= IDX_DIM)
        iqm_ref[hh] = jnp.where(keep, pair, jnp.zeros_like(pair))

    key_pos = lax.broadcasted_iota(jnp.int32, (tk, tq), 0)
    qry_pos = lax.broadcasted_iota(jnp.int32, (tk, tq), 1)

    def causal(j):
        return (i * tq + qry_pos) >= (j * tk + key_pos)

    def score_tiles(j0, u, c):
        for j in [j0 + t for t in range(u)]:
            ik = ik_ref[_key_rows(j, tk), :]
            sc = jnp.zeros((tk, tq), F32)
            for hh in range(IDX_HEADS):
                sc = sc + jnp.maximum(_dot_nt(ik, iqm_ref[hh]), 0.0) * iwt_ref[hh:hh + 1, :]
            key_ref[j] = _sortable(jnp.where(causal(j), sc, NEG_INF))
        return c

    _fold_tile_groups(0, n_tiles, score_tiles, 0, groups=(4, 2, 1))

    kf = float(topk)

    def count(cmp, cand):
        def body(j0, u, part):
            for t in range(u):
                hit = jnp.where(cmp(key_ref[j0 + t], cand), 1.0, 0.0)
                part = part + jnp.sum(hit.reshape(tk // SUBLANES, SUBLANES, tq), axis=0)
            return part

        part = _fold_tile_groups(0, n_tiles, body, jnp.zeros((SUBLANES, tq), F32), groups=(4, 2, 1))
        return jnp.sum(part, axis=0, keepdims=True)

    ge = lambda k, c: k >= c
    zero = jnp.zeros((1, tq), jnp.int32)
    total = (n_tiles * tk).astype(F32)
    n_pos = count(ge, zero)
    state = (jnp.where(n_pos >= kf, zero, jnp.int32(INT32_MIN)), jnp.where(n_pos >= kf, n_pos, total))

    def bit_step(b, state):
        base, n_base = state
        cand = base | lax.shift_left(jnp.int32(1), jnp.int32(30) - b)
        n = count(ge, cand)
        return jnp.where(n >= kf, cand, base), jnp.where(n >= kf, n, n_base)

    thr, n_thr = lax.fori_loop(0, 31, bit_step, state)
    thr_ref[...] = jnp.broadcast_to(thr, thr_ref.shape)
    surplus = jnp.maximum(jnp.max(n_thr - kf), 0.0)

    @pl.when(surplus == 0.0)
    def _():
        def mask_tiles(j0, u):
            for j in [j0 + t for t in range(u)]:
                take = (key_ref[j] >= thr_ref[0:1, :]) & causal(j)
                madd_ref[j] = jnp.where(take, 0.0, NEG_INF).T

        _for_tile_groups(0, n_tiles, mask_tiles)

    @pl.when(surplus != 0.0)
    def _():
        need_ref[...] = jnp.broadcast_to(kf - count(lambda k, c: k > c, thr_ref[0:1, :]), need_ref.shape)
        lower = jnp.where(lax.broadcasted_iota(jnp.int32, (tk, tk), 1)
                          < lax.broadcasted_iota(jnp.int32, (tk, tk), 0), 1.0, 0.0).astype(BF16)

        def mask_tile(j, seen):
            key = key_ref[j]
            thr = thr_ref[0:1, :]
            eq = jnp.where(key == thr, 1.0, 0.0)
            rank = seen + _dot(lower, eq.astype(BF16))
            take = (key > thr) | ((key == thr) & (rank < need_ref[0:1, :]))
            madd_ref[j] = jnp.where(take & causal(j), 0.0, NEG_INF).T
            return seen + jnp.sum(eq, axis=0, keepdims=True)

        lax.fori_loop(0, n_tiles, mask_tile, jnp.zeros((1, tq), F32))


def _dsa_body(q_ref, k_ref, v_ref, iq_ref, ik_ref, iw_ref, tab_ref, o_ref,
              key_ref, madd_ref, iqm_ref, thr_ref, need_ref, qs_ref, *scratch,
              tq, tk, topk, gp, bounded):
    i = pl.program_id(1)
    R = DSA_GROUP

    @pl.when(pl.program_id(2) == 0)
    def _():
        _dsa_select(iq_ref, ik_ref, iw_ref, key_ref, madd_ref, iqm_ref, thr_ref, need_ref, i,
                    tq=tq, tk=tk, topk=topk)

    streams = []
    for g in range(gp):
        for r in range(R):
            h = g * R + r
            qs_ref[g, r * tq:(r + 1) * tq] = q_ref[:, h * HEAD_DIM:(h + 1) * HEAD_DIM]
        cols = slice(g * HEAD_DIM, (g + 1) * HEAD_DIM)

        def bias(s, j, g=g):
            back = jnp.minimum(i - j, FAR_TILE)
            return s + jnp.concatenate([madd_ref[j] + tab_ref[g * R + r, back] for r in range(R)], axis=0)

        streams.append(_Stream(
            qs_ref.at[g],
            lambda rows, cols=cols: k_ref[rows, cols],
            lambda rows, cols=cols: v_ref[rows, cols],
            bias, tuple(ref.at[g] for ref in scratch)))

    for g, out in enumerate(_attend(streams, i, tk, bounded)):
        for r in range(R):
            h = g * R + r
            o_ref[:, h * HEAD_DIM:(h + 1) * HEAD_DIM] = out[r * tq:(r + 1) * tq].astype(o_ref.dtype)


def _dsa_attn(p3, ik2, iw, tab, tq, bounded):
    B, T, _ = p3.shape
    H, R, G = N_SELF_HEADS, DSA_GROUP, DSA_KV_HEADS
    topk = min(TOPK_MAX, T // 4)
    nq = T // tq
    iq_w = IDX_HEADS * IDX_DIM
    gp = G if bounded else 1
    kw = gp * HEAD_DIM
    k0 = H * HEAD_DIM // kw
    iq0 = (H + G) * HEAD_DIM // iq_w
    v0 = ((H + G) * HEAD_DIM + iq_w) // kw
    assert (H + G) * HEAD_DIM % iq_w == 0
    return pl.pallas_call(
        functools.partial(_dsa_body, tq=tq, tk=tq, topk=topk, gp=gp, bounded=bounded),
        grid=(B, nq, G // gp),
        in_specs=[
            pl.BlockSpec((None, tq, gp * R * HEAD_DIM), lambda b, i, g: (b, i, g)),
            pl.BlockSpec((None, T, kw), lambda b, i, g: (b, 0, k0 + g)),
            pl.BlockSpec((None, T, kw), lambda b, i, g: (b, 0, v0 + g)),
            pl.BlockSpec((None, tq, iq_w), lambda b, i, g: (b, i, iq0)),
            pl.BlockSpec((None, T, LANES), lambda b, i, g: (b, 0, 0)),
            pl.BlockSpec((None, IDX_HEADS, tq), lambda b, i, g: (b, 0, i)),
            pl.BlockSpec((gp * R, FAR_TILE + 1, tq, tq), lambda b, i, g: (g, 0, 0, 0)),
        ],
        out_specs=pl.BlockSpec((None, tq, gp * R * HEAD_DIM), lambda b, i, g: (b, i, g)),
        out_shape=jax.ShapeDtypeStruct((B, T, SELF_WIDTH), BF16),
        scratch_shapes=[
            pltpu.VMEM((nq, tq, tq), jnp.int32),
            pltpu.VMEM((nq, tq, tq), F32),
            pltpu.VMEM((IDX_HEADS, tq, LANES), BF16),
            pltpu.VMEM((SUBLANES, tq), jnp.int32),
            pltpu.VMEM((SUBLANES, tq), F32),
            pltpu.VMEM((gp, R * tq, HEAD_DIM), BF16),
        ] + _attention_scratch(R * tq, nq, tq, bounded, lead=(gp,)),
        compiler_params=_params(("parallel", "arbitrary", "arbitrary")),
        name="dsa_attn",
    )(p3, p3, p3, p3, ik2, iw, tab)


def _out_body(ys_ref, qm_ref, km_ref, vm_ref, ws_ref, wm_ref, x_ref, o_ref):
    acc = x_ref[...] + _dot(ys_ref[...], ws_ref[...])
    heads = []
    for h in range(N_MEM_HEADS):
        sl = slice(h * HEAD_DIM, (h + 1) * HEAD_DIM)
        s = _dot_nt(qm_ref[:, sl], km_ref[:, sl])
        p = jnp.exp2(s - jnp.max(s, axis=-1, keepdims=True))
        o = _dot(p.astype(BF16), vm_ref[:, sl]) / jnp.sum(p, axis=-1, keepdims=True)
        heads.append(o.astype(BF16))
    o_ref[...] = acc + _dot(jnp.concatenate(heads, axis=1), wm_ref[...])


def _out_proj(ys, p2, qm_block, kv3, wo, layer, x2):
    M, D = x2.shape
    B, N, _ = kv3.shape
    tm = min(512, M // B)
    per_batch = M // B // tm
    return pl.pallas_call(
        _out_body,
        grid=(M // tm,),
        in_specs=[
            pl.BlockSpec((tm, SELF_WIDTH), lambda i: (i, 0)),
            pl.BlockSpec((tm, MEM_WIDTH), lambda i: (i, qm_block)),
            pl.BlockSpec((None, N, MEM_WIDTH), lambda i: (i // per_batch, 0, 0)),
            pl.BlockSpec((None, N, MEM_WIDTH), lambda i: (i // per_batch, 0, 1)),
            pl.BlockSpec((None, SELF_WIDTH, D), lambda i: (layer, 0, 0)),
            pl.BlockSpec((None, MEM_WIDTH, D), lambda i: (layer, SELF_WIDTH // MEM_WIDTH, 0)),
            pl.BlockSpec((tm, D), lambda i: (i, 0)),
        ],
        out_specs=pl.BlockSpec((tm, D), lambda i: (i, 0)),
        out_shape=jax.ShapeDtypeStruct((M, D), F32),
        compiler_params=_params(("parallel",)),
        name="out_proj",
    )(ys, p2, kv3, kv3, wo, wo, x2)


def _scores_bounded(gq, gk, dim, scale, rel_bias):
    qk = dim * jnp.max(jnp.abs(gq)) * jnp.max(jnp.abs(gk)) * (scale * LOG2E * 1.02)
    bias = jnp.max(jnp.abs(rel_bias - rel_bias[NUM_BUCKETS - 1:])) * LOG2E
    return qk + bias <= MAX_EXP2_ARG


def kernel(x, mem, rel_bias, ffn1_g, ffn1_w_gate, ffn1_w_up, ffn1_w_down, ffn2_g, ffn2_w_gate,
           ffn2_w_up, ffn2_w_down, mix_g, mem_g, mem_w_kv, mem_gq, mem_gk, w_out, a_w_in, a_gq, a_gk,
           a_lam_q1, a_lam_k1, a_lam_q2, a_lam_k2, a_g_sub, b_w_in, b_gq, b_gk):
    B, T, D = x.shape
    N = mem.shape[1]
    depth = ffn1_g.shape[0]
    M = B * T
    tq = min(256, T)
    head_scale = HEAD_DIM ** -0.5 * LOG2E
    ones = lambda n: jnp.ones((n,), F32)
    bf = lambda w: w.astype(BF16)

    x2 = x.reshape(M, D)
    mem2 = mem.reshape(B * N, D)
    tab = _bias_tiles(rel_bias, tq, tq)
    ffn_w = [(bf(ffn1_w_gate), bf(ffn1_w_up), bf(0.5 * ffn1_w_down)),
             (bf(ffn2_w_gate), bf(ffn2_w_up), bf(0.5 * ffn2_w_down))]
    ffn_g = [ffn1_g, ffn2_g]
    wo, w_kv, w_a = bf(w_out), bf(mem_w_kv), bf(a_w_in)

    for i in range(depth):
        x2 = _ffn(x2, ffn_g[0][i], *ffn_w[0], i)

        kv_gain = jnp.concatenate([jnp.tile(mem_gk[i], N_MEM_HEADS), ones(MEM_WIDTH)])
        kv3 = _proj(mem2, mem_g[i], w_kv, i, [2, 0], kv_gain, BF16, MEM_WIDTH)
        kv3 = kv3.reshape(B, N, 2 * MEM_WIDTH)
        qm_gain = jnp.tile(mem_gq[i], N_MEM_HEADS) * head_scale
        j = i // 2

        if i % 2 == 0:
            n_grp = SELF_WIDTH // DIFF_QK_DIM
            gain = jnp.concatenate([jnp.tile(a_gq[j], n_grp) * (DIFF_QK_DIM ** -0.5 * LOG2E),
                                    jnp.tile(a_gk[j], n_grp), ones(SELF_WIDTH), qm_gain])
            kinds = [1] * 6 + [0] * 3 + [2]
            p3 = _proj(x2, mix_g[i], w_a, j, kinds, gain, BF16, 512).reshape(B, T, -1)
            lambda_init = 0.8 - 0.6 * math.exp(-0.3 * i)
            attn = functools.partial(_diff_attn, p3, tab, a_lam_q1[j], a_lam_k1[j], a_lam_q2[j],
                                     a_lam_k2[j], a_g_sub[j], lambda_init, tq)
            bounded = _scores_bounded(a_gq[j], a_gk[j], DIFF_QK_DIM, DIFF_QK_DIM ** -0.5, rel_bias)
            qm_block = 3 * SELF_WIDTH // MEM_WIDTH
        else:
            w = b_w_in[j]
            kvw = DSA_KV_HEADS * HEAD_DIM
            iqw = IDX_HEADS * IDX_DIM
            c_q, c_k, c_v, c_iq = 0, SELF_WIDTH, SELF_WIDTH + kvw, SELF_WIDTH + 2 * kvw
            c_ik = c_iq + iqw
            c_iw = c_ik + IDX_DIM
            c_qm = c_iw + IDX_HEADS
            w_main = jnp.concatenate([w[:, c_q:c_k], w[:, c_k:c_v], w[:, c_iq:c_ik], w[:, c_v:c_iq],
                                      w[:, c_qm:]], axis=1)
            gain = jnp.concatenate([jnp.tile(b_gq[j], N_SELF_HEADS) * head_scale,
                                    jnp.tile(b_gk[j], DSA_KV_HEADS), ones(iqw), ones(kvw), qm_gain])
            kinds = [2, 2, 2, 2, 0, 0, 0, 2]
            p3 = _proj(x2, mix_g[i], bf(w_main)[None], 0, kinds, gain, BF16, 512).reshape(B, T, -1)
            pad = LANES - IDX_DIM - IDX_HEADS
            w_idx = jnp.concatenate([w[:, c_ik:c_qm], jnp.zeros((D, pad), F32)], axis=1)
            idx_gain = jnp.concatenate([ones(IDX_DIM),
                                        jnp.full((IDX_HEADS,), (IDX_DIM * IDX_HEADS) ** -0.5, F32),
                                        jnp.zeros((pad,), F32)])
            ikw = _proj(x2, mix_g[i], bf(w_idx)[None], 0, [0], idx_gain, F32, LANES).reshape(B, T, LANES)
            ik = ikw[:, :, :IDX_DIM].astype(BF16)
            ik2 = jnp.concatenate([ik, ik], axis=-1)
            iw = jnp.swapaxes(ikw[:, :, IDX_DIM:IDX_DIM + IDX_HEADS], 1, 2)
            attn = functools.partial(_dsa_attn, p3, ik2, iw, tab, tq)
            bounded = _scores_bounded(b_gq[j], b_gk[j], HEAD_DIM, HEAD_DIM ** -0.5, rel_bias)
            qm_block = (p3.shape[-1] - MEM_WIDTH) // MEM_WIDTH

        y_self = lax.cond(bounded, lambda: attn(True), lambda: attn(False))
        x2 = _out_proj(y_self.reshape(M, SELF_WIDTH), p3.reshape(M, -1), qm_block, kv3, wo, i, x2)

        x2 = _ffn(x2, ffn_g[1][i], *ffn_w[1], i)

    return x2.reshape(B, T, D)
```

```python
import functools
import math
from typing import Any, Callable, NamedTuple

import jax
import jax.numpy as jnp
from jax import lax
from jax.experimental import pallas as pl
from jax.experimental.pallas import tpu as pltpu

HEAD_DIM = 128
N_SELF_HEADS = 12
N_MEM_HEADS = 4
SELF_WIDTH = N_SELF_HEADS * HEAD_DIM
MEM_WIDTH = N_MEM_HEADS * HEAD_DIM
DIFF_QK_DIM = HEAD_DIM // 2
DSA_KV_HEADS = 4
DSA_GROUP = N_SELF_HEADS // DSA_KV_HEADS
IDX_HEADS = 16
IDX_DIM = 64
TOPK_MAX = 256
NUM_BUCKETS = 32
MAX_DISTANCE = 128
EPS = 1e-6
NEG_INF = -1e30
INT32_MIN = -(2 ** 31)
LOG2E = math.log2(math.e)

FAR_TILE = 2

LANES = 128
VMEM_LIMIT = 56 * 1024 * 1024

F32 = jnp.float32
BF16 = jnp.bfloat16


def _dot(a, b):
    return jnp.dot(a, b, preferred_element_type=F32)


def _dot_nt(a, b):
    return lax.dot_general(a, b, (((1,), (1,)), ((), ())), preferred_element_type=F32)


def _params(semantics):
    return pltpu.CompilerParams(dimension_semantics=semantics, vmem_limit_bytes=VMEM_LIMIT)


def _rms_rows(x, g):
    ms = jnp.mean(x * x, axis=-1, keepdims=True)
    return x * lax.rsqrt(ms + EPS) * g


def _ffn_body(x_ref, g_ref, wg_ref, wu_ref, wd_ref, o_ref, h_ref):
    @pl.when(pl.program_id(1) == 0)
    def _():
        x = x_ref[...]
        h_ref[...] = _rms_rows(x, g_ref[...]).astype(BF16)
        o_ref[...] = x

    h = h_ref[...]
    gate = _dot(h, wg_ref[...])
    up = _dot(h, wu_ref[...])
    act = (gate * jax.nn.sigmoid(gate) * up).astype(BF16)
    o_ref[...] += _dot(act, wd_ref[...])


def _ffn(x2, g, wg, wu, wd_half, layer):
    M, D = x2.shape
    F = wg.shape[2]
    tm = min(1024, M)
    tf = min(512, F)
    return pl.pallas_call(
        _ffn_body,
        grid=(M // tm, F // tf),
        in_specs=[
            pl.BlockSpec((tm, D), lambda i, f: (i, 0)),
            pl.BlockSpec((1, D), lambda i, f: (0, 0)),
            pl.BlockSpec((None, D, tf), lambda i, f: (layer, 0, f)),
            pl.BlockSpec((None, D, tf), lambda i, f: (layer, 0, f)),
            pl.BlockSpec((None, tf, D), lambda i, f: (layer, f, 0)),
        ],
        out_specs=pl.BlockSpec((tm, D), lambda i, f: (i, 0)),
        out_shape=jax.ShapeDtypeStruct((M, D), F32),
        scratch_shapes=[pltpu.VMEM((tm, D), BF16)],
        compiler_params=_params(("parallel", "arbitrary")),
        name="ffn",
    )(x2, g.reshape(1, D), wg, wu, wd_half)


def _proj_body(types_ref, x_ref, g_ref, w_ref, gmat_ref, gain_ref, *rest, tn, kw, side):
    if side:
        ws_ref, gs_ref, o_ref, os_ref, h_ref = rest
    else:
        o_ref, h_ref = rest
    j = pl.program_id(1)

    @pl.when(j == 0)
    def _():
        h_ref[...] = _rms_rows(x_ref[...], g_ref[...]).astype(BF16)

    y = _dot(h_ref[...], w_ref[...])
    for sec in range(tn // kw):
        kind = types_ref[j * (tn // kw) + sec]

        @pl.when(kind == 0)
        def _(sec=sec):
            cols = slice(sec * kw, (sec + 1) * kw)
            o_ref[:, cols] = (y[:, cols] * gain_ref[:, cols]).astype(o_ref.dtype)

        @pl.when(kind != 0)
        def _(sec=sec, kind=kind):
            gmat = gmat_ref[kind]
            for c in range(sec * kw // LANES, (sec + 1) * kw // LANES):
                sl = slice(c * LANES, (c + 1) * LANES)
                yc = y[:, sl]
                ms = _dot((yc * yc).astype(BF16), gmat)
                o_ref[:, sl] = (yc * lax.rsqrt(ms + EPS) * gain_ref[:, sl]).astype(o_ref.dtype)

    if side:
        @pl.when(j == pl.num_programs(1) - 1)
        def _():
            os_ref[...] = _dot(h_ref[...], ws_ref[...]) * gs_ref[...]


def _group_mats():
    r = jnp.arange(LANES)
    same64 = (r[:, None] // DIFF_QK_DIM) == (r[None, :] // DIFF_QK_DIM)
    return jnp.stack([
        jnp.zeros((LANES, LANES), F32),
        jnp.where(same64, 1.0 / DIFF_QK_DIM, 0.0),
        jnp.full((LANES, LANES), 1.0 / HEAD_DIM, F32),
    ]).astype(BF16)


def _proj(x2, g, w, layer, kinds, gain, out_dtype, kw, side=None):
    M, D = x2.shape
    N = w.shape[2]
    tm = min(1024, M)
    tn = 2 * kw if N % (2 * kw) == 0 else kw
    kinds = jnp.asarray(kinds, jnp.int32)
    grid_spec = pltpu.PrefetchScalarGridSpec(
        num_scalar_prefetch=1,
        grid=(M // tm, N // tn),
        in_specs=[
            pl.BlockSpec((tm, D), lambda i, j, t: (i, 0)),
            pl.BlockSpec((1, D), lambda i, j, t: (0, 0)),
            pl.BlockSpec((None, D, tn), lambda i, j, t: (layer, 0, j)),
            pl.BlockSpec((3, LANES, LANES), lambda i, j, t: (0, 0, 0)),
            pl.BlockSpec((1, tn), lambda i, j, t: (0, j)),
        ] + ([pl.BlockSpec((D, LANES), lambda i, j, t: (0, 0)),
              pl.BlockSpec((1, LANES), lambda i, j, t: (0, 0))] if side else []),
        out_specs=[pl.BlockSpec((tm, tn), lambda i, j, t: (i, j))]
        + ([pl.BlockSpec((tm, LANES), lambda i, j, t: (i, 0))] if side else []),
        scratch_shapes=[pltpu.VMEM((tm, D), BF16)],
    )
    outs = pl.pallas_call(
        functools.partial(_proj_body, tn=tn, kw=kw, side=bool(side)),
        grid_spec=grid_spec,
        out_shape=[jax.ShapeDtypeStruct((M, N), out_dtype)]
        + ([jax.ShapeDtypeStruct((M, LANES), F32)] if side else []),
        compiler_params=_params(("parallel", "arbitrary")),
        name="proj",
    )(kinds, x2, g.reshape(1, D), w, _group_mats(), gain.reshape(1, N),
      *((side[0], side[1].reshape(1, LANES)) if side else ()))
    return outs if side else outs[0]


def _bias_body(rb_ref, o_ref, *, tq, tk):
    h = pl.program_id(0)
    r = lax.broadcasted_iota(jnp.int32, (tq, tk), 0)
    c = lax.broadcasted_iota(jnp.int32, (tq, tk), 1)
    max_exact = NUM_BUCKETS // 2
    far = rb_ref[NUM_BUCKETS - 1, h]
    for off in range(FAR_TILE + 1):
        dist = r - c + off * tk
        n = jnp.maximum(dist, 0)
        nf = jnp.maximum(n, 1).astype(F32)
        large = max_exact + (jnp.log(nf / max_exact) / math.log(MAX_DISTANCE / max_exact)
                             * (NUM_BUCKETS - max_exact)).astype(jnp.int32)
        large = jnp.minimum(large, NUM_BUCKETS - 1)
        bucket = jnp.where(n < max_exact, n, large)
        val = jnp.zeros((tq, tk), F32)
        for b in range(NUM_BUCKETS):
            val = jnp.where(bucket == b, rb_ref[b, h], val)
        o_ref[off] = jnp.where(dist >= 0, (val - far) * LOG2E, NEG_INF)


def _bias_tiles(rel_bias, tq, tk):
    assert tq == tk and tk >= MAX_DISTANCE
    return pl.pallas_call(
        functools.partial(_bias_body, tq=tq, tk=tk),
        grid=(N_SELF_HEADS,),
        in_specs=[pl.BlockSpec(memory_space=pltpu.SMEM)],
        out_specs=pl.BlockSpec((None, FAR_TILE + 1, tq, tk), lambda h: (h, 0, 0, 0)),
        out_shape=jax.ShapeDtypeStruct((N_SELF_HEADS, FAR_TILE + 1, tq, tk), F32),
        compiler_params=_params(("arbitrary",)),
        name="bias_tiles",
    )(rel_bias)


class _Stream(NamedTuple):
    qs_ref: Any
    k_rows: Callable
    v_rows: Callable
    bias: Callable
    scratch: tuple


TILE_GROUPS = (8, 4, 2, 1)
MAX_EXP2_ARG = 60.0


def _attention_scratch(rows, n_tiles, tk, bounded, lead=()):
    if bounded:
        return [pltpu.VMEM(lead + (rows, 2 * HEAD_DIM), F32)]
    return [
        pltpu.VMEM(lead + (n_tiles, rows, tk), F32),
        pltpu.VMEM(lead + (rows, LANES), F32),
        pltpu.VMEM(lead + (rows, LANES), F32),
        pltpu.VMEM(lead + (rows, HEAD_DIM), F32),
    ]


def _attend(streams, i, tk, bounded):
    return (_bounded_attention if bounded else _two_pass_attention)(streams, i, tk)


def _lane_fold(x, op):
    out = x[:, :LANES]
    for c in range(1, x.shape[1] // LANES):
        out = op(out, x[:, c * LANES:(c + 1) * LANES])
    return out


def _key_rows(j, tk, n=1):
    return pl.ds(pl.multiple_of(j * tk, tk), n * tk)


def _fold_tile_groups(lo, hi, body, carry, groups=TILE_GROUPS):
    for u in groups:
        n = (hi - lo) // u

        def trip(t, c, u=u, lo=lo):
            return body(lo + t * u, u, c)

        carry = lax.fori_loop(0, n, trip, carry)
        lo = lo + n * u
    return carry


def _for_tile_groups(lo, hi, body):
    _fold_tile_groups(lo, hi, lambda j0, u, c: (body(j0, u), c)[1], 0)


def _two_pass_attention(streams, i, tk):
    for st in streams:
        _, m_ref, l_ref, acc_ref = st.scratch
        m_ref[...] = jnp.full(m_ref.shape, NEG_INF, F32)
        l_ref[...] = jnp.zeros(l_ref.shape, F32)
        acc_ref[...] = jnp.zeros(acc_ref.shape, F32)

    def score(j0, u):
        for st in streams:
            s_ref, m_ref = st.scratch[:2]
            s = _dot_nt(st.qs_ref[...], st.k_rows(_key_rows(j0, tk, u)))
            m = m_ref[...]
            for t in range(u):
                s_t = st.bias(s[:, t * tk:(t + 1) * tk], j0 + t)
                s_ref[j0 + t] = s_t
                m = jnp.maximum(m, _lane_fold(s_t, jnp.maximum))
            m_ref[...] = m

    _for_tile_groups(0, i + 1, score)

    for st in streams:
        m_ref = st.scratch[1]
        m = jnp.max(m_ref[...], axis=-1, keepdims=True)
        m_ref[...] = jnp.broadcast_to(m, m_ref.shape)

    def pv(j0, u):
        for st in streams:
            s_ref, m_ref, l_ref, acc_ref = st.scratch
            m = jnp.concatenate([m_ref[...]] * (tk // LANES), axis=1)
            p = jnp.concatenate([jnp.exp2(s_ref[j0 + t] - m) for t in range(u)], axis=1)
            l_ref[...] += _lane_fold(p, jnp.add)
            acc_ref[...] += _dot(p.astype(BF16), st.v_rows(_key_rows(j0, tk, u)))

    _for_tile_groups(0, i + 1, pv)
    return [st.scratch[3][...] / jnp.sum(st.scratch[2][...], axis=-1, keepdims=True) for st in streams]


def _bounded_attention(streams, i, tk):
    for st in streams:
        acc_ref, = st.scratch
        acc_ref[...] = jnp.zeros(acc_ref.shape, F32)

    def step(j0, u):
        for st in streams:
            acc_ref, = st.scratch
            rows = _key_rows(j0, tk, u)
            s = _dot_nt(st.qs_ref[...], st.k_rows(rows))
            p = jnp.concatenate(
                [jnp.exp2(st.bias(s[:, t * tk:(t + 1) * tk], j0 + t)).astype(BF16) for t in range(u)],
                axis=1)
            v = st.v_rows(rows)
            acc_ref[...] += _dot(p, jnp.concatenate([v, jnp.ones_like(v)], axis=1))

    _for_tile_groups(0, i + 1, step)
    return [st.scratch[0][:, :HEAD_DIM] / st.scratch[0][:, HEAD_DIM:] for st in streams]


def _diff_body(lq1_ref, lk1_ref, lq2_ref, lk2_ref, q_ref, k_ref, v_ref, tab_ref, gsub_ref, o_ref,
               qs_ref, *scratch, tq, tk, hp, lambda_init, bounded):
    i = pl.program_id(2)
    lane = lax.broadcasted_iota(jnp.int32, (tq, HEAD_DIM), 1)
    streams = []
    for hh in range(hp):
        cols = slice(hh * HEAD_DIM, (hh + 1) * HEAD_DIM)
        q = q_ref[:, cols]
        zero = jnp.zeros_like(q)
        qs_ref[hh, :tq] = jnp.where(lane < DIFF_QK_DIM, q, zero)
        qs_ref[hh, tq:] = jnp.where(lane >= DIFF_QK_DIM, q, zero)

        def bias(s, j, hh=hh):
            tab = tab_ref[hh, jnp.minimum(i - j, FAR_TILE)]
            return s + jnp.concatenate([tab, tab], axis=0)

        streams.append(_Stream(
            qs_ref.at[hh],
            lambda rows, cols=cols: k_ref[rows, cols],
            lambda rows, cols=cols: v_ref[rows, cols],
            bias, tuple(r.at[hh] for r in scratch)))

    outs = _attend(streams, i, tk, bounded)

    lam = (jnp.exp(jnp.sum(lq1_ref[...] * lk1_ref[...], axis=-1, keepdims=True))
           - jnp.exp(jnp.sum(lq2_ref[...] * lk2_ref[...], axis=-1, keepdims=True)) + lambda_init)
    for hh, out in enumerate(outs):
        o = out[:tq] - lam * out[tq:]
        o_ref[:, hh * HEAD_DIM:(hh + 1) * HEAD_DIM] = (
            _rms_rows(o, gsub_ref[...]) * (1.0 - lambda_init)).astype(o_ref.dtype)


def _diff_attn(p3, tab, lq1, lk1, lq2, lk2, g_sub, lambda_init, tq, bounded):
    B, T, _ = p3.shape
    H = N_SELF_HEADS
    hp = 6 if bounded else 2
    nq = T // tq
    w = hp * HEAD_DIM
    vec = lambda a: a.reshape(1, -1).astype(F32)
    small = lambda n: pl.BlockSpec((1, n), lambda b, h, i: (0, 0))
    return pl.pallas_call(
        functools.partial(_diff_body, tq=tq, tk=tq, hp=hp, lambda_init=lambda_init, bounded=bounded),
        grid=(B, H // hp, nq),
        in_specs=[
            small(DIFF_QK_DIM), small(DIFF_QK_DIM), small(DIFF_QK_DIM), small(DIFF_QK_DIM),
            pl.BlockSpec((None, tq, w), lambda b, h, i: (b, i, h)),
            pl.BlockSpec((None, T, w), lambda b, h, i: (b, 0, H // hp + h)),
            pl.BlockSpec((None, T, w), lambda b, h, i: (b, 0, 2 * (H // hp) + h)),
            pl.BlockSpec((hp, FAR_TILE + 1, tq, tq), lambda b, h, i: (h, 0, 0, 0)),
            small(HEAD_DIM),
        ],
        out_specs=pl.BlockSpec((None, tq, w), lambda b, h, i: (b, i, h)),
        out_shape=jax.ShapeDtypeStruct((B, T, SELF_WIDTH), BF16),
        scratch_shapes=[pltpu.VMEM((hp, 2 * tq, HEAD_DIM), BF16)]
        + _attention_scratch(2 * tq, nq, tq, bounded, lead=(hp,)),
        compiler_params=_params(("parallel", "parallel", "arbitrary")),
        name="diff_attn",
    )(vec(lq1), vec(lk1), vec(lq2), vec(lk2), p3, p3, p3, tab, vec(g_sub))


SUBLANES = 8


def _sortable(x):
    bits = pltpu.bitcast(x, jnp.int32)
    return jnp.where(bits < 0, bits ^ jnp.int32(0x7FFFFFFF), bits)


def _dsa_select(iq_ref, ik_ref, iwt_ref, key_ref, madd_ref, iqm_ref, thr_ref, need_ref, i,
                *, tq, tk, topk):
    n_tiles = i + 1
    lane = lax.broadcasted_iota(jnp.int32, (tq, LANES), 1)
    for hh in range(IDX_HEADS):
        pair = iq_ref[:, (hh // 2) * LANES:(hh // 2 + 1) * LANES]
        keep = (lane < IDX_DIM) if hh % 2 == 0 else (lane >= IDX_DIM)
        iqm_ref[hh] = jnp.where(keep, pair, jnp.zeros_like(pair))

    key_pos = lax.broadcasted_iota(jnp.int32, (tk, tq), 0)
    qry_pos = lax.broadcasted_iota(jnp.int32, (tk, tq), 1)

    def causal(j):
        return (i * tq + qry_pos) >= (j * tk + key_pos)

    def score_tiles(j0, u, c):
        for j in [j0 + t for t in range(u)]:
            ik = ik_ref[_key_rows(j, tk), :]
            sc = jnp.zeros((tk, tq), F32)
            for hh in range(IDX_HEADS):
                sc = sc + jnp.maximum(_dot_nt(ik, iqm_ref[hh]), 0.0) * iwt_ref[hh:hh + 1, :]
            key_ref[j] = _sortable(jnp.where(causal(j), sc, NEG_INF))
        return c

    _fold_tile_groups(0, n_tiles, score_tiles, 0, groups=(4, 2, 1))

    kf = float(topk)

    def count(cmp, cand):
        def body(j0, u, part):
            for t in range(u):
                hit = jnp.where(cmp(key_ref[j0 + t], cand), 1.0, 0.0)
                part = part + jnp.sum(hit.reshape(tk // SUBLANES, SUBLANES, tq), axis=0)
            return part

        part = _fold_tile_groups(0, n_tiles, body, jnp.zeros((SUBLANES, tq), F32), groups=(4, 2, 1))
        return jnp.sum(part, axis=0, keepdims=True)

    ge = lambda k, c: k >= c
    zero = jnp.zeros((1, tq), jnp.int32)
    total = (n_tiles * tk).astype(F32)
    n_pos = count(ge, zero)
    state = (jnp.where(n_pos >= kf, zero, jnp.int32(INT32_MIN)), jnp.where(n_pos >= kf, n_pos, total))

    def bit_step(b, state):
        base, n_base = state
        cand = base | lax.shift_left(jnp.int32(1), jnp.int32(30) - b)
        n = count(ge, cand)
        return jnp.where(n >= kf, cand, base), jnp.where(n >= kf, n, n_base)

    thr, n_thr = lax.fori_loop(0, 31, bit_step, state)
    thr_ref[...] = jnp.broadcast_to(thr, thr_ref.shape)
    surplus = jnp.maximum(jnp.max(n_thr - kf), 0.0)

    @pl.when(surplus == 0.0)
    def _():
        def mask_tiles(j0, u):
            for j in [j0 + t for t in range(u)]:
                take = (key_ref[j] >= thr_ref[0:1, :]) & causal(j)
                madd_ref[j] = jnp.where(take, 0.0, NEG_INF).T

        _for_tile_groups(0, n_tiles, mask_tiles)

    @pl.when(surplus != 0.0)
    def _():
        need_ref[...] = jnp.broadcast_to(kf - count(lambda k, c: k > c, thr_ref[0:1, :]), need_ref.shape)
        lower = jnp.where(lax.broadcasted_iota(jnp.int32, (tk, tk), 1)
                          < lax.broadcasted_iota(jnp.int32, (tk, tk), 0), 1.0, 0.0).astype(BF16)

        def mask_tile(j, seen):
            key = key_ref[j]
            thr = thr_ref[0:1, :]
            eq = jnp.where(key == thr, 1.0, 0.0)
            rank = seen + _dot(lower, eq.astype(BF16))
            take = (key > thr) | ((key == thr) & (rank < need_ref[0:1, :]))
            madd_ref[j] = jnp.where(take & causal(j), 0.0, NEG_INF).T
            return seen + jnp.sum(eq, axis=0, keepdims=True)

        lax.fori_loop(0, n_tiles, mask_tile, jnp.zeros((1, tq), F32))


def _dsa_body(q_ref, k_ref, v_ref, iq_ref, ik_ref, iw_ref, tab_ref, o_ref,
              key_ref, madd_ref, iqm_ref, thr_ref, need_ref, qs_ref, *scratch,
              tq, tk, topk, gp, bounded):
    i = pl.program_id(1)
    R = DSA_GROUP

    @pl.when(pl.program_id(2) == 0)
    def _():
        _dsa_select(iq_ref, ik_ref, iw_ref, key_ref, madd_ref, iqm_ref, thr_ref, need_ref, i,
                    tq=tq, tk=tk, topk=topk)

    streams = []
    for g in range(gp):
        for r in range(R):
            h = g * R + r
            qs_ref[g, r * tq:(r + 1) * tq] = q_ref[:, h * HEAD_DIM:(h + 1) * HEAD_DIM]
        cols = slice(g * HEAD_DIM, (g + 1) * HEAD_DIM)

        def bias(s, j, g=g):
            back = jnp.minimum(i - j, FAR_TILE)
            return s + jnp.concatenate([madd_ref[j] + tab_ref[g * R + r, back] for r in range(R)], axis=0)

        streams.append(_Stream(
            qs_ref.at[g],
            lambda rows, cols=cols: k_ref[rows, cols],
            lambda rows, cols=cols: v_ref[rows, cols],
            bias, tuple(ref.at[g] for ref in scratch)))

    for g, out in enumerate(_attend(streams, i, tk, bounded)):
        for r in range(R):
            h = g * R + r
            o_ref[:, h * HEAD_DIM:(h + 1) * HEAD_DIM] = out[r * tq:(r + 1) * tq].astype(o_ref.dtype)


def _dsa_attn(p3, ik2, iw, tab, tq, bounded):
    B, T, _ = p3.shape
    H, R, G = N_SELF_HEADS, DSA_GROUP, DSA_KV_HEADS
    topk = min(TOPK_MAX, T // 4)
    nq = T // tq
    iq_w = IDX_HEADS * IDX_DIM
    gp = G if bounded else 1
    kw = gp * HEAD_DIM
    k0 = H * HEAD_DIM // kw
    iq0 = (H + G) * HEAD_DIM // iq_w
    v0 = ((H + G) * HEAD_DIM + iq_w) // kw
    assert (H + G) * HEAD_DIM % iq_w == 0
    return pl.pallas_call(
        functools.partial(_dsa_body, tq=tq, tk=tq, topk=topk, gp=gp, bounded=bounded),
        grid=(B, nq, G // gp),
        in_specs=[
            pl.BlockSpec((None, tq, gp * R * HEAD_DIM), lambda b, i, g: (b, i, g)),
            pl.BlockSpec((None, T, kw), lambda b, i, g: (b, 0, k0 + g)),
            pl.BlockSpec((None, T, kw), lambda b, i, g: (b, 0, v0 + g)),
            pl.BlockSpec((None, tq, iq_w), lambda b, i, g: (b, i, iq0)),
            pl.BlockSpec((None, T, LANES), lambda b, i, g: (b, 0, 0)),
            pl.BlockSpec((None, IDX_HEADS, tq), lambda b, i, g: (b, 0, i)),
            pl.BlockSpec((gp * R, FAR_TILE + 1, tq, tq), lambda b, i, g: (g, 0, 0, 0)),
        ],
        out_specs=pl.BlockSpec((None, tq, gp * R * HEAD_DIM), lambda b, i, g: (b, i, g)),
        out_shape=jax.ShapeDtypeStruct((B, T, SELF_WIDTH), BF16),
        scratch_shapes=[
            pltpu.VMEM((nq, tq, tq), jnp.int32),
            pltpu.VMEM((nq, tq, tq), F32),
            pltpu.VMEM((IDX_HEADS, tq, LANES), BF16),
            pltpu.VMEM((SUBLANES, tq), jnp.int32),
            pltpu.VMEM((SUBLANES, tq), F32),
            pltpu.VMEM((gp, R * tq, HEAD_DIM), BF16),
        ] + _attention_scratch(R * tq, nq, tq, bounded, lead=(gp,)),
        compiler_params=_params(("parallel", "arbitrary", "arbitrary")),
        name="dsa_attn",
    )(p3, p3, p3, p3, ik2, iw, tab)


def _out_body(ys_ref, qm_ref, km_ref, vm_ref, ws_ref, wm_ref, x_ref, o_ref):
    acc = x_ref[...] + _dot(ys_ref[...], ws_ref[...])
    heads = []
    for h in range(N_MEM_HEADS):
        sl = slice(h * HEAD_DIM, (h + 1) * HEAD_DIM)
        s = _dot_nt(qm_ref[:, sl], km_ref[:, sl])
        p = jnp.exp2(s - jnp.max(s, axis=-1, keepdims=True))
        o = _dot(p.astype(BF16), vm_ref[:, sl]) / jnp.sum(p, axis=-1, keepdims=True)
        heads.append(o.astype(BF16))
    o_ref[...] = acc + _dot(jnp.concatenate(heads, axis=1), wm_ref[...])


def _out_proj(ys, p2, qm_block, kv3, wo, layer, x2):
    M, D = x2.shape
    B, N, _ = kv3.shape
    tm = min(512, M // B)
    per_batch = M // B // tm
    return pl.pallas_call(
        _out_body,
        grid=(M // tm,),
        in_specs=[
            pl.BlockSpec((tm, SELF_WIDTH), lambda i: (i, 0)),
            pl.BlockSpec((tm, MEM_WIDTH), lambda i: (i, qm_block)),
            pl.BlockSpec((None, N, MEM_WIDTH), lambda i: (i // per_batch, 0, 0)),
            pl.BlockSpec((None, N, MEM_WIDTH), lambda i: (i // per_batch, 0, 1)),
            pl.BlockSpec((None, SELF_WIDTH, D), lambda i: (layer, 0, 0)),
            pl.BlockSpec((None, MEM_WIDTH, D), lambda i: (layer, SELF_WIDTH // MEM_WIDTH, 0)),
            pl.BlockSpec((tm, D), lambda i: (i, 0)),
        ],
        out_specs=pl.BlockSpec((tm, D), lambda i: (i, 0)),
        out_shape=jax.ShapeDtypeStruct((M, D), F32),
        compiler_params=_params(("parallel",)),
        name="out_proj",
    )(ys, p2, kv3, kv3, wo, wo, x2)


def _scores_bounded(gq, gk, dim, scale, rel_bias):
    qk = dim * jnp.max(jnp.abs(gq)) * jnp.max(jnp.abs(gk)) * (scale * LOG2E * 1.02)
    bias = jnp.max(jnp.abs(rel_bias - rel_bias[NUM_BUCKETS - 1:])) * LOG2E
    return qk + bias <= MAX_EXP2_ARG


def kernel(x, mem, rel_bias, ffn1_g, ffn1_w_gate, ffn1_w_up, ffn1_w_down, ffn2_g, ffn2_w_gate,
           ffn2_w_up, ffn2_w_down, mix_g, mem_g, mem_w_kv, mem_gq, mem_gk, w_out, a_w_in, a_gq, a_gk,
           a_lam_q1, a_lam_k1, a_lam_q2, a_lam_k2, a_g_sub, b_w_in, b_gq, b_gk):
    B, T, D = x.shape
    N = mem.shape[1]
    depth = ffn1_g.shape[0]
    M = B * T
    tq = min(256, T)
    head_scale = HEAD_DIM ** -0.5 * LOG2E
    ones = lambda n: jnp.ones((n,), F32)
    bf = lambda w: w.astype(BF16)

    x2 = x.reshape(M, D)
    mem2 = mem.reshape(B * N, D)
    tab = _bias_tiles(rel_bias, tq, tq)
    ffn_w = [(bf(ffn1_w_gate), bf(ffn1_w_up), bf(0.5 * ffn1_w_down)),
             (bf(ffn2_w_gate), bf(ffn2_w_up), bf(0.5 * ffn2_w_down))]
    ffn_g = [ffn1_g, ffn2_g]
    wo, w_kv, w_a = bf(w_out), bf(mem_w_kv), bf(a_w_in)

    for i in range(depth):
        x2 = _ffn(x2, ffn_g[0][i], *ffn_w[0], i)

        kv_gain = jnp.concatenate([jnp.tile(mem_gk[i], N_MEM_HEADS), ones(MEM_WIDTH)])
        kv3 = _proj(mem2, mem_g[i], w_kv, i, [2, 0], kv_gain, BF16, MEM_WIDTH)
        kv3 = kv3.reshape(B, N, 2 * MEM_WIDTH)
        qm_gain = jnp.tile(mem_gq[i], N_MEM_HEADS) * head_scale
        j = i // 2

        if i % 2 == 0:
            n_grp = SELF_WIDTH // DIFF_QK_DIM
            gain = jnp.concatenate([jnp.tile(a_gq[j], n_grp) * (DIFF_QK_DIM ** -0.5 * LOG2E),
                                    jnp.tile(a_gk[j], n_grp), ones(SELF_WIDTH), qm_gain])
            kinds = [1] * 6 + [0] * 3 + [2]
            p3 = _proj(x2, mix_g[i], w_a, j, kinds, gain, BF16, 512).reshape(B, T, -1)
            lambda_init = 0.8 - 0.6 * math.exp(-0.3 * i)
            attn = functools.partial(_diff_attn, p3, tab, a_lam_q1[j], a_lam_k1[j], a_lam_q2[j],
                                     a_lam_k2[j], a_g_sub[j], lambda_init, tq)
            bounded = _scores_bounded(a_gq[j], a_gk[j], DIFF_QK_DIM, DIFF_QK_DIM ** -0.5, rel_bias)
            qm_block = 3 * SELF_WIDTH // MEM_WIDTH
        else:
            w = b_w_in[j]
            kvw = DSA_KV_HEADS * HEAD_DIM
            iqw = IDX_HEADS * IDX_DIM
            c_q, c_k, c_v, c_iq = 0, SELF_WIDTH, SELF_WIDTH + kvw, SELF_WIDTH + 2 * kvw
            c_ik = c_iq + iqw
            c_iw = c_ik + IDX_DIM
            c_qm = c_iw + IDX_HEADS
            w_main = jnp.concatenate([w[:, c_q:c_k], w[:, c_k:c_v], w[:, c_iq:c_ik], w[:, c_v:c_iq],
                                      w[:, c_qm:]], axis=1)
            gain = jnp.concatenate([jnp.tile(b_gq[j], N_SELF_HEADS) * head_scale,
                                    jnp.tile(b_gk[j], DSA_KV_HEADS), ones(iqw), ones(kvw), qm_gain])
            kinds = [2, 2, 2, 2, 0, 0, 0, 2]
            pad = LANES - IDX_DIM - IDX_HEADS
            w_idx = jnp.concatenate([w[:, c_ik:c_qm], jnp.zeros((D, pad), F32)], axis=1)
            idx_gain = jnp.concatenate([ones(IDX_DIM),
                                        jnp.full((IDX_HEADS,), (IDX_DIM * IDX_HEADS) ** -0.5, F32),
                                        jnp.zeros((pad,), F32)])
            p3, ikw = _proj(x2, mix_g[i], bf(w_main)[None], 0, kinds, gain, BF16, 512,
                            side=(bf(w_idx), idx_gain))
            p3, ikw = p3.reshape(B, T, -1), ikw.reshape(B, T, LANES)
            ik = ikw[:, :, :IDX_DIM].astype(BF16)
            ik2 = jnp.concatenate([ik, ik], axis=-1)
            iw = jnp.swapaxes(ikw[:, :, IDX_DIM:IDX_DIM + IDX_HEADS], 1, 2)
            attn = functools.partial(_dsa_attn, p3, ik2, iw, tab, tq)
            bounded = _scores_bounded(b_gq[j], b_gk[j], HEAD_DIM, HEAD_DIM ** -0.5, rel_bias)
            qm_block = (p3.shape[-1] - MEM_WIDTH) // MEM_WIDTH

        y_self = lax.cond(bounded, lambda: attn(True), lambda: attn(False))
        x2 = _out_proj(y_self.reshape(M, SELF_WIDTH), p3.reshape(M, -1), qm_block, kv3, wo, i, x2)

        x2 = _ffn(x2, ffn_g[1][i], *ffn_w[1], i)

    return x2.reshape(B, T, D)
```

```python
import functools
import math
from typing import Any, Callable, NamedTuple

import jax
import jax.numpy as jnp
from jax import lax
from jax.experimental import pallas as pl
from jax.experimental.pallas import tpu as pltpu

HEAD_DIM = 128
N_SELF_HEADS = 12
N_MEM_HEADS = 4
SELF_WIDTH = N_SELF_HEADS * HEAD_DIM
MEM_WIDTH = N_MEM_HEADS * HEAD_DIM
DIFF_QK_DIM = HEAD_DIM // 2
DSA_KV_HEADS = 4
DSA_GROUP = N_SELF_HEADS // DSA_KV_HEADS
IDX_HEADS = 16
IDX_DIM = 64
TOPK_MAX = 256
NUM_BUCKETS = 32
MAX_DISTANCE = 128
EPS = 1e-6
NEG_INF = -1e30
INT32_MIN = -(2 ** 31)
LOG2E = math.log2(math.e)

FAR_TILE = 2

LANES = 128
VMEM_LIMIT = 56 * 1024 * 1024

F32 = jnp.float32
BF16 = jnp.bfloat16


def _dot(a, b):
    return jnp.dot(a, b, preferred_element_type=F32)


def _dot_nt(a, b):
    return lax.dot_general(a, b, (((1,), (1,)), ((), ())), preferred_element_type=F32)


def _params(semantics):
    return pltpu.CompilerParams(dimension_semantics=semantics, vmem_limit_bytes=VMEM_LIMIT)


def _rms_rows(x, g):
    ms = jnp.mean(x * x, axis=-1, keepdims=True)
    return x * lax.rsqrt(ms + EPS) * g


def _ffn_body(x_ref, g_ref, wg_ref, wu_ref, wd_ref, o_ref, h_ref):
    @pl.when(pl.program_id(1) == 0)
    def _():
        x = x_ref[...]
        h_ref[...] = _rms_rows(x, g_ref[...]).astype(BF16)
        o_ref[...] = x

    h = h_ref[...]
    gate = _dot(h, wg_ref[...])
    up = _dot(h, wu_ref[...])
    act = (gate * jax.nn.sigmoid(gate) * up).astype(BF16)
    o_ref[...] += _dot(act, wd_ref[...])


def _ffn(x2, g, wg, wu, wd_half, layer):
    M, D = x2.shape
    F = wg.shape[2]
    tm = min(1024, M)
    tf = min(512, F)
    return pl.pallas_call(
        _ffn_body,
        grid=(M // tm, F // tf),
        in_specs=[
            pl.BlockSpec((tm, D), lambda i, f: (i, 0)),
            pl.BlockSpec((1, D), lambda i, f: (0, 0)),
            pl.BlockSpec((None, D, tf), lambda i, f: (layer, 0, f)),
            pl.BlockSpec((None, D, tf), lambda i, f: (layer, 0, f)),
            pl.BlockSpec((None, tf, D), lambda i, f: (layer, f, 0)),
        ],
        out_specs=pl.BlockSpec((tm, D), lambda i, f: (i, 0)),
        out_shape=jax.ShapeDtypeStruct((M, D), F32),
        scratch_shapes=[pltpu.VMEM((tm, D), BF16)],
        compiler_params=_params(("parallel", "arbitrary")),
        name="ffn",
    )(x2, g.reshape(1, D), wg, wu, wd_half)


def _proj_body(types_ref, x_ref, g_ref, w_ref, gmat_ref, gain_ref, *rest, tn, kw, side):
    if side:
        ws_ref, gs_ref, o_ref, os_ref, h_ref = rest
    else:
        o_ref, h_ref = rest
    j = pl.program_id(1)

    @pl.when(j == 0)
    def _():
        h_ref[...] = _rms_rows(x_ref[...], g_ref[...]).astype(BF16)

    y = _dot(h_ref[...], w_ref[...])
    for sec in range(tn // kw):
        kind = types_ref[j * (tn // kw) + sec]

        @pl.when(kind == 0)
        def _(sec=sec):
            cols = slice(sec * kw, (sec + 1) * kw)
            o_ref[:, cols] = (y[:, cols] * gain_ref[:, cols]).astype(o_ref.dtype)

        @pl.when(kind != 0)
        def _(sec=sec, kind=kind):
            gmat = gmat_ref[kind]
            for c in range(sec * kw // LANES, (sec + 1) * kw // LANES):
                sl = slice(c * LANES, (c + 1) * LANES)
                yc = y[:, sl]
                ms = _dot((yc * yc).astype(BF16), gmat)
                o_ref[:, sl] = (yc * lax.rsqrt(ms + EPS) * gain_ref[:, sl]).astype(o_ref.dtype)

    if side:
        @pl.when(j == pl.num_programs(1) - 1)
        def _():
            os_ref[...] = _dot(h_ref[...], ws_ref[...]) * gs_ref[...]


def _group_mats():
    r = jnp.arange(LANES)
    same64 = (r[:, None] // DIFF_QK_DIM) == (r[None, :] // DIFF_QK_DIM)
    return jnp.stack([
        jnp.zeros((LANES, LANES), F32),
        jnp.where(same64, 1.0 / DIFF_QK_DIM, 0.0),
        jnp.full((LANES, LANES), 1.0 / HEAD_DIM, F32),
    ]).astype(BF16)


def _proj(x2, g, w, layer, kinds, gain, out_dtype, kw, side=None):
    M, D = x2.shape
    N = w.shape[2]
    tm = min(1024, M)
    tn = 2 * kw if N % (2 * kw) == 0 else kw
    kinds = jnp.asarray(kinds, jnp.int32)
    main_spec = pl.BlockSpec((tm, tn), lambda i, j, t: (i, j))
    grid_spec = pltpu.PrefetchScalarGridSpec(
        num_scalar_prefetch=1,
        grid=(M // tm, N // tn),
        in_specs=[
            pl.BlockSpec((tm, D), lambda i, j, t: (i, 0)),
            pl.BlockSpec((1, D), lambda i, j, t: (0, 0)),
            pl.BlockSpec((None, D, tn), lambda i, j, t: (layer, 0, j)),
            pl.BlockSpec((3, LANES, LANES), lambda i, j, t: (0, 0, 0)),
            pl.BlockSpec((1, tn), lambda i, j, t: (0, j)),
        ] + ([pl.BlockSpec((D, LANES), lambda i, j, t: (0, 0)),
              pl.BlockSpec((1, LANES), lambda i, j, t: (0, 0))] if side else []),
        out_specs=([main_spec, pl.BlockSpec((tm, LANES), lambda i, j, t: (i, 0))] if side else main_spec),
        scratch_shapes=[pltpu.VMEM((tm, D), BF16)],
    )
    main_shape = jax.ShapeDtypeStruct((M, N), out_dtype)
    return pl.pallas_call(
        functools.partial(_proj_body, tn=tn, kw=kw, side=bool(side)),
        grid_spec=grid_spec,
        out_shape=[main_shape, jax.ShapeDtypeStruct((M, LANES), F32)] if side else main_shape,
        compiler_params=_params(("parallel", "arbitrary")),
        name="proj",
    )(kinds, x2, g.reshape(1, D), w, _group_mats(), gain.reshape(1, N),
      *((side[0], side[1].reshape(1, LANES)) if side else ()))


def _bias_body(rb_ref, o_ref, *, tq, tk):
    h = pl.program_id(0)
    r = lax.broadcasted_iota(jnp.int32, (tq, tk), 0)
    c = lax.broadcasted_iota(jnp.int32, (tq, tk), 1)
    max_exact = NUM_BUCKETS // 2
    far = rb_ref[NUM_BUCKETS - 1, h]
    o_ref[FAR_TILE] = jnp.zeros((tq, tk), F32)
    for off in range(FAR_TILE):
        dist = r - c + off * tk
        n = jnp.maximum(dist, 0)
        nf = jnp.maximum(n, 1).astype(F32)
        large = max_exact + (jnp.log(nf / max_exact) / math.log(MAX_DISTANCE / max_exact)
                             * (NUM_BUCKETS - max_exact)).astype(jnp.int32)
        large = jnp.minimum(large, NUM_BUCKETS - 1)
        bucket = jnp.where(n < max_exact, n, large)
        val = jnp.zeros((tq, tk), F32)
        for b in range(NUM_BUCKETS):
            val = jnp.where(bucket == b, rb_ref[b, h], val)
        o_ref[off] = jnp.where(dist >= 0, (val - far) * LOG2E, NEG_INF)


def _bias_tiles(rel_bias, tq, tk):
    assert tq == tk and tk >= MAX_DISTANCE
    return pl.pallas_call(
        functools.partial(_bias_body, tq=tq, tk=tk),
        grid=(N_SELF_HEADS,),
        in_specs=[pl.BlockSpec(memory_space=pltpu.SMEM)],
        out_specs=pl.BlockSpec((None, FAR_TILE + 1, tq, tk), lambda h: (h, 0, 0, 0)),
        out_shape=jax.ShapeDtypeStruct((N_SELF_HEADS, FAR_TILE + 1, tq, tk), F32),
        compiler_params=_params(("arbitrary",)),
        name="bias_tiles",
    )(rel_bias)


class _Stream(NamedTuple):
    qs_ref: Any
    k_rows: Callable
    v_rows: Callable
    bias: Callable
    scratch: tuple


TILE_GROUPS = (8, 4, 2, 1)
MAX_EXP2_ARG = 60.0


def _attention_scratch(rows, n_tiles, tk, bounded, lead=()):
    if bounded:
        return [pltpu.VMEM(lead + (rows, 2 * HEAD_DIM), F32)]
    return [
        pltpu.VMEM(lead + (n_tiles, rows, tk), F32),
        pltpu.VMEM(lead + (rows, LANES), F32),
        pltpu.VMEM(lead + (rows, LANES), F32),
        pltpu.VMEM(lead + (rows, HEAD_DIM), F32),
    ]


def _attend(streams, i, tk, bounded):
    return (_bounded_attention if bounded else _two_pass_attention)(streams, i, tk)


def _lane_fold(x, op):
    out = x[:, :LANES]
    for c in range(1, x.shape[1] // LANES):
        out = op(out, x[:, c * LANES:(c + 1) * LANES])
    return out


def _key_rows(j, tk, n=1):
    return pl.ds(pl.multiple_of(j * tk, tk), n * tk)


def _fold_tile_groups(lo, hi, body, carry, groups=TILE_GROUPS):
    for u in groups:
        n = (hi - lo) // u

        def trip(t, c, u=u, lo=lo):
            return body(lo + t * u, u, c)

        carry = lax.fori_loop(0, n, trip, carry)
        lo = lo + n * u
    return carry


def _for_tile_groups(lo, hi, body):
    _fold_tile_groups(lo, hi, lambda j0, u, c: (body(j0, u), c)[1], 0)


def _two_pass_attention(streams, i, tk):
    for st in streams:
        _, m_ref, l_ref, acc_ref = st.scratch
        m_ref[...] = jnp.full(m_ref.shape, NEG_INF, F32)
        l_ref[...] = jnp.zeros(l_ref.shape, F32)
        acc_ref[...] = jnp.zeros(acc_ref.shape, F32)

    def score(j0, u):
        for st in streams:
            s_ref, m_ref = st.scratch[:2]
            s = _dot_nt(st.qs_ref[...], st.k_rows(_key_rows(j0, tk, u)))
            m = m_ref[...]
            for t in range(u):
                s_t = st.bias(s[:, t * tk:(t + 1) * tk], j0 + t)
                s_ref[j0 + t] = s_t
                m = jnp.maximum(m, _lane_fold(s_t, jnp.maximum))
            m_ref[...] = m

    _for_tile_groups(0, i + 1, score)

    for st in streams:
        m_ref = st.scratch[1]
        m = jnp.max(m_ref[...], axis=-1, keepdims=True)
        m_ref[...] = jnp.broadcast_to(m, m_ref.shape)

    def pv(j0, u):
        for st in streams:
            s_ref, m_ref, l_ref, acc_ref = st.scratch
            m = jnp.concatenate([m_ref[...]] * (tk // LANES), axis=1)
            p = jnp.concatenate([jnp.exp2(s_ref[j0 + t] - m) for t in range(u)], axis=1)
            l_ref[...] += _lane_fold(p, jnp.add)
            acc_ref[...] += _dot(p.astype(BF16), st.v_rows(_key_rows(j0, tk, u)))

    _for_tile_groups(0, i + 1, pv)
    return [st.scratch[3][...] / jnp.sum(st.scratch[2][...], axis=-1, keepdims=True) for st in streams]


def _bounded_attention(streams, i, tk):
    for st in streams:
        acc_ref, = st.scratch
        acc_ref[...] = jnp.zeros(acc_ref.shape, F32)

    def step(j0, u):
        for st in streams:
            acc_ref, = st.scratch
            rows = _key_rows(j0, tk, u)
            s = _dot_nt(st.qs_ref[...], st.k_rows(rows))
            p = jnp.concatenate(
                [jnp.exp2(st.bias(s[:, t * tk:(t + 1) * tk], j0 + t)).astype(BF16) for t in range(u)],
                axis=1)
            v = st.v_rows(rows)
            acc_ref[...] += _dot(p, jnp.concatenate([v, jnp.ones_like(v)], axis=1))

    _for_tile_groups(0, i + 1, step)
    return [st.scratch[0][:, :HEAD_DIM] / st.scratch[0][:, HEAD_DIM:] for st in streams]


def _diff_body(lq1_ref, lk1_ref, lq2_ref, lk2_ref, q_ref, k_ref, v_ref, tab_ref, gsub_ref, o_ref,
               qs_ref, *scratch, tq, tk, hp, lambda_init, bounded):
    i = pl.program_id(2)
    lane = lax.broadcasted_iota(jnp.int32, (tq, HEAD_DIM), 1)
    streams = []
    for hh in range(hp):
        cols = slice(hh * HEAD_DIM, (hh + 1) * HEAD_DIM)
        q = q_ref[:, cols]
        zero = jnp.zeros_like(q)
        qs_ref[hh, :tq] = jnp.where(lane < DIFF_QK_DIM, q, zero)
        qs_ref[hh, tq:] = jnp.where(lane >= DIFF_QK_DIM, q, zero)

        def bias(s, j, hh=hh):
            tab = tab_ref[hh, jnp.minimum(i - j, FAR_TILE)]
            return s + jnp.concatenate([tab, tab], axis=0)

        streams.append(_Stream(
            qs_ref.at[hh],
            lambda rows, cols=cols: k_ref[rows, cols],
            lambda rows, cols=cols: v_ref[rows, cols],
            bias, tuple(r.at[hh] for r in scratch)))

    outs = _attend(streams, i, tk, bounded)

    lam = (jnp.exp(jnp.sum(lq1_ref[...] * lk1_ref[...], axis=-1, keepdims=True))
           - jnp.exp(jnp.sum(lq2_ref[...] * lk2_ref[...], axis=-1, keepdims=True)) + lambda_init)
    for hh, out in enumerate(outs):
        o = out[:tq] - lam * out[tq:]
        o_ref[:, hh * HEAD_DIM:(hh + 1) * HEAD_DIM] = (
            _rms_rows(o, gsub_ref[...]) * (1.0 - lambda_init)).astype(o_ref.dtype)


def _diff_attn(p3, tab, lq1, lk1, lq2, lk2, g_sub, lambda_init, tq, bounded):
    B, T, _ = p3.shape
    H = N_SELF_HEADS
    hp = 6 if bounded else 2
    nq = T // tq
    w = hp * HEAD_DIM
    vec = lambda a: a.reshape(1, -1).astype(F32)
    small = lambda n: pl.BlockSpec((1, n), lambda b, h, i: (0, 0))
    return pl.pallas_call(
        functools.partial(_diff_body, tq=tq, tk=tq, hp=hp, lambda_init=lambda_init, bounded=bounded),
        grid=(B, H // hp, nq),
        in_specs=[
            small(DIFF_QK_DIM), small(DIFF_QK_DIM), small(DIFF_QK_DIM), small(DIFF_QK_DIM),
            pl.BlockSpec((None, tq, w), lambda b, h, i: (b, i, h)),
            pl.BlockSpec((None, T, w), lambda b, h, i: (b, 0, H // hp + h)),
            pl.BlockSpec((None, T, w), lambda b, h, i: (b, 0, 2 * (H // hp) + h)),
            pl.BlockSpec((hp, FAR_TILE + 1, tq, tq), lambda b, h, i: (h, 0, 0, 0)),
            small(HEAD_DIM),
        ],
        out_specs=pl.BlockSpec((None, tq, w), lambda b, h, i: (b, i, h)),
        out_shape=jax.ShapeDtypeStruct((B, T, SELF_WIDTH), BF16),
        scratch_shapes=[pltpu.VMEM((hp, 2 * tq, HEAD_DIM), BF16)]
        + _attention_scratch(2 * tq, nq, tq, bounded, lead=(hp,)),
        compiler_params=_params(("parallel", "parallel", "arbitrary")),
        name="diff_attn",
    )(vec(lq1), vec(lk1), vec(lq2), vec(lk2), p3, p3, p3, tab, vec(g_sub))


SUBLANES = 8


def _sortable(x):
    bits = pltpu.bitcast(x, jnp.int32)
    return jnp.where(bits < 0, bits ^ jnp.int32(0x7FFFFFFF), bits)


def _dsa_select(iq_ref, ik_ref, iwt_ref, key_ref, madd_ref, iqm_ref, thr_ref, need_ref, i,
                *, tq, tk, topk):
    n_tiles = i + 1
    lane = lax.broadcasted_iota(jnp.int32, (tq, LANES), 1)
    for hh in range(IDX_HEADS):
        pair = iq_ref[:, (hh // 2) * LANES:(hh // 2 + 1) * LANES]
        keep = (lane < IDX_DIM) if hh % 2 == 0 else (lane >= IDX_DIM)
        iqm_ref[hh] = jnp.where(keep, pair, jnp.zeros_like(pair))

    key_pos = lax.broadcasted_iota(jnp.int32, (tk, tq), 0)
    qry_pos = lax.broadcasted_iota(jnp.int32, (tk, tq), 1)

    def causal(j):
        return (i * tq + qry_pos) >= (j * tk + key_pos)

    def score_tiles(j0, u, c):
        for j in [j0 + t for t in range(u)]:
            ik = ik_ref[_key_rows(j, tk), :]
            sc = jnp.zeros((tk, tq), F32)
            for hh in range(IDX_HEADS):
                sc = sc + jnp.maximum(_dot_nt(ik, iqm_ref[hh]), 0.0) * iwt_ref[hh:hh + 1, :]
            key_ref[j] = _sortable(jnp.where(causal(j), sc, NEG_INF))
        return c

    _fold_tile_groups(0, n_tiles, score_tiles, 0, groups=(4, 2, 1))

    kf = float(topk)

    def count(cmp, cand):
        def body(j0, u, part):
            for t in range(u):
                hit = jnp.where(cmp(key_ref[j0 + t], cand), 1.0, 0.0)
                part = part + jnp.sum(hit.reshape(tk // SUBLANES, SUBLANES, tq), axis=0)
            return part

        part = _fold_tile_groups(0, n_tiles, body, jnp.zeros((SUBLANES, tq), F32), groups=(4, 2, 1))
        return jnp.sum(part, axis=0, keepdims=True)

    ge = lambda k, c: k >= c
    zero = jnp.zeros((1, tq), jnp.int32)
    total = (n_tiles * tk).astype(F32)
    n_pos = count(ge, zero)
    state = (jnp.where(n_pos >= kf, zero, jnp.int32(INT32_MIN)), jnp.where(n_pos >= kf, n_pos, total))

    def bit_step(b, state):
        base, n_base = state
        cand = base | lax.shift_left(jnp.int32(1), jnp.int32(30) - b)
        n = count(ge, cand)
        return jnp.where(n >= kf, cand, base), jnp.where(n >= kf, n, n_base)

    thr, n_thr = lax.fori_loop(0, 31, bit_step, state)
    thr_ref[...] = jnp.broadcast_to(thr, thr_ref.shape)
    surplus = jnp.maximum(jnp.max(n_thr - kf), 0.0)

    @pl.when(surplus == 0.0)
    def _():
        def mask_tiles(j0, u):
            for j in [j0 + t for t in range(u)]:
                take = (key_ref[j] >= thr_ref[0:1, :]) & causal(j)
                madd_ref[j] = jnp.where(take, 0.0, NEG_INF).T

        _for_tile_groups(0, n_tiles, mask_tiles)

    @pl.when(surplus != 0.0)
    def _():
        need_ref[...] = jnp.broadcast_to(kf - count(lambda k, c: k > c, thr_ref[0:1, :]), need_ref.shape)
        lower = jnp.where(lax.broadcasted_iota(jnp.int32, (tk, tk), 1)
                          < lax.broadcasted_iota(jnp.int32, (tk, tk), 0), 1.0, 0.0).astype(BF16)

        def mask_tile(j, seen):
            key = key_ref[j]
            thr = thr_ref[0:1, :]
            eq = jnp.where(key == thr, 1.0, 0.0)
            rank = seen + _dot(lower, eq.astype(BF16))
            take = (key > thr) | ((key == thr) & (rank < need_ref[0:1, :]))
            madd_ref[j] = jnp.where(take & causal(j), 0.0, NEG_INF).T
            return seen + jnp.sum(eq, axis=0, keepdims=True)

        lax.fori_loop(0, n_tiles, mask_tile, jnp.zeros((1, tq), F32))


def _dsa_body(q_ref, k_ref, v_ref, iq_ref, ik_ref, iw_ref, tab_ref, o_ref,
              key_ref, madd_ref, iqm_ref, thr_ref, need_ref, qs_ref, *scratch,
              tq, tk, topk, gp, bounded):
    i = pl.program_id(1)
    R = DSA_GROUP

    @pl.when(pl.program_id(2) == 0)
    def _():
        _dsa_select(iq_ref, ik_ref, iw_ref, key_ref, madd_ref, iqm_ref, thr_ref, need_ref, i,
                    tq=tq, tk=tk, topk=topk)

    streams = []
    for g in range(gp):
        for r in range(R):
            h = g * R + r
            qs_ref[g, r * tq:(r + 1) * tq] = q_ref[:, h * HEAD_DIM:(h + 1) * HEAD_DIM]
        cols = slice(g * HEAD_DIM, (g + 1) * HEAD_DIM)

        def bias(s, j, g=g):
            back = jnp.minimum(i - j, FAR_TILE)
            return s + jnp.concatenate([madd_ref[j] + tab_ref[g * R + r, back] for r in range(R)], axis=0)

        streams.append(_Stream(
            qs_ref.at[g],
            lambda rows, cols=cols: k_ref[rows, cols],
            lambda rows, cols=cols: v_ref[rows, cols],
            bias, tuple(ref.at[g] for ref in scratch)))

    for g, out in enumerate(_attend(streams, i, tk, bounded)):
        for r in range(R):
            h = g * R + r
            o_ref[:, h * HEAD_DIM:(h + 1) * HEAD_DIM] = out[r * tq:(r + 1) * tq].astype(o_ref.dtype)


def _dsa_attn(p3, ik2, iw, tab, tq, bounded):
    B, T, _ = p3.shape
    H, R, G = N_SELF_HEADS, DSA_GROUP, DSA_KV_HEADS
    topk = min(TOPK_MAX, T // 4)
    nq = T // tq
    iq_w = IDX_HEADS * IDX_DIM
    gp = G if bounded else 1
    kw = gp * HEAD_DIM
    k0 = H * HEAD_DIM // kw
    iq0 = (H + G) * HEAD_DIM // iq_w
    v0 = ((H + G) * HEAD_DIM + iq_w) // kw
    assert (H + G) * HEAD_DIM % iq_w == 0
    return pl.pallas_call(
        functools.partial(_dsa_body, tq=tq, tk=tq, topk=topk, gp=gp, bounded=bounded),
        grid=(B, nq, G // gp),
        in_specs=[
            pl.BlockSpec((None, tq, gp * R * HEAD_DIM), lambda b, i, g: (b, i, g)),
            pl.BlockSpec((None, T, kw), lambda b, i, g: (b, 0, k0 + g)),
            pl.BlockSpec((None, T, kw), lambda b, i, g: (b, 0, v0 + g)),
            pl.BlockSpec((None, tq, iq_w), lambda b, i, g: (b, i, iq0)),
            pl.BlockSpec((None, T, LANES), lambda b, i, g: (b, 0, 0)),
            pl.BlockSpec((None, IDX_HEADS, tq), lambda b, i, g: (b, 0, i)),
            pl.BlockSpec((gp * R, FAR_TILE + 1, tq, tq), lambda b, i, g: (g, 0, 0, 0)),
        ],
        out_specs=pl.BlockSpec((None, tq, gp * R * HEAD_DIM), lambda b, i, g: (b, i, g)),
        out_shape=jax.ShapeDtypeStruct((B, T, SELF_WIDTH), BF16),
        scratch_shapes=[
            pltpu.VMEM((nq, tq, tq), jnp.int32),
            pltpu.VMEM((nq, tq, tq), F32),
            pltpu.VMEM((IDX_HEADS, tq, LANES), BF16),
            pltpu.VMEM((SUBLANES, tq), jnp.int32),
            pltpu.VMEM((SUBLANES, tq), F32),
            pltpu.VMEM((gp, R * tq, HEAD_DIM), BF16),
        ] + _attention_scratch(R * tq, nq, tq, bounded, lead=(gp,)),
        compiler_params=_params(("parallel", "arbitrary", "arbitrary")),
        name="dsa_attn",
    )(p3, p3, p3, p3, ik2, iw, tab)


def _out_body(ys_ref, qm_ref, km_ref, vm_ref, ws_ref, wm_ref, x_ref, o_ref):
    acc = x_ref[...] + _dot(ys_ref[...], ws_ref[...])
    heads = []
    for h in range(N_MEM_HEADS):
        sl = slice(h * HEAD_DIM, (h + 1) * HEAD_DIM)
        s = _dot_nt(qm_ref[:, sl], km_ref[:, sl])
        p = jnp.exp2(s - jnp.max(s, axis=-1, keepdims=True))
        o = _dot(p.astype(BF16), vm_ref[:, sl]) / jnp.sum(p, axis=-1, keepdims=True)
        heads.append(o.astype(BF16))
    o_ref[...] = acc + _dot(jnp.concatenate(heads, axis=1), wm_ref[...])


def _out_proj(ys, p2, qm_block, kv3, wo, layer, x2):
    M, D = x2.shape
    B, N, _ = kv3.shape
    tm = min(512, M // B)
    per_batch = M // B // tm
    return pl.pallas_call(
        _out_body,
        grid=(M // tm,),
        in_specs=[
            pl.BlockSpec((tm, SELF_WIDTH), lambda i: (i, 0)),
            pl.BlockSpec((tm, MEM_WIDTH), lambda i: (i, qm_block)),
            pl.BlockSpec((None, N, MEM_WIDTH), lambda i: (i // per_batch, 0, 0)),
            pl.BlockSpec((None, N, MEM_WIDTH), lambda i: (i // per_batch, 0, 1)),
            pl.BlockSpec((None, SELF_WIDTH, D), lambda i: (layer, 0, 0)),
            pl.BlockSpec((None, MEM_WIDTH, D), lambda i: (layer, SELF_WIDTH // MEM_WIDTH, 0)),
            pl.BlockSpec((tm, D), lambda i: (i, 0)),
        ],
        out_specs=pl.BlockSpec((tm, D), lambda i: (i, 0)),
        out_shape=jax.ShapeDtypeStruct((M, D), F32),
        compiler_params=_params(("parallel",)),
        name="out_proj",
    )(ys, p2, kv3, kv3, wo, wo, x2)


def _scores_bounded(gq, gk, dim, scale, rel_bias):
    qk = dim * jnp.max(jnp.abs(gq)) * jnp.max(jnp.abs(gk)) * (scale * LOG2E * 1.02)
    bias = jnp.max(jnp.abs(rel_bias - rel_bias[NUM_BUCKETS - 1:])) * LOG2E
    return qk + bias <= MAX_EXP2_ARG


def kernel(x, mem, rel_bias, ffn1_g, ffn1_w_gate, ffn1_w_up, ffn1_w_down, ffn2_g, ffn2_w_gate,
           ffn2_w_up, ffn2_w_down, mix_g, mem_g, mem_w_kv, mem_gq, mem_gk, w_out, a_w_in, a_gq, a_gk,
           a_lam_q1, a_lam_k1, a_lam_q2, a_lam_k2, a_g_sub, b_w_in, b_gq, b_gk):
    B, T, D = x.shape
    N = mem.shape[1]
    depth = ffn1_g.shape[0]
    M = B * T
    tq = min(256, T)
    head_scale = HEAD_DIM ** -0.5 * LOG2E
    ones = lambda n: jnp.ones((n,), F32)
    bf = lambda w: w.astype(BF16)

    x2 = x.reshape(M, D)
    mem2 = mem.reshape(B * N, D)
    tab = _bias_tiles(rel_bias, tq, tq)
    ffn_w = [(bf(ffn1_w_gate), bf(ffn1_w_up), bf(0.5 * ffn1_w_down)),
             (bf(ffn2_w_gate), bf(ffn2_w_up), bf(0.5 * ffn2_w_down))]
    ffn_g = [ffn1_g, ffn2_g]
    wo, w_kv, w_a = bf(w_out), bf(mem_w_kv), bf(a_w_in)

    for i in range(depth):
        x2 = _ffn(x2, ffn_g[0][i], *ffn_w[0], i)

        kv_gain = jnp.concatenate([jnp.tile(mem_gk[i], N_MEM_HEADS), ones(MEM_WIDTH)])
        kv3 = _proj(mem2, mem_g[i], w_kv, i, [2, 0], kv_gain, BF16, MEM_WIDTH)
        kv3 = kv3.reshape(B, N, 2 * MEM_WIDTH)
        qm_gain = jnp.tile(mem_gq[i], N_MEM_HEADS) * head_scale
        j = i // 2

        if i % 2 == 0:
            n_grp = SELF_WIDTH // DIFF_QK_DIM
            gain = jnp.concatenate([jnp.tile(a_gq[j], n_grp) * (DIFF_QK_DIM ** -0.5 * LOG2E),
                                    jnp.tile(a_gk[j], n_grp), ones(SELF_WIDTH), qm_gain])
            kinds = [1] * 6 + [0] * 3 + [2]
            p3 = _proj(x2, mix_g[i], w_a, j, kinds, gain, BF16, 512).reshape(B, T, -1)
            lambda_init = 0.8 - 0.6 * math.exp(-0.3 * i)
            attn = functools.partial(_diff_attn, p3, tab, a_lam_q1[j], a_lam_k1[j], a_lam_q2[j],
                                     a_lam_k2[j], a_g_sub[j], lambda_init, tq)
            bounded = _scores_bounded(a_gq[j], a_gk[j], DIFF_QK_DIM, DIFF_QK_DIM ** -0.5, rel_bias)
            qm_block = 3 * SELF_WIDTH // MEM_WIDTH
        else:
            w = b_w_in[j]
            kvw = DSA_KV_HEADS * HEAD_DIM
            iqw = IDX_HEADS * IDX_DIM
            c_q, c_k, c_v, c_iq = 0, SELF_WIDTH, SELF_WIDTH + kvw, SELF_WIDTH + 2 * kvw
            c_ik = c_iq + iqw
            c_iw = c_ik + IDX_DIM
            c_qm = c_iw + IDX_HEADS
            w_main = jnp.concatenate([w[:, c_q:c_k], w[:, c_k:c_v], w[:, c_iq:c_ik], w[:, c_v:c_iq],
                                      w[:, c_qm:]], axis=1)
            gain = jnp.concatenate([jnp.tile(b_gq[j], N_SELF_HEADS) * head_scale,
                                    jnp.tile(b_gk[j], DSA_KV_HEADS), ones(iqw), ones(kvw), qm_gain])
            kinds = [2, 2, 2, 2, 0, 0, 0, 2]
            pad = LANES - IDX_DIM - IDX_HEADS
            w_idx = jnp.concatenate([w[:, c_ik:c_qm], jnp.zeros((D, pad), F32)], axis=1)
            idx_gain = jnp.concatenate([ones(IDX_DIM),
                                        jnp.full((IDX_HEADS,), (IDX_DIM * IDX_HEADS) ** -0.5, F32),
                                        jnp.zeros((pad,), F32)])
            p3, ikw = _proj(x2, mix_g[i], bf(w_main)[None], 0, kinds, gain, BF16, 512,
                            side=(bf(w_idx), idx_gain))
            p3, ikw = p3.reshape(B, T, -1), ikw.reshape(B, T, LANES)
            ik = ikw[:, :, :IDX_DIM].astype(BF16)
            ik2 = jnp.concatenate([ik, ik], axis=-1)
            iw = jnp.swapaxes(ikw[:, :, IDX_DIM:IDX_DIM + IDX_HEADS], 1, 2)
            attn = functools.partial(_dsa_attn, p3, ik2, iw, tab, tq)
            bounded = _scores_bounded(b_gq[j], b_gk[j], HEAD_DIM, HEAD_DIM ** -0.5, rel_bias)
            qm_block = (p3.shape[-1] - MEM_WIDTH) // MEM_WIDTH

        y_self = lax.cond(bounded, lambda: attn(True), lambda: attn(False))
        x2 = _out_proj(y_self.reshape(M, SELF_WIDTH), p3.reshape(M, -1), qm_block, kv3, wo, i, x2)

        x2 = _ffn(x2, ffn_g[1][i], *ffn_w[1], i)

    return x2.reshape(B, T, D)
```

```python
import functools
import math
from typing import Any, Callable, NamedTuple

import jax
import jax.numpy as jnp
from jax import lax
from jax.experimental import pallas as pl
from jax.experimental.pallas import tpu as pltpu

HEAD_DIM = 128
N_SELF_HEADS = 12
N_MEM_HEADS = 4
SELF_WIDTH = N_SELF_HEADS * HEAD_DIM
MEM_WIDTH = N_MEM_HEADS * HEAD_DIM
DIFF_QK_DIM = HEAD_DIM // 2
DSA_KV_HEADS = 4
DSA_GROUP = N_SELF_HEADS // DSA_KV_HEADS
IDX_HEADS = 16
IDX_DIM = 64
TOPK_MAX = 256
NUM_BUCKETS = 32
MAX_DISTANCE = 128
EPS = 1e-6
NEG_INF = -1e30
INT32_MIN = -(2 ** 31)
LOG2E = math.log2(math.e)

FAR_TILE = 2

LANES = 128
VMEM_LIMIT = 56 * 1024 * 1024

F32 = jnp.float32
BF16 = jnp.bfloat16


def _dot(a, b):
    return jnp.dot(a, b, preferred_element_type=F32)


def _dot_nt(a, b):
    return lax.dot_general(a, b, (((1,), (1,)), ((), ())), preferred_element_type=F32)


def _params(semantics):
    return pltpu.CompilerParams(dimension_semantics=semantics, vmem_limit_bytes=VMEM_LIMIT)


def _rms_rows(x, g):
    ms = jnp.mean(x * x, axis=-1, keepdims=True)
    return x * lax.rsqrt(ms + EPS) * g


def _ffn_body(x_ref, g_ref, wg_ref, wu_ref, wd_ref, o_ref, h_ref):
    @pl.when(pl.program_id(1) == 0)
    def _():
        x = x_ref[...]
        h_ref[...] = _rms_rows(x, g_ref[...]).astype(BF16)
        o_ref[...] = x

    h = h_ref[...]
    gate = _dot(h, wg_ref[...])
    up = _dot(h, wu_ref[...])
    act = (gate * jax.nn.sigmoid(gate) * up).astype(BF16)
    o_ref[...] += _dot(act, wd_ref[...])


def _ffn(x2, g, wg, wu, wd_half, layer):
    M, D = x2.shape
    F = wg.shape[2]
    tm = min(1024, M)
    tf = min(512, F)
    return pl.pallas_call(
        _ffn_body,
        grid=(M // tm, F // tf),
        in_specs=[
            pl.BlockSpec((tm, D), lambda i, f: (i, 0)),
            pl.BlockSpec((1, D), lambda i, f: (0, 0)),
            pl.BlockSpec((None, D, tf), lambda i, f: (layer, 0, f)),
            pl.BlockSpec((None, D, tf), lambda i, f: (layer, 0, f)),
            pl.BlockSpec((None, tf, D), lambda i, f: (layer, f, 0)),
        ],
        out_specs=pl.BlockSpec((tm, D), lambda i, f: (i, 0)),
        out_shape=jax.ShapeDtypeStruct((M, D), F32),
        scratch_shapes=[pltpu.VMEM((tm, D), BF16)],
        compiler_params=_params(("parallel", "arbitrary")),
        name="ffn",
    )(x2, g.reshape(1, D), wg, wu, wd_half)


def _proj_body(types_ref, x_ref, g_ref, w_ref, gmat_ref, gain_ref, *rest, tn, kw, side):
    if side:
        ws_ref, gs_ref, o_ref, os_ref, h_ref = rest
    else:
        o_ref, h_ref = rest
    j = pl.program_id(1)

    @pl.when(j == 0)
    def _():
        h_ref[...] = _rms_rows(x_ref[...], g_ref[...]).astype(BF16)

    y = _dot(h_ref[...], w_ref[...])
    for sec in range(tn // kw):
        kind = types_ref[j * (tn // kw) + sec]

        @pl.when(kind == 0)
        def _(sec=sec):
            cols = slice(sec * kw, (sec + 1) * kw)
            o_ref[:, cols] = (y[:, cols] * gain_ref[:, cols]).astype(o_ref.dtype)

        @pl.when(kind != 0)
        def _(sec=sec, kind=kind):
            gmat = gmat_ref[kind]
            for c in range(sec * kw // LANES, (sec + 1) * kw // LANES):
                sl = slice(c * LANES, (c + 1) * LANES)
                yc = y[:, sl]
                ms = _dot((yc * yc).astype(BF16), gmat)
                o_ref[:, sl] = (yc * lax.rsqrt(ms + EPS) * gain_ref[:, sl]).astype(o_ref.dtype)

    if side:
        @pl.when(j == pl.num_programs(1) - 1)
        def _():
            os_ref[...] = _dot(h_ref[...], ws_ref[...]) * gs_ref[...]


def _group_mats():
    r = jnp.arange(LANES)
    same64 = (r[:, None] // DIFF_QK_DIM) == (r[None, :] // DIFF_QK_DIM)
    return jnp.stack([
        jnp.zeros((LANES, LANES), F32),
        jnp.where(same64, 1.0 / DIFF_QK_DIM, 0.0),
        jnp.full((LANES, LANES), 1.0 / HEAD_DIM, F32),
    ]).astype(BF16)


def _proj(x2, g, w, layer, kinds, gain, out_dtype, kw, side=None):
    M, D = x2.shape
    N = w.shape[2]
    tm = min(1024, M)
    tn = 2 * kw if N % (2 * kw) == 0 else kw
    kinds = jnp.asarray(kinds, jnp.int32)
    main_spec = pl.BlockSpec((tm, tn), lambda i, j, t: (i, j))
    grid_spec = pltpu.PrefetchScalarGridSpec(
        num_scalar_prefetch=1,
        grid=(M // tm, N // tn),
        in_specs=[
            pl.BlockSpec((tm, D), lambda i, j, t: (i, 0)),
            pl.BlockSpec((1, D), lambda i, j, t: (0, 0)),
            pl.BlockSpec((None, D, tn), lambda i, j, t: (layer, 0, j)),
            pl.BlockSpec((3, LANES, LANES), lambda i, j, t: (0, 0, 0)),
            pl.BlockSpec((1, tn), lambda i, j, t: (0, j)),
        ] + ([pl.BlockSpec((D, LANES), lambda i, j, t: (0, 0)),
              pl.BlockSpec((1, LANES), lambda i, j, t: (0, 0))] if side else []),
        out_specs=([main_spec, pl.BlockSpec((tm, LANES), lambda i, j, t: (i, 0))] if side else main_spec),
        scratch_shapes=[pltpu.VMEM((tm, D), BF16)],
    )
    main_shape = jax.ShapeDtypeStruct((M, N), out_dtype)
    return pl.pallas_call(
        functools.partial(_proj_body, tn=tn, kw=kw, side=bool(side)),
        grid_spec=grid_spec,
        out_shape=[main_shape, jax.ShapeDtypeStruct((M, LANES), F32)] if side else main_shape,
        compiler_params=_params(("parallel", "arbitrary")),
        name="proj",
    )(kinds, x2, g.reshape(1, D), w, _group_mats(), gain.reshape(1, N),
      *((side[0], side[1].reshape(1, LANES)) if side else ()))


def _bias_body(rb_ref, o_ref, *, tq, tk):
    h = pl.program_id(0)
    r = lax.broadcasted_iota(jnp.int32, (tq, tk), 0)
    c = lax.broadcasted_iota(jnp.int32, (tq, tk), 1)
    max_exact = NUM_BUCKETS // 2
    far = rb_ref[NUM_BUCKETS - 1, h]
    o_ref[FAR_TILE] = jnp.zeros((tq, tk), F32)
    for off in range(FAR_TILE):
        dist = r - c + off * tk
        n = jnp.maximum(dist, 0)
        nf = jnp.maximum(n, 1).astype(F32)
        large = max_exact + (jnp.log(nf / max_exact) / math.log(MAX_DISTANCE / max_exact)
                             * (NUM_BUCKETS - max_exact)).astype(jnp.int32)
        large = jnp.minimum(large, NUM_BUCKETS - 1)
        bucket = jnp.where(n < max_exact, n, large)
        val = jnp.zeros((tq, tk), F32)
        for b in range(NUM_BUCKETS):
            val = jnp.where(bucket == b, rb_ref[b, h], val)
        o_ref[off] = jnp.where(dist >= 0, (val - far) * LOG2E, NEG_INF)


def _bias_tiles(rel_bias, tq, tk):
    assert tq == tk and tk >= MAX_DISTANCE
    return pl.pallas_call(
        functools.partial(_bias_body, tq=tq, tk=tk),
        grid=(N_SELF_HEADS,),
        in_specs=[pl.BlockSpec(memory_space=pltpu.SMEM)],
        out_specs=pl.BlockSpec((None, FAR_TILE + 1, tq, tk), lambda h: (h, 0, 0, 0)),
        out_shape=jax.ShapeDtypeStruct((N_SELF_HEADS, FAR_TILE + 1, tq, tk), F32),
        compiler_params=_params(("arbitrary",)),
        name="bias_tiles",
    )(rel_bias)


class _Stream(NamedTuple):
    qs_ref: Any
    k_rows: Callable
    v_rows: Callable
    bias: Callable
    scratch: tuple


TILE_GROUPS = (8, 4, 2, 1)
MAX_EXP2_ARG = 60.0


def _attention_scratch(rows, n_tiles, tk, bounded, lead=()):
    if bounded:
        return [pltpu.VMEM(lead + (rows, 2 * HEAD_DIM), F32)]
    return [
        pltpu.VMEM(lead + (n_tiles, rows, tk), F32),
        pltpu.VMEM(lead + (rows, LANES), F32),
        pltpu.VMEM(lead + (rows, LANES), F32),
        pltpu.VMEM(lead + (rows, HEAD_DIM), F32),
    ]


def _attend(streams, i, tk, bounded):
    return (_bounded_attention if bounded else _two_pass_attention)(streams, i, tk)


def _lane_fold(x, op):
    out = x[:, :LANES]
    for c in range(1, x.shape[1] // LANES):
        out = op(out, x[:, c * LANES:(c + 1) * LANES])
    return out


def _key_rows(j, tk, n=1):
    return pl.ds(pl.multiple_of(j * tk, tk), n * tk)


def _fold_tile_groups(lo, hi, body, carry, groups=TILE_GROUPS):
    for u in groups:
        n = (hi - lo) // u

        def trip(t, c, u=u, lo=lo):
            return body(lo + t * u, u, c)

        carry = lax.fori_loop(0, n, trip, carry)
        lo = lo + n * u
    return carry


def _for_tile_groups(lo, hi, body):
    _fold_tile_groups(lo, hi, lambda j0, u, c: (body(j0, u), c)[1], 0)


def _two_pass_attention(streams, i, tk):
    for st in streams:
        _, m_ref, l_ref, acc_ref = st.scratch
        m_ref[...] = jnp.full(m_ref.shape, NEG_INF, F32)
        l_ref[...] = jnp.zeros(l_ref.shape, F32)
        acc_ref[...] = jnp.zeros(acc_ref.shape, F32)

    def score(j0, u):
        for st in streams:
            s_ref, m_ref = st.scratch[:2]
            s = _dot_nt(st.qs_ref[...], st.k_rows(_key_rows(j0, tk, u)))
            m = m_ref[...]
            for t in range(u):
                s_t = st.bias(s[:, t * tk:(t + 1) * tk], j0 + t)
                s_ref[j0 + t] = s_t
                m = jnp.maximum(m, _lane_fold(s_t, jnp.maximum))
            m_ref[...] = m

    _for_tile_groups(0, i + 1, score)

    for st in streams:
        m_ref = st.scratch[1]
        m = jnp.max(m_ref[...], axis=-1, keepdims=True)
        m_ref[...] = jnp.broadcast_to(m, m_ref.shape)

    def pv(j0, u):
        for st in streams:
            s_ref, m_ref, l_ref, acc_ref = st.scratch
            m = jnp.concatenate([m_ref[...]] * (tk // LANES), axis=1)
            p = jnp.concatenate([jnp.exp2(s_ref[j0 + t] - m) for t in range(u)], axis=1)
            l_ref[...] += _lane_fold(p, jnp.add)
            acc_ref[...] += _dot(p.astype(BF16), st.v_rows(_key_rows(j0, tk, u)))

    _for_tile_groups(0, i + 1, pv)
    return [st.scratch[3][...] / jnp.sum(st.scratch[2][...], axis=-1, keepdims=True) for st in streams]


def _bounded_attention(streams, i, tk):
    for st in streams:
        acc_ref, = st.scratch
        acc_ref[...] = jnp.zeros(acc_ref.shape, F32)

    def step(j0, u):
        for st in streams:
            acc_ref, = st.scratch
            rows = _key_rows(j0, tk, u)
            s = _dot_nt(st.qs_ref[...], st.k_rows(rows))
            p = jnp.concatenate(
                [jnp.exp2(st.bias(s[:, t * tk:(t + 1) * tk], j0 + t)).astype(BF16) for t in range(u)],
                axis=1)
            v = st.v_rows(rows)
            acc_ref[...] += _dot(p, jnp.concatenate([v, jnp.ones_like(v)], axis=1))

    _for_tile_groups(0, i + 1, step)
    return [st.scratch[0][:, :HEAD_DIM] / st.scratch[0][:, HEAD_DIM:] for st in streams]


def _diff_body(lq1_ref, lk1_ref, lq2_ref, lk2_ref, q_ref, k_ref, v_ref, tab_ref, gsub_ref, o_ref,
               qs_ref, *scratch, tq, tk, hp, lambda_init, bounded):
    i = pl.program_id(2)
    lane = lax.broadcasted_iota(jnp.int32, (tq, HEAD_DIM), 1)
    streams = []
    for hh in range(hp):
        cols = slice(hh * HEAD_DIM, (hh + 1) * HEAD_DIM)
        q = q_ref[:, cols]
        zero = jnp.zeros_like(q)
        qs_ref[hh, :tq] = jnp.where(lane < DIFF_QK_DIM, q, zero)
        qs_ref[hh, tq:] = jnp.where(lane >= DIFF_QK_DIM, q, zero)

        def bias(s, j, hh=hh):
            tab = tab_ref[hh, jnp.minimum(i - j, FAR_TILE)]
            return s + jnp.concatenate([tab, tab], axis=0)

        streams.append(_Stream(
            qs_ref.at[hh],
            lambda rows, cols=cols: k_ref[rows, cols],
            lambda rows, cols=cols: v_ref[rows, cols],
            bias, tuple(r.at[hh] for r in scratch)))

    outs = _attend(streams, i, tk, bounded)

    lam = (jnp.exp(jnp.sum(lq1_ref[...] * lk1_ref[...], axis=-1, keepdims=True))
           - jnp.exp(jnp.sum(lq2_ref[...] * lk2_ref[...], axis=-1, keepdims=True)) + lambda_init)
    for hh, out in enumerate(outs):
        o = out[:tq] - lam * out[tq:]
        o_ref[:, hh * HEAD_DIM:(hh + 1) * HEAD_DIM] = (
            _rms_rows(o, gsub_ref[...]) * (1.0 - lambda_init)).astype(o_ref.dtype)


def _diff_attn(p3, tab, lq1, lk1, lq2, lk2, g_sub, lambda_init, tq, bounded):
    B, T, _ = p3.shape
    H = N_SELF_HEADS
    hp = 6 if bounded else 2
    nq = T // tq
    w = hp * HEAD_DIM
    vec = lambda a: a.reshape(1, -1).astype(F32)
    small = lambda n: pl.BlockSpec((1, n), lambda b, h, i: (0, 0))
    return pl.pallas_call(
        functools.partial(_diff_body, tq=tq, tk=tq, hp=hp, lambda_init=lambda_init, bounded=bounded),
        grid=(B, H // hp, nq),
        in_specs=[
            small(DIFF_QK_DIM), small(DIFF_QK_DIM), small(DIFF_QK_DIM), small(DIFF_QK_DIM),
            pl.BlockSpec((None, tq, w), lambda b, h, i: (b, i, h)),
            pl.BlockSpec((None, T, w), lambda b, h, i: (b, 0, H // hp + h)),
            pl.BlockSpec((None, T, w), lambda b, h, i: (b, 0, 2 * (H // hp) + h)),
            pl.BlockSpec((hp, FAR_TILE + 1, tq, tq), lambda b, h, i: (h, 0, 0, 0)),
            small(HEAD_DIM),
        ],
        out_specs=pl.BlockSpec((None, tq, w), lambda b, h, i: (b, i, h)),
        out_shape=jax.ShapeDtypeStruct((B, T, SELF_WIDTH), BF16),
        scratch_shapes=[pltpu.VMEM((hp, 2 * tq, HEAD_DIM), BF16)]
        + _attention_scratch(2 * tq, nq, tq, bounded, lead=(hp,)),
        compiler_params=_params(("parallel", "parallel", "arbitrary")),
        name="diff_attn",
    )(vec(lq1), vec(lk1), vec(lq2), vec(lk2), p3, p3, p3, tab, vec(g_sub))


SUBLANES = 8


def _sortable(x):
    bits = pltpu.bitcast(x, jnp.int32)
    return jnp.where(bits < 0, bits ^ jnp.int32(0x7FFFFFFF), bits)


def _dsa_select(iq_ref, ik_ref, iwt_ref, key_ref, madd_ref, iqm_ref, thr_ref, need_ref, i,
                *, tq, tk, topk):
    n_tiles = i + 1
    lane = lax.broadcasted_iota(jnp.int32, (tq, LANES), 1)
    for hh in range(IDX_HEADS):
        pair = iq_ref[:, (hh // 2) * LANES:(hh // 2 + 1) * LANES]
        keep = (lane < IDX_DIM) if hh % 2 == 0 else (lane >= IDX_DIM)
        iqm_ref[hh] = jnp.where(keep, pair, jnp.zeros_like(pair))

    key_pos = lax.broadcasted_iota(jnp.int32, (tk, tq), 0)
    qry_pos = lax.broadcasted_iota(jnp.int32, (tk, tq), 1)

    def causal(j):
        return (i * tq + qry_pos) >= (j * tk + key_pos)

    def score_tiles(j0, u, c):
        for j in [j0 + t for t in range(u)]:
            ik = ik_ref[_key_rows(j, tk), :]
            sc = jnp.zeros((tk, tq), F32)
            for hh in range(IDX_HEADS):
                sc = sc + jnp.maximum(_dot_nt(ik, iqm_ref[hh]), 0.0) * iwt_ref[hh:hh + 1, :]
            key_ref[j] = _sortable(jnp.where(causal(j), sc, NEG_INF))
        return c

    _fold_tile_groups(0, n_tiles, score_tiles, 0, groups=(4, 2, 1))

    kf = float(topk)

    def count(cmp, cand):
        def body(j0, u, part):
            for t in range(u):
                hit = jnp.where(cmp(key_ref[j0 + t], cand), 1.0, 0.0)
                part = part + jnp.sum(hit.reshape(tk // SUBLANES, SUBLANES, tq), axis=0)
            return part

        part = _fold_tile_groups(0, n_tiles, body, jnp.zeros((SUBLANES, tq), F32), groups=(4, 2, 1))
        return jnp.sum(part, axis=0, keepdims=True)

    ge = lambda k, c: k >= c
    zero = jnp.zeros((1, tq), jnp.int32)
    total = (n_tiles * tk).astype(F32)
    n_pos = count(ge, zero)
    state = (jnp.where(n_pos >= kf, zero, jnp.int32(INT32_MIN)), jnp.where(n_pos >= kf, n_pos, total))

    def bit_step(b, state):
        base, n_base = state
        cand = base | lax.shift_left(jnp.int32(1), jnp.int32(30) - b)
        n = count(ge, cand)
        return jnp.where(n >= kf, cand, base), jnp.where(n >= kf, n, n_base)

    def unsettled(n_base):
        return jnp.max(jnp.abs(n_base - kf))

    def search_on(c):
        return (c[0] < 31) & (c[3] > 0.0)

    def search_step(c):
        base, n_base = bit_step(c[0], (c[1], c[2]))
        return c[0] + 1, base, n_base, unsettled(n_base)

    _, thr, n_thr, _ = lax.while_loop(search_on, search_step,
                                      (jnp.int32(0), state[0], state[1], unsettled(state[1])))
    thr_ref[...] = jnp.broadcast_to(thr, thr_ref.shape)
    surplus = jnp.maximum(jnp.max(n_thr - kf), 0.0)

    @pl.when(surplus == 0.0)
    def _():
        def mask_tiles(j0, u):
            for j in [j0 + t for t in range(u)]:
                take = (key_ref[j] >= thr_ref[0:1, :]) & causal(j)
                madd_ref[j] = jnp.where(take, 0.0, NEG_INF).T

        _for_tile_groups(0, n_tiles, mask_tiles)

    @pl.when(surplus != 0.0)
    def _():
        need_ref[...] = jnp.broadcast_to(kf - count(lambda k, c: k > c, thr_ref[0:1, :]), need_ref.shape)
        lower = jnp.where(lax.broadcasted_iota(jnp.int32, (tk, tk), 1)
                          < lax.broadcasted_iota(jnp.int32, (tk, tk), 0), 1.0, 0.0).astype(BF16)

        def mask_tile(j, seen):
            key = key_ref[j]
            thr = thr_ref[0:1, :]
            eq = jnp.where(key == thr, 1.0, 0.0)
            rank = seen + _dot(lower, eq.astype(BF16))
            take = (key > thr) | ((key == thr) & (rank < need_ref[0:1, :]))
            madd_ref[j] = jnp.where(take & causal(j), 0.0, NEG_INF).T
            return seen + jnp.sum(eq, axis=0, keepdims=True)

        lax.fori_loop(0, n_tiles, mask_tile, jnp.zeros((1, tq), F32))


def _dsa_body(q_ref, k_ref, v_ref, iq_ref, ik_ref, iw_ref, tab_ref, o_ref,
              key_ref, madd_ref, iqm_ref, thr_ref, need_ref, qs_ref, *scratch,
              tq, tk, topk, gp, bounded):
    i = pl.program_id(1)
    R = DSA_GROUP

    @pl.when(pl.program_id(2) == 0)
    def _():
        _dsa_select(iq_ref, ik_ref, iw_ref, key_ref, madd_ref, iqm_ref, thr_ref, need_ref, i,
                    tq=tq, tk=tk, topk=topk)

    streams = []
    for g in range(gp):
        for r in range(R):
            h = g * R + r
            qs_ref[g, r * tq:(r + 1) * tq] = q_ref[:, h * HEAD_DIM:(h + 1) * HEAD_DIM]
        cols = slice(g * HEAD_DIM, (g + 1) * HEAD_DIM)

        def bias(s, j, g=g):
            back = jnp.minimum(i - j, FAR_TILE)
            return s + jnp.concatenate([madd_ref[j] + tab_ref[g * R + r, back] for r in range(R)], axis=0)

        streams.append(_Stream(
            qs_ref.at[g],
            lambda rows, cols=cols: k_ref[rows, cols],
            lambda rows, cols=cols: v_ref[rows, cols],
            bias, tuple(ref.at[g] for ref in scratch)))

    for g, out in enumerate(_attend(streams, i, tk, bounded)):
        for r in range(R):
            h = g * R + r
            o_ref[:, h * HEAD_DIM:(h + 1) * HEAD_DIM] = out[r * tq:(r + 1) * tq].astype(o_ref.dtype)


def _dsa_attn(p3, ik2, iw, tab, tq, bounded):
    B, T, _ = p3.shape
    H, R, G = N_SELF_HEADS, DSA_GROUP, DSA_KV_HEADS
    topk = min(TOPK_MAX, T // 4)
    nq = T // tq
    iq_w = IDX_HEADS * IDX_DIM
    gp = G if bounded else 1
    kw = gp * HEAD_DIM
    k0 = H * HEAD_DIM // kw
    iq0 = (H + G) * HEAD_DIM // iq_w
    v0 = ((H + G) * HEAD_DIM + iq_w) // kw
    assert (H + G) * HEAD_DIM % iq_w == 0
    return pl.pallas_call(
        functools.partial(_dsa_body, tq=tq, tk=tq, topk=topk, gp=gp, bounded=bounded),
        grid=(B, nq, G // gp),
        in_specs=[
            pl.BlockSpec((None, tq, gp * R * HEAD_DIM), lambda b, i, g: (b, i, g)),
            pl.BlockSpec((None, T, kw), lambda b, i, g: (b, 0, k0 + g)),
            pl.BlockSpec((None, T, kw), lambda b, i, g: (b, 0, v0 + g)),
            pl.BlockSpec((None, tq, iq_w), lambda b, i, g: (b, i, iq0)),
            pl.BlockSpec((None, T, LANES), lambda b, i, g: (b, 0, 0)),
            pl.BlockSpec((None, IDX_HEADS, tq), lambda b, i, g: (b, 0, i)),
            pl.BlockSpec((gp * R, FAR_TILE + 1, tq, tq), lambda b, i, g: (g, 0, 0, 0)),
        ],
        out_specs=pl.BlockSpec((None, tq, gp * R * HEAD_DIM), lambda b, i, g: (b, i, g)),
        out_shape=jax.ShapeDtypeStruct((B, T, SELF_WIDTH), BF16),
        scratch_shapes=[
            pltpu.VMEM((nq, tq, tq), jnp.int32),
            pltpu.VMEM((nq, tq, tq), F32),
            pltpu.VMEM((IDX_HEADS, tq, LANES), BF16),
            pltpu.VMEM((SUBLANES, tq), jnp.int32),
            pltpu.VMEM((SUBLANES, tq), F32),
            pltpu.VMEM((gp, R * tq, HEAD_DIM), BF16),
        ] + _attention_scratch(R * tq, nq, tq, bounded, lead=(gp,)),
        compiler_params=_params(("parallel", "arbitrary", "arbitrary")),
        name="dsa_attn",
    )(p3, p3, p3, p3, ik2, iw, tab)


def _out_body(ys_ref, qm_ref, km_ref, vm_ref, ws_ref, wm_ref, x_ref, o_ref):
    acc = x_ref[...] + _dot(ys_ref[...], ws_ref[...])
    heads = []
    for h in range(N_MEM_HEADS):
        sl = slice(h * HEAD_DIM, (h + 1) * HEAD_DIM)
        s = _dot_nt(qm_ref[:, sl], km_ref[:, sl])
        p = jnp.exp2(s - jnp.max(s, axis=-1, keepdims=True))
        o = _dot(p.astype(BF16), vm_ref[:, sl]) / jnp.sum(p, axis=-1, keepdims=True)
        heads.append(o.astype(BF16))
    o_ref[...] = acc + _dot(jnp.concatenate(heads, axis=1), wm_ref[...])


def _out_proj(ys, p2, qm_block, kv3, wo, layer, x2):
    M, D = x2.shape
    B, N, _ = kv3.shape
    tm = min(512, M // B)
    per_batch = M // B // tm
    return pl.pallas_call(
        _out_body,
        grid=(M // tm,),
        in_specs=[
            pl.BlockSpec((tm, SELF_WIDTH), lambda i: (i, 0)),
            pl.BlockSpec((tm, MEM_WIDTH), lambda i: (i, qm_block)),
            pl.BlockSpec((None, N, MEM_WIDTH), lambda i: (i // per_batch, 0, 0)),
            pl.BlockSpec((None, N, MEM_WIDTH), lambda i: (i // per_batch, 0, 1)),
            pl.BlockSpec((None, SELF_WIDTH, D), lambda i: (layer, 0, 0)),
            pl.BlockSpec((None, MEM_WIDTH, D), lambda i: (layer, SELF_WIDTH // MEM_WIDTH, 0)),
            pl.BlockSpec((tm, D), lambda i: (i, 0)),
        ],
        out_specs=pl.BlockSpec((tm, D), lambda i: (i, 0)),
        out_shape=jax.ShapeDtypeStruct((M, D), F32),
        compiler_params=_params(("parallel",)),
        name="out_proj",
    )(ys, p2, kv3, kv3, wo, wo, x2)


def _scores_bounded(gq, gk, dim, scale, rel_bias):
    qk = dim * jnp.max(jnp.abs(gq)) * jnp.max(jnp.abs(gk)) * (scale * LOG2E * 1.02)
    bias = jnp.max(jnp.abs(rel_bias - rel_bias[NUM_BUCKETS - 1:])) * LOG2E
    return qk + bias <= MAX_EXP2_ARG


def kernel(x, mem, rel_bias, ffn1_g, ffn1_w_gate, ffn1_w_up, ffn1_w_down, ffn2_g, ffn2_w_gate,
           ffn2_w_up, ffn2_w_down, mix_g, mem_g, mem_w_kv, mem_gq, mem_gk, w_out, a_w_in, a_gq, a_gk,
           a_lam_q1, a_lam_k1, a_lam_q2, a_lam_k2, a_g_sub, b_w_in, b_gq, b_gk):
    B, T, D = x.shape
    N = mem.shape[1]
    depth = ffn1_g.shape[0]
    M = B * T
    tq = min(256, T)
    head_scale = HEAD_DIM ** -0.5 * LOG2E
    ones = lambda n: jnp.ones((n,), F32)
    bf = lambda w: w.astype(BF16)

    x2 = x.reshape(M, D)
    mem2 = mem.reshape(B * N, D)
    tab = _bias_tiles(rel_bias, tq, tq)
    ffn_w = [(bf(ffn1_w_gate), bf(ffn1_w_up), bf(0.5 * ffn1_w_down)),
             (bf(ffn2_w_gate), bf(ffn2_w_up), bf(0.5 * ffn2_w_down))]
    ffn_g = [ffn1_g, ffn2_g]
    wo, w_kv, w_a = bf(w_out), bf(mem_w_kv), bf(a_w_in)

    for i in range(depth):
        x2 = _ffn(x2, ffn_g[0][i], *ffn_w[0], i)

        kv_gain = jnp.concatenate([jnp.tile(mem_gk[i], N_MEM_HEADS), ones(MEM_WIDTH)])
        kv3 = _proj(mem2, mem_g[i], w_kv, i, [2, 0], kv_gain, BF16, MEM_WIDTH)
        kv3 = kv3.reshape(B, N, 2 * MEM_WIDTH)
        qm_gain = jnp.tile(mem_gq[i], N_MEM_HEADS) * head_scale
        j = i // 2

        if i % 2 == 0:
            n_grp = SELF_WIDTH // DIFF_QK_DIM
            gain = jnp.concatenate([jnp.tile(a_gq[j], n_grp) * (DIFF_QK_DIM ** -0.5 * LOG2E),
                                    jnp.tile(a_gk[j], n_grp), ones(SELF_WIDTH), qm_gain])
            kinds = [1] * 6 + [0] * 3 + [2]
            p3 = _proj(x2, mix_g[i], w_a, j, kinds, gain, BF16, 512).reshape(B, T, -1)
            lambda_init = 0.8 - 0.6 * math.exp(-0.3 * i)
            attn = functools.partial(_diff_attn, p3, tab, a_lam_q1[j], a_lam_k1[j], a_lam_q2[j],
                                     a_lam_k2[j], a_g_sub[j], lambda_init, tq)
            bounded = _scores_bounded(a_gq[j], a_gk[j], DIFF_QK_DIM, DIFF_QK_DIM ** -0.5, rel_bias)
            qm_block = 3 * SELF_WIDTH // MEM_WIDTH
        else:
            w = b_w_in[j]
            kvw = DSA_KV_HEADS * HEAD_DIM
            iqw = IDX_HEADS * IDX_DIM
            c_q, c_k, c_v, c_iq = 0, SELF_WIDTH, SELF_WIDTH + kvw, SELF_WIDTH + 2 * kvw
            c_ik = c_iq + iqw
            c_iw = c_ik + IDX_DIM
            c_qm = c_iw + IDX_HEADS
            w_main = jnp.concatenate([w[:, c_q:c_k], w[:, c_k:c_v], w[:, c_iq:c_ik], w[:, c_v:c_iq],
                                      w[:, c_qm:]], axis=1)
            gain = jnp.concatenate([jnp.tile(b_gq[j], N_SELF_HEADS) * head_scale,
                                    jnp.tile(b_gk[j], DSA_KV_HEADS), ones(iqw), ones(kvw), qm_gain])
            kinds = [2, 2, 2, 2, 0, 0, 0, 2]
            pad = LANES - IDX_DIM - IDX_HEADS
            w_idx = jnp.concatenate([w[:, c_ik:c_qm], jnp.zeros((D, pad), F32)], axis=1)
            idx_gain = jnp.concatenate([ones(IDX_DIM),
                                        jnp.full((IDX_HEADS,), (IDX_DIM * IDX_HEADS) ** -0.5, F32),
                                        jnp.zeros((pad,), F32)])
            p3, ikw = _proj(x2, mix_g[i], bf(w_main)[None], 0, kinds, gain, BF16, 512,
                            side=(bf(w_idx), idx_gain))
            p3, ikw = p3.reshape(B, T, -1), ikw.reshape(B, T, LANES)
            ik = ikw[:, :, :IDX_DIM].astype(BF16)
            ik2 = jnp.concatenate([ik, ik], axis=-1)
            iw = jnp.swapaxes(ikw[:, :, IDX_DIM:IDX_DIM + IDX_HEADS], 1, 2)
            attn = functools.partial(_dsa_attn, p3, ik2, iw, tab, tq)
            bounded = _scores_bounded(b_gq[j], b_gk[j], HEAD_DIM, HEAD_DIM ** -0.5, rel_bias)
            qm_block = (p3.shape[-1] - MEM_WIDTH) // MEM_WIDTH

        y_self = lax.cond(bounded, lambda: attn(True), lambda: attn(False))
        x2 = _out_proj(y_self.reshape(M, SELF_WIDTH), p3.reshape(M, -1), qm_block, kv3, wo, i, x2)

        x2 = _ffn(x2, ffn_g[1][i], *ffn_w[1], i)

    return x2.reshape(B, T, D)
```
